```python
import math
import jax, jax.numpy as jnp
from jax import lax
import numpy as np

D_MODEL = 1024
BATCH = 32
SEQ = 256
DEPTH = 2
DEC_BATCH = 8
DEC_SEQ = 2048
PAST_LEN = 512

GRID_W = 64
MOD_CHUNKS = 6
EPS = 1e-6
POOL_WIDTH = D_MODEL // 2
POOL_GROUPS = 4
POOL_GROUP_W = POOL_WIDTH // POOL_GROUPS
POOL_WINDOWS = (2, 4, 8, 16)
N_Q_HEADS = 8
N_KV_HEADS = 2
GQA_GROUP = N_Q_HEADS // N_KV_HEADS
HEAD_DIM = 64
ATTN_WIDTH = N_Q_HEADS * HEAD_DIM
KV_WIDTH = N_KV_HEADS * HEAD_DIM
WINDOW = 128
BLOCK = 128
ROPE_BASE = 10000.0
ATTN_SCALE = HEAD_DIM ** -0.5
SSD_HEADS = 16
SSD_HEADDIM = 64
SSD_INNER = SSD_HEADS * SSD_HEADDIM
SSD_GROUPS = 2
HEADS_PER_GROUP = SSD_HEADS // SSD_GROUPS
D_STATE = 128
CONV_W = 5
CHUNK = 128
CONV_CH = SSD_INNER + 2 * SSD_GROUPS * D_STATE
N_BRANCH = 3
D_FF = 4 * D_MODEL
IN_SIZES = (POOL_WIDTH, ATTN_WIDTH, KV_WIDTH, KV_WIDTH, SSD_INNER, CONV_CH, 2 * SSD_HEADS, N_BRANCH * D_MODEL)
IN_COLS = sum(IN_SIZES)

kernel_name = "hybrid_pool_swa_ssd_prefix_dit_step"


def rmsnorm(x, g):
    xf = x.astype(jnp.float32)
    y = xf * lax.rsqrt(jnp.mean(xf * xf, axis=-1, keepdims=True) + EPS)
    return (y * g.astype(jnp.float32)).astype(x.dtype)


def modulation(cvec, w_mod, b_mod):
    m = (jax.nn.silu(cvec) @ w_mod + b_mod)[..., None, :]
    return jnp.split(m, MOD_CHUNKS, axis=-1)


def in_projection(h, w_in):
    proj = h @ w_in
    offsets = []
    acc = 0
    for s in IN_SIZES[:-1]:
        acc += s
        offsets.append(acc)
    return jnp.split(proj, offsets, axis=-1)


def pool_mixer(u, pool_w, pool_scale):
    b, L, _ = u.shape
    cs = jnp.concatenate([jnp.zeros((b, 1, POOL_WIDTH), jnp.float32), jnp.cumsum(u.astype(jnp.float32), axis=1)], axis=1)
    t = jnp.arange(L)
    outs = []
    for gi, w in enumerate(POOL_WINDOWS):
        lo = jnp.clip(t - w // 2, 0, L)
        hi = jnp.clip(t + w // 2, 0, L)
        csg = cs[:, :, gi * POOL_GROUP_W:(gi + 1) * POOL_GROUP_W]
        cnt = (hi - lo).astype(jnp.float32)[None, :, None]
        outs.append((csg[:, hi] - csg[:, lo]) / cnt)
    pooled = jnp.concatenate(outs, axis=-1).astype(u.dtype) - u
    pooled = pooled.reshape(b, L, POOL_GROUPS, POOL_GROUP_W)
    mixed = jnp.einsum('blgc,gcd->blgd', pooled, pool_w).reshape(b, L, POOL_WIDTH)
    return mixed * pool_scale


def axial_rope(x):
    L = x.shape[1]
    rows = L // GRID_W
    row = jnp.repeat(jnp.arange(rows), GRID_W).astype(jnp.float32)
    col = jnp.tile(jnp.arange(GRID_W), rows).astype(jnp.float32)
    half = HEAD_DIM // 2
    nf = half // 2
    inv = ROPE_BASE ** (-jnp.arange(nf, dtype=jnp.float32) / nf)

    def rot(xa, pos):
        ang = pos[:, None] * inv[None, :]
        cos = jnp.cos(ang)[None, :, None, :]
        sin = jnp.sin(ang)[None, :, None, :]
        x1, x2 = xa[..., :nf], xa[..., nf:]
        return jnp.concatenate([x1 * cos - x2 * sin, x1 * sin + x2 * cos], axis=-1)

    xf = x.astype(jnp.float32)
    return jnp.concatenate([rot(xf[..., :half], row), rot(xf[..., half:], col)], axis=-1).astype(x.dtype)


def context_attention(q, k, v, sink):
    b, L = q.shape[:2]
    nb = L // BLOCK
    qb = q.reshape(b, nb, BLOCK, N_KV_HEADS, GQA_GROUP, HEAD_DIM).transpose(1, 0, 2, 3, 4, 5)
    sink_b = sink.astype(jnp.float32).reshape(1, N_KV_HEADS, GQA_GROUP, 1, 1)

    def one_block(qblk):
        s = jnp.einsum('bqkgd,bskd->bkgqs', qblk, k).astype(jnp.float32) * ATTN_SCALE
        m = jnp.maximum(jnp.max(s, axis=-1, keepdims=True), sink_b)
        p = jnp.exp(s - m)
        denom = jnp.sum(p, axis=-1, keepdims=True) + jnp.exp(sink_b - m)
        o = jnp.einsum('bkgqs,bskd->bqkgd', (p / denom).astype(v.dtype), v)
        return o.reshape(b, BLOCK, ATTN_WIDTH)

    out = lax.map(one_block, qb)
    return out.transpose(1, 0, 2, 3).reshape(b, L, ATTN_WIDTH)


def latent_attention(q, k, v, k_ctx, v_ctx, sink):
    b, L = q.shape[:2]
    nb = L // BLOCK
    qb = q.reshape(b, nb, BLOCK, N_KV_HEADS, GQA_GROUP, HEAD_DIM)
    pad = ((0, 0), (BLOCK, BLOCK), (0, 0), (0, 0))
    kp = jnp.pad(k, pad).reshape(b, nb + 2, BLOCK, N_KV_HEADS, HEAD_DIM)
    vp = jnp.pad(v, pad).reshape(b, nb + 2, BLOCK, N_KV_HEADS, HEAD_DIM)
    kw = jnp.concatenate([kp[:, :-2], kp[:, 1:-1], kp[:, 2:]], axis=2)
    vw = jnp.concatenate([vp[:, :-2], vp[:, 1:-1], vp[:, 2:]], axis=2)
    qpos = jnp.arange(nb)[:, None] * BLOCK + jnp.arange(BLOCK)[None, :]
    kpos = (jnp.arange(nb)[:, None] - 1) * BLOCK + jnp.arange(3 * BLOCK)[None, :]
    valid = (jnp.abs(qpos[:, :, None] - kpos[:, None, :]) <= WINDOW) & ((kpos >= 0) & (kpos < L))[:, None, :]
    s_loc = jnp.einsum('bnqkgd,bnskd->bnkgqs', qb, kw).astype(jnp.float32) * ATTN_SCALE
    s_loc = jnp.where(valid[None, :, None, None], s_loc, -jnp.inf)
    s_ctx = jnp.einsum('bnqkgd,bskd->bnkgqs', qb, k_ctx).astype(jnp.float32) * ATTN_SCALE
    sink_b = sink.astype(jnp.float32).reshape(1, 1, N_KV_HEADS, GQA_GROUP, 1, 1)
    m = jnp.maximum(jnp.maximum(jnp.max(s_loc, axis=-1, keepdims=True), jnp.max(s_ctx, axis=-1, keepdims=True)), sink_b)
    p_loc = jnp.exp(s_loc - m)
    p_ctx = jnp.exp(s_ctx - m)
    denom = jnp.sum(p_loc, axis=-1, keepdims=True) + jnp.sum(p_ctx, axis=-1, keepdims=True) + jnp.exp(sink_b - m)
    o = (jnp.einsum('bnkgqs,bnskd->bnqkgd', (p_loc / denom).astype(v.dtype), vw)
         + jnp.einsum('bnkgqs,bskd->bnqkgd', (p_ctx / denom).astype(v.dtype), v_ctx))
    return o.reshape(b, L, ATTN_WIDTH)


def depthwise_conv(u, w, bias):
    pad = CONV_W // 2
    out = lax.conv_general_dilated(u, w[:, None, :], window_strides=(1,), padding=((pad, pad),),
                                   dimension_numbers=('NWC', 'WIO', 'NWC'), feature_group_count=u.shape[-1])
    return out + bias


def ssd_chunked(x, dt, A, B, C, init_state):
    b, L = x.shape[:2]
    nc = L // CHUNK
    G, HG, P, N = SSD_GROUPS, HEADS_PER_GROUP, SSD_HEADDIM, D_STATE
    xc = x.astype(jnp.float32).reshape(b, nc, CHUNK, G, HG, P)
    dtc = dt.reshape(b, nc, CHUNK, G, HG)
    Bc = B.astype(jnp.float32).reshape(b, nc, CHUNK, G, N)
    Cc = C.astype(jnp.float32).reshape(b, nc, CHUNK, G, N)
    acum = jnp.moveaxis(jnp.cumsum(dtc * A.reshape(G, HG), axis=2), 2, -1)
    causal = jnp.tril(jnp.ones((CHUNK, CHUNK), dtype=bool))
    seg = jnp.where(causal, acum[..., :, None] - acum[..., None, :], -jnp.inf)
    cb = jnp.einsum('bcign,bcjgn->bcgij', Cc, Bc)
    w_intra = cb[:, :, :, None] * jnp.exp(seg)
    dtx = xc * dtc[..., None]
    y_diag = jnp.einsum('bcghij,bcjghp->bcighp', w_intra, dtx)
    decay_to_end = jnp.exp(acum[..., -1:] - acum)
    chunk_states = jnp.einsum('bcghj,bcjgn,bcjghp->bcghpn', decay_to_end, Bc, dtx)
    chunk_decay = jnp.exp(acum[..., -1])

    def step(S, inp):
        st, dec = inp
        return S * dec[..., None, None] + st, S

    S0 = init_state.astype(jnp.float32).reshape(b, G, HG, P, N)
    S_final, S_in = lax.scan(step, S0, (jnp.moveaxis(chunk_states, 1, 0), jnp.moveaxis(chunk_decay, 1, 0)))
    S_in = jnp.moveaxis(S_in, 0, 1)
    y_off = jnp.einsum('bcign,bcghpn,bcghi->bcighp', Cc, S_in, jnp.exp(acum))
    y = (y_diag + y_off).reshape(b, L, SSD_HEADS, P)
    return y, S_final.reshape(b, SSD_HEADS, P, N)


def ssd_mixer(z, xbc, dt_raw, init_f, init_b, p):
    b, L, _ = xbc.shape
    xbc = jax.nn.silu(depthwise_conv(xbc, p['conv_w'], p['conv_b']))
    xs, Bs, Cs = jnp.split(xbc, [SSD_INNER, SSD_INNER + SSD_GROUPS * D_STATE], axis=-1)
    x = xs.reshape(b, L, SSD_HEADS, SSD_HEADDIM)
    B = Bs.reshape(b, L, SSD_GROUPS, D_STATE)
    C = Cs.reshape(b, L, SSD_GROUPS, D_STATE)
    dtb = p['dt_bias'].astype(jnp.float32)
    A = -jnp.exp(p['a_log'].astype(jnp.float32))
    dt_f_raw, dt_b_raw = jnp.split(dt_raw.astype(jnp.float32), 2, axis=-1)
    dt_f = jax.nn.softplus(dt_f_raw + dtb[0])
    dt_b = jax.nn.softplus(dt_b_raw + dtb[1])
    y_f, s_f = ssd_chunked(x, dt_f, A[0], B, C, init_f)
    y_b, s_b = ssd_chunked(x[:, ::-1], dt_b[:, ::-1], A[1], B[:, ::-1], C[:, ::-1], init_b)
    y = y_f + y_b[:, ::-1] + p['d_skip'].astype(jnp.float32)[:, None] * x.astype(jnp.float32)
    y = y.reshape(b, L, SSD_INNER) * jax.nn.silu(z.astype(jnp.float32))
    return rmsnorm(y, p['ssd_norm_g']).astype(z.dtype), s_f, s_b


def token_mixers(h, p, ctx):
    b, L, _ = h.shape
    pool_u, q, k, v, z, xbc, dt_raw, gate_logits = in_projection(h, p['w_in'])
    pool_out = pool_mixer(pool_u, p['pool_w'], p['pool_scale'])
    q = rmsnorm(q.reshape(b, L, N_Q_HEADS, HEAD_DIM), p['q_norm_g'])
    k = rmsnorm(k.reshape(b, L, N_KV_HEADS, HEAD_DIM), p['k_norm_g'])
    v = v.reshape(b, L, N_KV_HEADS, HEAD_DIM)
    if ctx is None:
        attn_out = context_attention(q, k, v, p['attn_sink'])
        init_f = jnp.zeros((b, SSD_HEADS, SSD_HEADDIM, D_STATE), jnp.float32)
        init_b = init_f
    else:
        k_ctx, v_ctx, s_ctx = ctx
        attn_out = latent_attention(axial_rope(q), axial_rope(k), v, k_ctx, v_ctx, p['attn_sink'])
        init_f, init_b = s_ctx[:, 0], s_ctx[:, 1]
    ssd_out, s_f, s_b = ssd_mixer(z, xbc, dt_raw, init_f, init_b, p)
    g_pool, g_attn, g_ssd = jnp.split(jax.nn.sigmoid(gate_logits), N_BRANCH, axis=-1)
    merged = (g_pool * (pool_out @ p['w_pool_o']) + g_attn * (attn_out @ p['w_attn_o'])
              + g_ssd * (ssd_out @ p['w_ssd_o']))
    out = merged @ p['w_out']
    if ctx is None:
        return out, (k, v, jnp.stack([s_f, s_b], axis=1).astype(h.dtype))
    return out, None


def layer(x, cvec, p, ctx):
    sh1, sc1, g1, sh2, sc2, g2 = modulation(cvec, p['w_mod'], p['b_mod'])
    h = rmsnorm(x, p['norm1_g']) * (1.0 + sc1) + sh1
    mix, ctx_tensors = token_mixers(h, p, ctx)
    x = x + g1 * mix
    h = rmsnorm(x, p['norm2_g']) * (1.0 + sc2) + sh2
    x = x + g2 * (jnp.square(jax.nn.relu(h @ p['w_mlp1'])) @ p['w_mlp2'])
    return x, ctx_tensors


def setup_inputs(seed: int = 0) -> dict:
    key = jax.random.key(seed)
    ks = jax.random.split(key, 40)

    def nrm(k, shape, scale):
        return jax.random.normal(k, shape, jnp.float32) * scale

    dt_init = jnp.exp(jax.random.uniform(ks[30], (DEPTH, 2, SSD_HEADS), jnp.float32, math.log(1e-3), math.log(1e-1)))
    return {
        'x_prompt': nrm(ks[0], (BATCH, SEQ, D_MODEL), 1.0),
        'x_sample': nrm(ks[1], (DEC_BATCH, DEC_SEQ, D_MODEL), 1.0),
        'cache_k': nrm(ks[2], (DEC_BATCH, DEPTH, PAST_LEN, N_KV_HEADS, HEAD_DIM), 1.0),
        'cache_v': nrm(ks[3], (DEC_BATCH, DEPTH, PAST_LEN, N_KV_HEADS, HEAD_DIM), 1.0),
        'state_ssd': nrm(ks[4], (DEC_BATCH, DEPTH, 2, SSD_HEADS, SSD_HEADDIM, D_STATE), 0.1),
        'c': nrm(ks[5], (DEC_BATCH, D_MODEL), 1.0),
        'c_ctx': nrm(ks[6], (D_MODEL,), 1.0),
        'w_mod': nrm(ks[7], (DEPTH, D_MODEL, MOD_CHUNKS * D_MODEL), 0.5 * D_MODEL ** -0.5),
        'b_mod': nrm(ks[8], (DEPTH, MOD_CHUNKS * D_MODEL), 0.02),
        'norm1_g': 1.0 + nrm(ks[9], (DEPTH, D_MODEL), 0.02),
        'norm2_g': 1.0 + nrm(ks[10], (DEPTH, D_MODEL), 0.02),
        'w_in': nrm(ks[11], (DEPTH, D_MODEL, IN_COLS), D_MODEL ** -0.5),
        'pool_w': nrm(ks[12], (DEPTH, POOL_GROUPS, POOL_GROUP_W, POOL_GROUP_W), POOL_GROUP_W ** -0.5),
        'pool_scale': 1.0 + nrm(ks[13], (DEPTH, POOL_WIDTH), 0.02),
        'w_pool_o': nrm(ks[14], (DEPTH, POOL_WIDTH, D_MODEL), POOL_WIDTH ** -0.5),
        'q_norm_g': 1.0 + nrm(ks[15], (DEPTH, HEAD_DIM), 0.02),
        'k_norm_g': 1.0 + nrm(ks[16], (DEPTH, HEAD_DIM), 0.02),
        'attn_sink': nrm(ks[17], (DEPTH, N_Q_HEADS), 1.0),
        'w_attn_o': nrm(ks[18], (DEPTH, ATTN_WIDTH, D_MODEL), ATTN_WIDTH ** -0.5),
        'conv_w': nrm(ks[19], (DEPTH, CONV_W, CONV_CH), CONV_W ** -0.5),
        'conv_b': nrm(ks[20], (DEPTH, CONV_CH), 0.02),
        'dt_bias': dt_init + jnp.log(-jnp.expm1(-dt_init)),
        'a_log': jnp.log(jax.random.uniform(ks[21], (DEPTH, 2, SSD_HEADS), jnp.float32, 1.0, 16.0)),
        'd_skip': 1.0 + nrm(ks[22], (DEPTH, SSD_HEADS), 0.1),
        'ssd_norm_g': 1.0 + nrm(ks[23], (DEPTH, SSD_INNER), 0.02),
        'w_ssd_o': nrm(ks[24], (DEPTH, SSD_INNER, D_MODEL), SSD_INNER ** -0.5),
        'w_out': nrm(ks[25], (DEPTH, D_MODEL, D_MODEL), D_MODEL ** -0.5),
        'w_mlp1': nrm(ks[26], (DEPTH, D_MODEL, D_FF), D_MODEL ** -0.5),
        'w_mlp2': nrm(ks[27], (DEPTH, D_FF, D_MODEL), D_FF ** -0.5),
    }


def reference(x_prompt, x_sample, cache_k, cache_v, state_ssd, c, c_ctx, w_mod, b_mod, norm1_g, norm2_g,
              w_in, pool_w, pool_scale, w_pool_o, q_norm_g, k_norm_g, attn_sink, w_attn_o, conv_w, conv_b,
              dt_bias, a_log, d_skip, ssd_norm_g, w_ssd_o, w_out, w_mlp1, w_mlp2):
    y_prompt = x_prompt
    y_sample = x_sample
    ks_out, vs_out, ss_out = [], [], []
    for l in range(DEPTH):
        p = {
            'w_mod': w_mod[l], 'b_mod': b_mod[l], 'norm1_g': norm1_g[l], 'norm2_g': norm2_g[l],
            'w_in': w_in[l], 'pool_w': pool_w[l], 'pool_scale': pool_scale[l], 'w_pool_o': w_pool_o[l],
            'q_norm_g': q_norm_g[l], 'k_norm_g': k_norm_g[l], 'attn_sink': attn_sink[l], 'w_attn_o': w_attn_o[l],
            'conv_w': conv_w[l], 'conv_b': conv_b[l], 'dt_bias': dt_bias[l], 'a_log': a_log[l],
            'd_skip': d_skip[l], 'ssd_norm_g': ssd_norm_g[l], 'w_ssd_o': w_ssd_o[l], 'w_out': w_out[l],
            'w_mlp1': w_mlp1[l], 'w_mlp2': w_mlp2[l],
        }
        y_prompt, (k_l, v_l, s_l) = layer(y_prompt, c_ctx, p, None)
        ks_out.append(k_l)
        vs_out.append(v_l)
        ss_out.append(s_l)
        y_sample, _ = layer(y_sample, c, p, (cache_k[:, l], cache_v[:, l], state_ssd[:, l]))
    new_cache_k = jnp.stack(ks_out, axis=1)
    new_cache_v = jnp.stack(vs_out, axis=1)
    new_state_ssd = jnp.stack(ss_out, axis=1)
    return (y_prompt, y_sample, new_cache_k, new_cache_v, new_state_ssd)
```

```python
import functools
import math

import jax
import jax.numpy as jnp
from jax import lax
from jax.experimental import pallas as pl
from jax.experimental.pallas import tpu as pltpu

F32 = jnp.float32
BF16 = jnp.bfloat16
HIGHEST = lax.Precision.HIGHEST

D_MODEL = 1024
DEPTH = 2
GRID_W = 64
MOD_CHUNKS = 6
EPS = 1e-6
POOL_WIDTH = D_MODEL // 2
POOL_GROUPS = 4
POOL_GROUP_W = POOL_WIDTH // POOL_GROUPS
POOL_WINDOWS = (2, 4, 8, 16)
N_Q_HEADS = 8
N_KV_HEADS = 2
GQA_GROUP = N_Q_HEADS // N_KV_HEADS
HEAD_DIM = 64
ATTN_WIDTH = N_Q_HEADS * HEAD_DIM
KV_WIDTH = N_KV_HEADS * HEAD_DIM
WINDOW = 128
BLOCK = 128
ROPE_BASE = 10000.0
ATTN_SCALE = HEAD_DIM ** -0.5
SSD_HEADS = 16
SSD_HEADDIM = 64
SSD_INNER = SSD_HEADS * SSD_HEADDIM
SSD_GROUPS = 2
HEADS_PER_GROUP = SSD_HEADS // SSD_GROUPS
D_STATE = 128
CONV_W = 5
CHUNK = 128
CONV_CH = SSD_INNER + 2 * SSD_GROUPS * D_STATE
N_BRANCH = 3
D_FF = 4 * D_MODEL

LANES = 128
HALO = 8
MOD_ROWS = 16
VMEM_LIMIT = 56 * 1024 * 1024
TOKEN_TILE = 256
NEG_INF = float("-inf")


def _params(*sem):
    return pltpu.CompilerParams(dimension_semantics=sem, vmem_limit_bytes=VMEM_LIMIT)


def _const_spec(shape):
    nd = len(shape)
    return pl.BlockSpec(shape, lambda *_: (0,) * nd)


def _bdot(a, b):
    return jnp.dot(a.astype(BF16), b.astype(BF16), preferred_element_type=F32)


def _bdot_nt(a, b):
    return lax.dot_general(a.astype(BF16), b.astype(BF16), (((1,), (1,)), ((), ())),
                           preferred_element_type=F32)


def _sigmoid(x):
    return 1.0 / (1.0 + jnp.exp(-x))


def _silu(x):
    return x * _sigmoid(x)


def _softplus(x):
    return jnp.maximum(x, 0.0) + jnp.log1p(jnp.exp(-jnp.abs(x)))


def _rms_mod(x, g, scale, shift):
    ms = jnp.mean(x * x, axis=-1, keepdims=True)
    return (x * lax.rsqrt(ms + EPS)) * g * (1.0 + scale) + shift


def _mod_kernel(c_ref, w_ref, b_ref, o_ref):
    o_ref[0] = _bdot(_silu(c_ref[...]), w_ref[0]) + b_ref[0]


def _modulation(cvecs, w_mod, b_mod):
    n = MOD_CHUNKS * D_MODEL
    tn = n // 4
    out = pl.pallas_call(
        _mod_kernel,
        grid=(DEPTH, n // tn),
        in_specs=[pl.BlockSpec((MOD_ROWS, D_MODEL), lambda l, j: (0, 0)),
                  pl.BlockSpec((1, D_MODEL, tn), lambda l, j: (l, 0, j)),
                  pl.BlockSpec((1, 1, tn), lambda l, j: (l, 0, j))],
        out_specs=pl.BlockSpec((1, MOD_ROWS, tn), lambda l, j: (l, 0, j)),
        out_shape=jax.ShapeDtypeStruct((DEPTH, MOD_ROWS, n), F32),
        compiler_params=_params("arbitrary", "arbitrary"),
        name="modulation",
    )(cvecs, w_mod, b_mod.reshape(DEPTH, 1, n))
    return out.reshape(DEPTH * MOD_ROWS, 1, n)


def _mod_spec(layer, per_batch):
    base = layer * MOD_ROWS
    if per_batch:
        return pl.BlockSpec((1, 1, MOD_CHUNKS * D_MODEL), lambda b, i: (base + 1 + b, 0, 0))
    return pl.BlockSpec((1, 1, MOD_CHUNKS * D_MODEL), lambda b, i: (base, 0, 0))


DT_PAD = 2 * LANES


def _inproj_kernel(x_ref, mod_ref, g_ref, wa_ref, wz_ref, wx_ref, wd_ref, wg_ref,
                   u_ref, q_ref, k_ref, v_ref, z_ref, xbc_ref, dt_ref, gate_ref):
    mod = mod_ref[0]
    h = _rms_mod(x_ref[0], g_ref[...], mod[:, D_MODEL:2 * D_MODEL], mod[:, 0:D_MODEL]).astype(BF16)
    a = jnp.dot(h, wa_ref[...], preferred_element_type=F32)
    u_ref[0] = a[:, 0:POOL_WIDTH]
    q_ref[0] = a[:, POOL_WIDTH:POOL_WIDTH + ATTN_WIDTH]
    k_ref[0] = a[:, POOL_WIDTH + ATTN_WIDTH:POOL_WIDTH + ATTN_WIDTH + KV_WIDTH]
    v_ref[0] = a[:, POOL_WIDTH + ATTN_WIDTH + KV_WIDTH:]
    z_ref[0] = jnp.dot(h, wz_ref[...], preferred_element_type=F32)
    xbc_ref[0] = jnp.dot(h, wx_ref[...], preferred_element_type=F32)
    dt_ref[0] = jnp.dot(h, wd_ref[...], preferred_element_type=F32)
    gate_ref[0] = jnp.dot(h, wg_ref[...], preferred_element_type=F32)


def _in_projection(x, mods, layer, per_batch, norm_g, w):
    b, L, _ = x.shape
    tm = TOKEN_TILE
    widths = (POOL_WIDTH, ATTN_WIDTH, KV_WIDTH, KV_WIDTH, SSD_INNER, CONV_CH, DT_PAD, N_BRANCH * D_MODEL)
    tok = lambda n: pl.BlockSpec((1, tm, n), lambda bi, i: (bi, i, 0))
    return pl.pallas_call(
        _inproj_kernel,
        grid=(b, L // tm),
        in_specs=[tok(D_MODEL), _mod_spec(layer, per_batch), _const_spec((1, D_MODEL)),
                  _const_spec(w["a"].shape), _const_spec(w["z"].shape), _const_spec(w["xbc"].shape),
                  _const_spec(w["dt"].shape), _const_spec(w["gate"].shape)],
        out_specs=[tok(n) for n in widths],
        out_shape=[jax.ShapeDtypeStruct((b, L, n), F32) for n in widths],
        compiler_params=_params("parallel", "parallel"),
        name="in_projection",
    )(x, mods, norm_g, w["a"], w["z"], w["xbc"], w["dt"], w["gate"])


def _halo_specs(rows, cols, L):
    per = rows // HALO
    last = L // HALO - 1
    main = pl.BlockSpec((1, rows, cols), lambda b, i: (b, i, 0))
    prev = pl.BlockSpec((1, HALO, cols), lambda b, i: (b, jnp.maximum(i * per - 1, 0), 0))
    nxt = pl.BlockSpec((1, HALO, cols), lambda b, i: (b, jnp.minimum((i + 1) * per, last), 0))
    return main, prev, nxt


def _pool_kernel(u_ref, up_ref, un_ref, pw_ref, ps_ref, o_ref, pad_ref, *, rows, L):
    i = pl.program_id(1)
    n = pl.num_programs(1)
    pad_ref[0:HALO, :] = jnp.where(i > 0, up_ref[0], 0.0)
    pad_ref[HALO:HALO + rows, :] = u_ref[0]
    pad_ref[HALO + rows:2 * HALO + rows, :] = jnp.where(i < n - 1, un_ref[0], 0.0)
    t = i * rows + lax.broadcasted_iota(jnp.int32, (rows, POOL_GROUP_W), 0)
    for gi, w in enumerate(POOL_WINDOWS):
        cols = slice(gi * POOL_GROUP_W, (gi + 1) * POOL_GROUP_W)
        acc = pad_ref[HALO - w // 2:HALO - w // 2 + rows, cols]
        for d in range(-w // 2 + 1, w // 2):
            acc = acc + pad_ref[HALO + d:HALO + d + rows, cols]
        cnt = (jnp.minimum(t + w // 2, L) - jnp.maximum(t - w // 2, 0)).astype(F32)
        pooled = acc / cnt - pad_ref[HALO:HALO + rows, cols]
        o_ref[0, :, cols] = _bdot(pooled, pw_ref[gi]) * ps_ref[:, cols]


def _pool_mixer(u, pool_w, pool_scale):
    b, L, _ = u.shape
    rows = 256
    assert max(POOL_WINDOWS) // 2 <= HALO and L % rows == 0
    main, prev, nxt = _halo_specs(rows, POOL_WIDTH, L)
    return pl.pallas_call(
        functools.partial(_pool_kernel, rows=rows, L=L),
        grid=(b, L // rows),
        in_specs=[main, prev, nxt, _const_spec(pool_w.shape), _const_spec(pool_scale.shape)],
        out_specs=main,
        out_shape=jax.ShapeDtypeStruct((b, L, POOL_WIDTH), F32),
        scratch_shapes=[pltpu.VMEM((rows + 2 * HALO, POOL_WIDTH), F32)],
        compiler_params=_params("parallel", "parallel"),
        name="pool_mixer",
    )(u, u, u, pool_w, pool_scale)


def _head_rms(x, block_mean, g):
    ms = jnp.dot(x * x, block_mean, preferred_element_type=F32, precision=HIGHEST)
    return x * lax.rsqrt(ms + EPS) * g


def _rope(x, cos, sin_signed):
    width = x.shape[-1]
    lane = lax.broadcasted_iota(jnp.int32, x.shape, 1)
    partner = jnp.where((lane & 16) == 0, pltpu.roll(x, width - 16, 1), pltpu.roll(x, 16, 1))
    return x * cos + partner * sin_signed


def _ctx_attn_kernel(sink_ref, q_ref, k_ref, v_ref, qg_ref, kg_ref, bmq_ref, bmk_ref, o_ref, kn_ref):
    kn = _head_rms(k_ref[0], bmk_ref[...], kg_ref[...])
    kn_ref[0] = kn
    qn = (_head_rms(q_ref[0], bmq_ref[...], qg_ref[...]) * ATTN_SCALE).astype(BF16)
    knb = kn.astype(BF16)
    vb = v_ref[0].astype(BF16)
    outs = []
    for h in range(N_Q_HEADS):
        j = h // GQA_GROUP
        kv = slice(j * HEAD_DIM, (j + 1) * HEAD_DIM)
        s = _bdot_nt(qn[:, h * HEAD_DIM:(h + 1) * HEAD_DIM], knb[:, kv])
        sink = sink_ref[h]
        m = jnp.maximum(jnp.max(s, axis=-1, keepdims=True), sink)
        p = jnp.exp(s - m)
        denom = jnp.sum(p, axis=-1, keepdims=True) + jnp.exp(sink - m)
        outs.append(_bdot(p, vb[:, kv]) / denom)
    o_ref[0] = jnp.concatenate(outs, axis=-1)


def _context_attention(q, k, v, sink, qg, kg, bmq, bmk):
    b, L, _ = q.shape
    seq = lambda n: pl.BlockSpec((1, L, n), lambda bi: (bi, 0, 0))
    return pl.pallas_call(
        _ctx_attn_kernel,
        grid=(b,),
        in_specs=[pl.BlockSpec(memory_space=pltpu.SMEM), seq(ATTN_WIDTH), seq(KV_WIDTH), seq(KV_WIDTH),
                  _const_spec(qg.shape), _const_spec(kg.shape), _const_spec(bmq.shape), _const_spec(bmk.shape)],
        out_specs=[seq(ATTN_WIDTH), seq(KV_WIDTH)],
        out_shape=[jax.ShapeDtypeStruct((b, L, ATTN_WIDTH), F32), jax.ShapeDtypeStruct((b, L, KV_WIDTH), F32)],
        compiler_params=_params("parallel"),
        name="context_attention",
    )(sink, q, k, v, qg, kg, bmq, bmk)


def _lat_attn_kernel(sink_ref, q_ref, k_ref, v_ref, kc_ref, vc_ref, cosq_ref, sinq_ref, cos_k_ref, sin_k_ref,
                     qg_ref, kg_ref, bmq_ref, bmk_ref, o_ref, kpad_ref, vpad_ref, kcb_ref, vcb_ref, *, L):
    n = pl.program_id(1)

    @pl.when(n == 0)
    def _():
        kn = _head_rms(k_ref[0], bmk_ref[...], kg_ref[...])
        zeros = jnp.zeros((BLOCK, KV_WIDTH), BF16)
        kpad_ref[0:BLOCK, :] = zeros
        kpad_ref[BLOCK + L:2 * BLOCK + L, :] = zeros
        kpad_ref[BLOCK:BLOCK + L, :] = _rope(kn, cos_k_ref[...], sin_k_ref[...]).astype(BF16)
        vpad_ref[0:BLOCK, :] = zeros
        vpad_ref[BLOCK + L:2 * BLOCK + L, :] = zeros
        vpad_ref[BLOCK:BLOCK + L, :] = v_ref[0].astype(BF16)
        kcb_ref[...] = kc_ref[0, 0].astype(BF16)
        vcb_ref[...] = vc_ref[0, 0].astype(BF16)

    qn = _head_rms(q_ref[0], bmq_ref[...], qg_ref[...])
    qr = (_rope(qn, cosq_ref[...], sinq_ref[...]) * ATTN_SCALE).astype(BF16)
    start = pl.multiple_of(n * BLOCK, BLOCK)
    kl = kpad_ref[pl.ds(start, 3 * BLOCK), :]
    vl = vpad_ref[pl.ds(start, 3 * BLOCK), :]
    kc = kcb_ref[...]
    vc = vcb_ref[...]

    rows = GQA_GROUP * BLOCK
    r = lax.broadcasted_iota(jnp.int32, (rows, 3 * BLOCK), 0) & (BLOCK - 1)
    col = lax.broadcasted_iota(jnp.int32, (rows, 3 * BLOCK), 1)
    kpos = (n - 1) * BLOCK + col
    band = jnp.where(col - r >= 0, jnp.where(col - r <= 2 * WINDOW, 1, 0), 0)
    inside = jnp.where(kpos >= 0, jnp.where(kpos < L, 1, 0), 0)
    valid = (band * inside) > 0
    head_of_row = lax.broadcasted_iota(jnp.int32, (rows, 1), 0) // BLOCK

    outs = []
    for j in range(N_KV_HEADS):
        kv = slice(j * HEAD_DIM, (j + 1) * HEAD_DIM)
        qs = jnp.concatenate([qr[:, (j * GQA_GROUP + g) * HEAD_DIM:(j * GQA_GROUP + g + 1) * HEAD_DIM]
                              for g in range(GQA_GROUP)], axis=0)
        s_loc = jnp.where(valid, _bdot_nt(qs, kl[:, kv]), NEG_INF)
        s_ctx = _bdot_nt(qs, kc[:, kv])
        sink = jnp.zeros((rows, 1), F32)
        for g in range(GQA_GROUP):
            sink = jnp.where(head_of_row == g, sink_ref[j * GQA_GROUP + g], sink)
        m = jnp.maximum(jnp.maximum(jnp.max(s_loc, axis=-1, keepdims=True),
                                    jnp.max(s_ctx, axis=-1, keepdims=True)), sink)
        p_loc = jnp.exp(s_loc - m)
        p_ctx = jnp.exp(s_ctx - m)
        denom = (jnp.sum(p_loc, axis=-1, keepdims=True) + jnp.sum(p_ctx, axis=-1, keepdims=True)
                 + jnp.exp(sink - m))
        o = (_bdot(p_loc, vl[:, kv]) + _bdot(p_ctx, vc[:, kv])) / denom
        outs.extend(o[g * BLOCK:(g + 1) * BLOCK] for g in range(GQA_GROUP))
    o_ref[0] = jnp.concatenate(outs, axis=-1)


def _latent_attention(q, k, v, cache_k, cache_v, layer, sink, cos_tab, sin_tab, qg, kg, bmq, bmk):
    b, L, _ = q.shape
    past = cache_k.shape[2]
    blk = lambda n: pl.BlockSpec((1, BLOCK, n), lambda bi, i: (bi, i, 0))
    seq = lambda n: pl.BlockSpec((1, L, n), lambda bi, i: (bi, 0, 0))
    cache = pl.BlockSpec((1, 1, past, KV_WIDTH), lambda bi, i: (bi, layer, 0, 0))
    tabq = pl.BlockSpec((BLOCK, ATTN_WIDTH), lambda bi, i: (i, 0))
    tabk = pl.BlockSpec((L, KV_WIDTH), lambda bi, i: (0, 0))
    return pl.pallas_call(
        functools.partial(_lat_attn_kernel, L=L),
        grid=(b, L // BLOCK),
        in_specs=[pl.BlockSpec(memory_space=pltpu.SMEM), blk(ATTN_WIDTH), seq(KV_WIDTH), seq(KV_WIDTH),
                  cache, cache, tabq, tabq, tabk, tabk,
                  _const_spec(qg.shape), _const_spec(kg.shape), _const_spec(bmq.shape), _const_spec(bmk.shape)],
        out_specs=blk(ATTN_WIDTH),
        out_shape=jax.ShapeDtypeStruct((b, L, ATTN_WIDTH), F32),
        scratch_shapes=[pltpu.VMEM((L + 2 * BLOCK, KV_WIDTH), BF16), pltpu.VMEM((L + 2 * BLOCK, KV_WIDTH), BF16),
                        pltpu.VMEM((past, KV_WIDTH), BF16), pltpu.VMEM((past, KV_WIDTH), BF16)],
        compiler_params=_params("parallel", "arbitrary"),
        name="latent_attention",
    )(sink, q, k, v, cache_k, cache_v, cos_tab, sin_tab, cos_tab, sin_tab, qg, kg, bmq, bmk)


def _rope_tables(L):
    t = jnp.arange(L)
    row = (t // GRID_W).astype(F32)
    col = (t % GRID_W).astype(F32)
    nf = HEAD_DIM // 4
    inv = ROPE_BASE ** (-jnp.arange(nf, dtype=F32) / nf)
    ang_r = row[:, None] * inv[None, :]
    ang_c = col[:, None] * inv[None, :]
    cos = jnp.concatenate([jnp.cos(ang_r), jnp.cos(ang_r), jnp.cos(ang_c), jnp.cos(ang_c)], axis=-1)
    sin = jnp.concatenate([-jnp.sin(ang_r), jnp.sin(ang_r), -jnp.sin(ang_c), jnp.sin(ang_c)], axis=-1)
    return jnp.tile(cos, (1, N_Q_HEADS)), jnp.tile(sin, (1, N_Q_HEADS))


def _block_mean(width):
    i = jnp.arange(width) // HEAD_DIM
    return jnp.where(i[:, None] == i[None, :], 1.0 / HEAD_DIM, 0.0).astype(F32)


def _ssd_kernel(xbc_ref, xp_ref, xn_ref, z_ref, dt_ref, dtt_ref, init_ref, tri_ref, trit_ref,
                cw_ref, cb_ref, bias_ref, biast_ref, alog_ref, alogt_ref, dskip_ref, ng_ref,
                o_ref, fin_ref, xpad_ref, xc_ref, y_ref, s_ref, *, nc, has_init):
    s = pl.program_id(1)
    fwd = s < nc
    c = jnp.where(fwd, s, 2 * nc - 1 - s)
    r0 = pl.multiple_of(c * CHUNK, CHUNK)

    @pl.when(fwd)
    def _():
        xpad_ref[0:HALO, :] = jnp.where(c > 0, xp_ref[0], 0.0)
        xpad_ref[HALO:HALO + CHUNK, :] = xbc_ref[0]
        xpad_ref[HALO + CHUNK:2 * HALO + CHUNK, :] = jnp.where(c < nc - 1, xn_ref[0], 0.0)
        acc = cb_ref[...] + xpad_ref[HALO - CONV_W // 2:HALO - CONV_W // 2 + CHUNK, :] * cw_ref[0:1, :]
        for kk in range(1, CONV_W):
            off = HALO + kk - CONV_W // 2
            acc = acc + xpad_ref[off:off + CHUNK, :] * cw_ref[kk:kk + 1, :]
        xc_ref[pl.ds(r0, CHUNK), :] = _silu(acc)

    @pl.when((s == 0) | (s == nc))
    def _():
        if has_init:
            s_ref[...] = init_ref[0, 0, 0]
        else:
            s_ref[...] = jnp.zeros(s_ref.shape, F32)

    xc = xc_ref[pl.ds(r0, CHUNK), :]
    xs = xc[:, 0:SSD_INNER]
    xst = xs.T
    tri = tri_ref[0]
    mask = tri > 0.0
    a_neg = -jnp.exp(alog_ref[0])
    a_neg_t = -jnp.exp(alogt_ref[0])
    dt = _softplus(dt_ref[0] + bias_ref[0])
    dtt = _softplus(dtt_ref[0, 0] + biast_ref[0])
    acum = jnp.dot(tri, dt * a_neg, preferred_element_type=F32, precision=HIGHEST)
    acum_t = jnp.dot(dtt * a_neg_t, trit_ref[0], preferred_element_type=F32, precision=HIGHEST)
    last = jnp.where(fwd, acum[CHUNK - 1:CHUNK, :], acum[0:1, :])
    last_t = jnp.where(fwd, acum_t[:, CHUNK - 1:CHUNK], acum_t[:, 0:1])
    coef = jnp.exp(last - acum) * dt
    chunk_decay_t = jnp.exp(last_t)

    ys = []
    for g in range(SSD_GROUPS):
        bm = xc[:, SSD_INNER + g * D_STATE:SSD_INNER + (g + 1) * D_STATE]
        cm = xc[:, SSD_INNER + (SSD_GROUPS + g) * D_STATE:SSD_INNER + (SSD_GROUPS + g + 1) * D_STATE]
        cbm = _bdot_nt(cm, bm)
        for hg in range(HEADS_PER_GROUP):
            h = g * HEADS_PER_GROUP + hg
            hp = slice(h * SSD_HEADDIM, (h + 1) * SSD_HEADDIM)
            a_col = jnp.broadcast_to(acum[:, h:h + 1], (CHUNK, CHUNK))
            seg = jnp.where(mask, a_col - acum_t[h:h + 1, :], NEG_INF)
            w = cbm * jnp.exp(seg) * dtt[h:h + 1, :]
            state = s_ref[hp, :]
            y = _bdot(w, xs[:, hp]) + _bdot_nt(cm * jnp.exp(a_col), state)
            ys.append(y)
            b_scaled = bm * jnp.broadcast_to(coef[:, h:h + 1], (CHUNK, D_STATE))
            s_ref[hp, :] = state * chunk_decay_t[h:h + 1, :] + _bdot(xst[hp, :], b_scaled)
    y = jnp.concatenate(ys, axis=-1)

    @pl.when(fwd)
    def _():
        y_ref[pl.ds(r0, CHUNK), :] = y

    @pl.when(jnp.logical_not(fwd))
    def _():
        tot = y_ref[pl.ds(r0, CHUNK), :] + y + dskip_ref[...] * xs
        tot = tot * _silu(z_ref[0])
        ms = jnp.mean(tot * tot, axis=-1, keepdims=True)
        o_ref[0] = tot * lax.rsqrt(ms + EPS) * ng_ref[...]

    @pl.when((s == nc - 1) | (s == 2 * nc - 1))
    def _():
        fin_ref[0, 0] = s_ref[...]


def _ssd_mixer(z, xbc, dt_raw, init_state, layer, consts, p):
    b, L, _ = z.shape
    nc = L // CHUNK
    has_init = init_state is not None
    dt_t = jnp.stack([dt_raw[:, :, 0:SSD_HEADS], dt_raw[:, :, LANES:LANES + SSD_HEADS]], axis=1)
    dt_t = jnp.swapaxes(dt_t, 2, 3)
    if not has_init:
        init_state = jnp.zeros((1, 1, 1, HALO, LANES), F32)
        init_spec = pl.BlockSpec((1, 1, 1, HALO, LANES), lambda bi, s: (0, 0, 0, 0, 0))
    else:
        init_spec = pl.BlockSpec((1, 1, 1, SSD_INNER, D_STATE), lambda bi, s: (bi, layer, s // nc, 0, 0))
    chunk_of = lambda s: jnp.where(s < nc, s, 2 * nc - 1 - s)
    bwd_chunk_of = lambda s: jnp.where(s < nc, nc - 1, 2 * nc - 1 - s)
    per = CHUNK // HALO
    last = L // HALO - 1
    direction = lambda shape: pl.BlockSpec((1,) + shape, lambda bi, s: (s // nc,) + (0,) * len(shape))
    other_dir = lambda shape: pl.BlockSpec((1,) + shape, lambda bi, s: (1 - s // nc,) + (0,) * len(shape))
    in_specs = [
        pl.BlockSpec((1, CHUNK, CONV_CH), lambda bi, s: (bi, jnp.minimum(s, nc - 1), 0)),
        pl.BlockSpec((1, HALO, CONV_CH), lambda bi, s: (bi, jnp.maximum(jnp.minimum(s, nc - 1) * per - 1, 0), 0)),
        pl.BlockSpec((1, HALO, CONV_CH), lambda bi, s: (bi, jnp.minimum((jnp.minimum(s, nc - 1) + 1) * per, last), 0)),
        pl.BlockSpec((1, CHUNK, SSD_INNER), lambda bi, s: (bi, bwd_chunk_of(s), 0)),
        pl.BlockSpec((1, CHUNK, LANES), lambda bi, s: (bi, chunk_of(s), s // nc)),
        pl.BlockSpec((1, 1, SSD_HEADS, CHUNK), lambda bi, s: (bi, s // nc, 0, chunk_of(s))),
        init_spec,
        direction((CHUNK, CHUNK)), other_dir((CHUNK, CHUNK)),
        _const_spec((CONV_W, CONV_CH)), _const_spec((1, CONV_CH)),
        direction((1, LANES)), direction((SSD_HEADS, 1)), direction((1, LANES)), direction((SSD_HEADS, 1)),
        _const_spec((1, SSD_INNER)), _const_spec((1, SSD_INNER)),
    ]
    out, fin = pl.pallas_call(
        functools.partial(_ssd_kernel, nc=nc, has_init=has_init),
        grid=(b, 2 * nc),
        in_specs=in_specs,
        out_specs=[pl.BlockSpec((1, CHUNK, SSD_INNER), lambda bi, s: (bi, bwd_chunk_of(s), 0)),
                   pl.BlockSpec((1, 1, SSD_INNER, D_STATE), lambda bi, s: (bi, s // nc, 0, 0))],
        out_shape=[jax.ShapeDtypeStruct((b, L, SSD_INNER), F32),
                   jax.ShapeDtypeStruct((b, 2, SSD_INNER, D_STATE), F32)],
        scratch_shapes=[pltpu.VMEM((CHUNK + 2 * HALO, CONV_CH), F32), pltpu.VMEM((L, CONV_CH), F32),
                        pltpu.VMEM((L, SSD_INNER), F32), pltpu.VMEM((SSD_INNER, D_STATE), F32)],
        compiler_params=_params("parallel", "arbitrary"),
        name="ssd_mixer",
    )(xbc, xbc, xbc, z, dt_raw, dt_t, init_state, consts["tri"], consts["tri"],
      p["conv_w"], p["conv_b"], p["dt_bias"], p["dt_bias_t"], p["a_log"], p["a_log_t"], p["d_skip"], p["ssd_norm_g"])
    return out, fin


def _ssd_consts():
    i = jnp.arange(CHUNK)
    lower = (i[None, :] <= i[:, None]).astype(F32)
    return {"tri": jnp.stack([lower, lower.T])}


def _merge_kernel(x_ref, mod_ref, pool_ref, attn_ref, ssd_ref, gate_ref, wp_ref, wa_ref, ws_ref, wo_ref, o_ref):
    gates = _sigmoid(gate_ref[0])
    merged = (gates[:, 0:D_MODEL] * _bdot(pool_ref[0], wp_ref[...])
              + gates[:, D_MODEL:2 * D_MODEL] * _bdot(attn_ref[0], wa_ref[...])
              + gates[:, 2 * D_MODEL:] * _bdot(ssd_ref[0], ws_ref[...]))
    g1 = mod_ref[0][:, 2 * D_MODEL:3 * D_MODEL]
    o_ref[0] = x_ref[0] + g1 * _bdot(merged, wo_ref[...])


def _merge(x, mods, layer, per_batch, pool, attn, ssd, gate, w):
    b, L, _ = x.shape
    tm = TOKEN_TILE
    tok = lambda n: pl.BlockSpec((1, tm, n), lambda bi, i: (bi, i, 0))
    return pl.pallas_call(
        _merge_kernel,
        grid=(b, L // tm),
        in_specs=[tok(D_MODEL), _mod_spec(layer, per_batch), tok(POOL_WIDTH), tok(ATTN_WIDTH), tok(SSD_INNER),
                  tok(N_BRANCH * D_MODEL), _const_spec(w["pool_o"].shape), _const_spec(w["attn_o"].shape),
                  _const_spec(w["ssd_o"].shape), _const_spec(w["out"].shape)],
        out_specs=tok(D_MODEL),
        out_shape=jax.ShapeDtypeStruct((b, L, D_MODEL), F32),
        compiler_params=_params("parallel", "parallel"),
        name="merge",
    )(x, mods, pool, attn, ssd, gate, w["pool_o"], w["attn_o"], w["ssd_o"], w["out"])


FF_CHUNK = 1024


def _mlp_kernel(x_ref, mod_ref, g_ref, w1_ref, w2_ref, o_ref):
    x = x_ref[0]
    mod = mod_ref[0]
    h = _rms_mod(x, g_ref[...], mod[:, 4 * D_MODEL:5 * D_MODEL], mod[:, 3 * D_MODEL:4 * D_MODEL]).astype(BF16)
    acc = jnp.zeros(x.shape, F32)
    for j in range(D_FF // FF_CHUNK):
        ff = slice(j * FF_CHUNK, (j + 1) * FF_CHUNK)
        a = jnp.dot(h, w1_ref[:, ff], preferred_element_type=F32)
        acc = acc + _bdot(jnp.square(jnp.maximum(a, 0.0)), w2_ref[ff, :])
    o_ref[0] = x + mod[:, 5 * D_MODEL:] * acc


def _mlp(x, mods, layer, per_batch, norm_g, w):
    b, L, _ = x.shape
    tm = TOKEN_TILE
    tok = pl.BlockSpec((1, tm, D_MODEL), lambda bi, i: (bi, i, 0))
    return pl.pallas_call(
        _mlp_kernel,
        grid=(b, L // tm),
        in_specs=[tok, _mod_spec(layer, per_batch), _const_spec((1, D_MODEL)),
                  _const_spec(w["mlp1"].shape), _const_spec(w["mlp2"].shape)],
        out_specs=tok,
        out_shape=jax.ShapeDtypeStruct((b, L, D_MODEL), F32),
        compiler_params=_params("parallel", "parallel"),
        name="mlp",
    )(x, mods, norm_g, w["mlp1"], w["mlp2"])


def _layer(x, mods, layer, per_batch, w, p, consts, ctx):
    b, L, _ = x.shape
    u, q, k, v, z, xbc, dt_raw, gate = _in_projection(x, mods, layer, per_batch, p["norm1_g"], w)
    pool = _pool_mixer(u, w["pool_w"], p["pool_scale"])
    if ctx is None:
        attn, k_norm = _context_attention(q, k, v, p["attn_sink"], p["q_norm_g"], p["k_norm_g"],
                                          consts["bmq"], consts["bmk"])
        init = None
    else:
        cache_k, cache_v, init = ctx
        cos_tab, sin_tab = consts["rope"]
        attn = _latent_attention(q, k, v, cache_k, cache_v, layer, p["attn_sink"], cos_tab, sin_tab,
                                 p["q_norm_g"], p["k_norm_g"], consts["bmq"], consts["bmk"])
        k_norm = None
    ssd, fin = _ssd_mixer(z, xbc, dt_raw, init, layer, consts, p)
    x = _merge(x, mods, layer, per_batch, pool, attn, ssd, gate, w)
    x = _mlp(x, mods, layer, per_batch, p["norm2_g"], w)
    return x, (k_norm, v, fin)


def _layer_weights(l, w_in, pool_w, w_pool_o, w_attn_o, w_ssd_o, w_out, w_mlp1, w_mlp2):
    wi = w_in[l]
    o_z = POOL_WIDTH + ATTN_WIDTH + 2 * KV_WIDTH
    o_x = o_z + SSD_INNER
    o_d = o_x + CONV_CH
    o_g = o_d + 2 * SSD_HEADS
    wd = wi[:, o_d:o_g]
    pad = jnp.zeros((D_MODEL, LANES - SSD_HEADS), F32)
    wd = jnp.concatenate([wd[:, :SSD_HEADS], pad, wd[:, SSD_HEADS:], pad], axis=1)
    cast = lambda a: a.astype(BF16)
    return {"a": cast(wi[:, :o_z]), "z": cast(wi[:, o_z:o_x]), "xbc": cast(wi[:, o_x:o_d]), "dt": cast(wd),
            "gate": cast(wi[:, o_g:]), "pool_w": cast(pool_w[l]), "pool_o": cast(w_pool_o[l]),
            "attn_o": cast(w_attn_o[l]), "ssd_o": cast(w_ssd_o[l]), "out": cast(w_out[l]),
            "mlp1": cast(w_mlp1[l]), "mlp2": cast(w_mlp2[l])}


def _pad_lanes(v):
    return jnp.pad(v, ((0, 0), (0, LANES - SSD_HEADS))).reshape(2, 1, LANES)


def kernel(x_prompt, x_sample, cache_k, cache_v, state_ssd, c, c_ctx, w_mod, b_mod, norm1_g, norm2_g, w_in, pool_w, pool_scale, w_pool_o, q_norm_g, k_norm_g, attn_sink, w_attn_o, conv_w, conv_b, dt_bias, a_log, d_skip, ssd_norm_g, w_ssd_o, w_out, w_mlp1, w_mlp2):
    batch, seq, _ = x_prompt.shape
    dec_batch, dec_seq, _ = x_sample.shape
    past = cache_k.shape[2]
    assert 1 + dec_batch <= MOD_ROWS

    cvecs = jnp.concatenate([c_ctx[None, :], c, jnp.zeros((MOD_ROWS - 1 - dec_batch, D_MODEL), F32)], axis=0)
    mods = _modulation(cvecs, w_mod, b_mod)

    consts = {"bmq": _block_mean(ATTN_WIDTH), "bmk": _block_mean(KV_WIDTH), "rope": _rope_tables(dec_seq)}
    consts.update(_ssd_consts())
    cache_k = cache_k.reshape(dec_batch, DEPTH, past, KV_WIDTH)
    cache_v = cache_v.reshape(dec_batch, DEPTH, past, KV_WIDTH)
    state = state_ssd.reshape(dec_batch, DEPTH, 2, SSD_INNER, D_STATE)

    y_prompt, y_sample = x_prompt, x_sample
    ks, vs, ss = [], [], []
    for l in range(DEPTH):
        w = _layer_weights(l, w_in, pool_w, w_pool_o, w_attn_o, w_ssd_o, w_out, w_mlp1, w_mlp2)
        p = {"norm1_g": norm1_g[l][None, :], "norm2_g": norm2_g[l][None, :], "pool_scale": pool_scale[l][None, :],
             "q_norm_g": jnp.tile(q_norm_g[l], N_Q_HEADS)[None, :], "k_norm_g": jnp.tile(k_norm_g[l], N_KV_HEADS)[None, :],
             "attn_sink": attn_sink[l], "conv_w": conv_w[l], "conv_b": conv_b[l][None, :],
             "dt_bias": _pad_lanes(dt_bias[l]), "dt_bias_t": dt_bias[l][:, :, None],
             "a_log": _pad_lanes(a_log[l]), "a_log_t": a_log[l][:, :, None],
             "d_skip": jnp.repeat(d_skip[l], SSD_HEADDIM)[None, :], "ssd_norm_g": ssd_norm_g[l][None, :]}
        y_prompt, (k_l, v_l, s_l) = _layer(y_prompt, mods, l, False, w, p, consts, None)
        ks.append(k_l.reshape(batch, seq, N_KV_HEADS, HEAD_DIM))
        vs.append(v_l.reshape(batch, seq, N_KV_HEADS, HEAD_DIM))
        ss.append(s_l.reshape(batch, 2, SSD_HEADS, SSD_HEADDIM, D_STATE))
        y_sample, _ = _layer(y_sample, mods, l, True, w, p, consts, (cache_k, cache_v, state))
    return (y_prompt, y_sample, jnp.stack(ks, axis=1), jnp.stack(vs, axis=1), jnp.stack(ss, axis=1))
```

```python
import functools
import math

import jax
import jax.numpy as jnp
from jax import lax
from jax.experimental import pallas as pl
from jax.experimental.pallas import tpu as pltpu

F32 = jnp.float32
BF16 = jnp.bfloat16
HIGHEST = lax.Precision.HIGHEST

D_MODEL = 1024
DEPTH = 2
GRID_W = 64
MOD_CHUNKS = 6
EPS = 1e-6
POOL_WIDTH = D_MODEL // 2
POOL_GROUPS = 4
POOL_GROUP_W = POOL_WIDTH // POOL_GROUPS
POOL_WINDOWS = (2, 4, 8, 16)
N_Q_HEADS = 8
N_KV_HEADS = 2
GQA_GROUP = N_Q_HEADS // N_KV_HEADS
HEAD_DIM = 64
ATTN_WIDTH = N_Q_HEADS * HEAD_DIM
KV_WIDTH = N_KV_HEADS * HEAD_DIM
WINDOW = 128
BLOCK = 128
ROPE_BASE = 10000.0
ATTN_SCALE = HEAD_DIM ** -0.5
SSD_HEADS = 16
SSD_HEADDIM = 64
SSD_INNER = SSD_HEADS * SSD_HEADDIM
SSD_GROUPS = 2
HEADS_PER_GROUP = SSD_HEADS // SSD_GROUPS
D_STATE = 128
CONV_W = 5
CHUNK = 128
CONV_CH = SSD_INNER + 2 * SSD_GROUPS * D_STATE
N_BRANCH = 3
D_FF = 4 * D_MODEL

LANES = 128
HALO = 8
MOD_ROWS = 16
VMEM_LIMIT = 56 * 1024 * 1024
TOKEN_TILE = 256
NEG_INF = float("-inf")
LOG2E = math.log2(math.e)
SCORE_SCALE = ATTN_SCALE * LOG2E


def _params(*sem):
    return pltpu.CompilerParams(dimension_semantics=sem, vmem_limit_bytes=VMEM_LIMIT)


def _const_spec(shape):
    nd = len(shape)
    return pl.BlockSpec(shape, lambda *_: (0,) * nd)


def _bdot(a, b):
    return jnp.dot(a.astype(BF16), b.astype(BF16), preferred_element_type=F32)


def _bdot_nt(a, b):
    return lax.dot_general(a.astype(BF16), b.astype(BF16), (((1,), (1,)), ((), ())),
                           preferred_element_type=F32)


def _sigmoid(x):
    return 1.0 / (1.0 + jnp.exp(-x))


def _silu(x):
    return x * _sigmoid(x)


def _softplus(x):
    return jnp.maximum(x, 0.0) + jnp.log1p(jnp.exp(-jnp.abs(x)))


def _rms_mod(x, g, scale, shift):
    ms = jnp.mean(x * x, axis=-1, keepdims=True)
    return (x * lax.rsqrt(ms + EPS)) * g * (1.0 + scale) + shift


def _mod_kernel(c_ref, w_ref, b_ref, o_ref):
    o_ref[0] = _bdot(_silu(c_ref[...]), w_ref[0]) + b_ref[0]


def _modulation(cvecs, w_mod, b_mod):
    n = MOD_CHUNKS * D_MODEL
    tn = n // 4
    out = pl.pallas_call(
        _mod_kernel,
        grid=(DEPTH, n // tn),
        in_specs=[pl.BlockSpec((MOD_ROWS, D_MODEL), lambda l, j: (0, 0)),
                  pl.BlockSpec((1, D_MODEL, tn), lambda l, j: (l, 0, j)),
                  pl.BlockSpec((1, 1, tn), lambda l, j: (l, 0, j))],
        out_specs=pl.BlockSpec((1, MOD_ROWS, tn), lambda l, j: (l, 0, j)),
        out_shape=jax.ShapeDtypeStruct((DEPTH, MOD_ROWS, n), F32),
        compiler_params=_params("arbitrary", "arbitrary"),
        name="modulation",
    )(cvecs, w_mod, b_mod.reshape(DEPTH, 1, n))
    return out.reshape(DEPTH * MOD_ROWS, 1, n)


def _mod_spec(layer, per_batch):
    base = layer * MOD_ROWS
    if per_batch:
        return pl.BlockSpec((1, 1, MOD_CHUNKS * D_MODEL), lambda b, i: (base + 1 + b, 0, 0))
    return pl.BlockSpec((1, 1, MOD_CHUNKS * D_MODEL), lambda b, i: (base, 0, 0))


DT_PAD = 2 * LANES


def _inproj_kernel(x_ref, mod_ref, g_ref, wa_ref, wz_ref, wx_ref, wd_ref, wg_ref,
                   u_ref, q_ref, k_ref, v_ref, z_ref, xbc_ref, dt_ref, gate_ref):
    mod = mod_ref[0]
    h = _rms_mod(x_ref[0], g_ref[...], mod[:, D_MODEL:2 * D_MODEL], mod[:, 0:D_MODEL]).astype(BF16)
    a = jnp.dot(h, wa_ref[...], preferred_element_type=F32)
    u_ref[0] = a[:, 0:POOL_WIDTH]
    q_ref[0] = a[:, POOL_WIDTH:POOL_WIDTH + ATTN_WIDTH]
    k_ref[0] = a[:, POOL_WIDTH + ATTN_WIDTH:POOL_WIDTH + ATTN_WIDTH + KV_WIDTH]
    v_ref[0] = a[:, POOL_WIDTH + ATTN_WIDTH + KV_WIDTH:]
    z_ref[0] = jnp.dot(h, wz_ref[...], preferred_element_type=F32)
    xbc_ref[0] = jnp.dot(h, wx_ref[...], preferred_element_type=F32)
    dt_ref[0] = jnp.dot(h, wd_ref[...], preferred_element_type=F32)
    gate_ref[0] = jnp.dot(h, wg_ref[...], preferred_element_type=F32)


def _in_projection(x, mods, layer, per_batch, norm_g, w):
    b, L, _ = x.shape
    tm = TOKEN_TILE
    widths = (POOL_WIDTH, ATTN_WIDTH, KV_WIDTH, KV_WIDTH, SSD_INNER, CONV_CH, DT_PAD, N_BRANCH * D_MODEL)
    tok = lambda n: pl.BlockSpec((1, tm, n), lambda bi, i: (bi, i, 0))
    return pl.pallas_call(
        _inproj_kernel,
        grid=(b, L // tm),
        in_specs=[tok(D_MODEL), _mod_spec(layer, per_batch), _const_spec((1, D_MODEL)),
                  _const_spec(w["a"].shape), _const_spec(w["z"].shape), _const_spec(w["xbc"].shape),
                  _const_spec(w["dt"].shape), _const_spec(w["gate"].shape)],
        out_specs=[tok(n) for n in widths],
        out_shape=[jax.ShapeDtypeStruct((b, L, n), F32) for n in widths],
        compiler_params=_params("parallel", "parallel"),
        name="in_projection",
    )(x, mods, norm_g, w["a"], w["z"], w["xbc"], w["dt"], w["gate"])


def _halo_specs(rows, cols, L):
    per = rows // HALO
    last = L // HALO - 1
    main = pl.BlockSpec((1, rows, cols), lambda b, i: (b, i, 0))
    prev = pl.BlockSpec((1, HALO, cols), lambda b, i: (b, jnp.maximum(i * per - 1, 0), 0))
    nxt = pl.BlockSpec((1, HALO, cols), lambda b, i: (b, jnp.minimum((i + 1) * per, last), 0))
    return main, prev, nxt


def _pool_kernel(u_ref, up_ref, un_ref, pw_ref, ps_ref, o_ref, pad_ref, *, rows, L):
    i = pl.program_id(1)
    n = pl.num_programs(1)
    pad_ref[0:HALO, :] = jnp.where(i > 0, up_ref[0], 0.0)
    pad_ref[HALO:HALO + rows, :] = u_ref[0]
    pad_ref[HALO + rows:2 * HALO + rows, :] = jnp.where(i < n - 1, un_ref[0], 0.0)
    t = i * rows + lax.broadcasted_iota(jnp.int32, (rows, POOL_GROUP_W), 0)
    for gi, w in enumerate(POOL_WINDOWS):
        cols = slice(gi * POOL_GROUP_W, (gi + 1) * POOL_GROUP_W)
        acc = pad_ref[HALO - w // 2:HALO - w // 2 + rows, cols]
        for d in range(-w // 2 + 1, w // 2):
            acc = acc + pad_ref[HALO + d:HALO + d + rows, cols]
        cnt = (jnp.minimum(t + w // 2, L) - jnp.maximum(t - w // 2, 0)).astype(F32)
        pooled = acc / cnt - pad_ref[HALO:HALO + rows, cols]
        o_ref[0, :, cols] = _bdot(pooled, pw_ref[gi]) * ps_ref[:, cols]


def _pool_mixer(u, pool_w, pool_scale):
    b, L, _ = u.shape
    rows = 256
    assert max(POOL_WINDOWS) // 2 <= HALO and L % rows == 0
    main, prev, nxt = _halo_specs(rows, POOL_WIDTH, L)
    return pl.pallas_call(
        functools.partial(_pool_kernel, rows=rows, L=L),
        grid=(b, L // rows),
        in_specs=[main, prev, nxt, _const_spec(pool_w.shape), _const_spec(pool_scale.shape)],
        out_specs=main,
        out_shape=jax.ShapeDtypeStruct((b, L, POOL_WIDTH), F32),
        scratch_shapes=[pltpu.VMEM((rows + 2 * HALO, POOL_WIDTH), F32)],
        compiler_params=_params("parallel", "parallel"),
        name="pool_mixer",
    )(u, u, u, pool_w, pool_scale)


def _head_rms(x, block_mean, g):
    sq = x * x
    hi = sq.astype(BF16)
    lo = (sq - hi.astype(F32)).astype(BF16)
    ms = jnp.concatenate(
        [jnp.dot(hi[:, c:c + LANES], block_mean, preferred_element_type=F32)
         + jnp.dot(lo[:, c:c + LANES], block_mean, preferred_element_type=F32)
         for c in range(0, x.shape[-1], LANES)], axis=-1)
    return x * lax.rsqrt(ms + EPS) * g


def _rope(x, cos, sin_signed):
    width = x.shape[-1]
    lane = lax.broadcasted_iota(jnp.int32, x.shape, 1)
    partner = jnp.where((lane & 16) == 0, pltpu.roll(x, width - 16, 1), pltpu.roll(x, 16, 1))
    return x * cos + partner * sin_signed


def _ctx_attn_kernel(sink_ref, q_ref, k_ref, v_ref, qg_ref, kg_ref, bm_ref, o_ref, kn_ref):
    kn = _head_rms(k_ref[0], bm_ref[...], kg_ref[...])
    kn_ref[0] = kn
    qn = (_head_rms(q_ref[0], bm_ref[...], qg_ref[...]) * SCORE_SCALE).astype(BF16)
    knb = kn.astype(BF16)
    vb = v_ref[0].astype(BF16)
    outs = []
    for h in range(N_Q_HEADS):
        j = h // GQA_GROUP
        kv = slice(j * HEAD_DIM, (j + 1) * HEAD_DIM)
        s = _bdot_nt(qn[:, h * HEAD_DIM:(h + 1) * HEAD_DIM], knb[:, kv])
        sink = sink_ref[h] * LOG2E
        m = jnp.maximum(jnp.max(s, axis=-1, keepdims=True), sink)
        p = jnp.exp2(s - m)
        denom = jnp.sum(p, axis=-1, keepdims=True) + jnp.exp2(sink - m)
        outs.append(_bdot(p, vb[:, kv]) / denom)
    o_ref[0] = jnp.concatenate(outs, axis=-1)


def _context_attention(q, k, v, sink, qg, kg, bm):
    b, L, _ = q.shape
    seq = lambda n: pl.BlockSpec((1, L, n), lambda bi: (bi, 0, 0))
    return pl.pallas_call(
        _ctx_attn_kernel,
        grid=(b,),
        in_specs=[pl.BlockSpec(memory_space=pltpu.SMEM), seq(ATTN_WIDTH), seq(KV_WIDTH), seq(KV_WIDTH),
                  _const_spec(qg.shape), _const_spec(kg.shape), _const_spec(bm.shape)],
        out_specs=[seq(ATTN_WIDTH), seq(KV_WIDTH)],
        out_shape=[jax.ShapeDtypeStruct((b, L, ATTN_WIDTH), F32), jax.ShapeDtypeStruct((b, L, KV_WIDTH), F32)],
        compiler_params=_params("parallel"),
        name="context_attention",
    )(sink, q, k, v, qg, kg, bm)


def _lat_attn_kernel(sink_ref, q_ref, k_ref, v_ref, kc_ref, vc_ref, cosq_ref, sinq_ref, cos_k_ref, sin_k_ref,
                     band_ref, qg_ref, kg_ref, bm_ref, o_ref, kpad_ref, vpad_ref, kcb_ref, vcb_ref, *, L):
    n = pl.program_id(1)

    @pl.when(n == 0)
    def _():
        kn = _head_rms(k_ref[0], bm_ref[...], kg_ref[...])
        zeros = jnp.zeros((BLOCK, KV_WIDTH), BF16)
        kpad_ref[0:BLOCK, :] = zeros
        kpad_ref[BLOCK + L:2 * BLOCK + L, :] = zeros
        kpad_ref[BLOCK:BLOCK + L, :] = _rope(kn, cos_k_ref[...], sin_k_ref[...]).astype(BF16)
        vpad_ref[0:BLOCK, :] = zeros
        vpad_ref[BLOCK + L:2 * BLOCK + L, :] = zeros
        vpad_ref[BLOCK:BLOCK + L, :] = v_ref[0].astype(BF16)
        kcb_ref[...] = kc_ref[0, 0].astype(BF16)
        vcb_ref[...] = vc_ref[0, 0].astype(BF16)

    qn = _head_rms(q_ref[0], bm_ref[...], qg_ref[...])
    qr = (_rope(qn, cosq_ref[...], sinq_ref[...]) * SCORE_SCALE).astype(BF16)
    start = pl.multiple_of(n * BLOCK, BLOCK)
    kl = kpad_ref[pl.ds(start, 3 * BLOCK), :]
    vl = vpad_ref[pl.ds(start, 3 * BLOCK), :]
    kc = kcb_ref[...]
    vc = vcb_ref[...]
    band = band_ref[0]

    outs = []
    for j in range(N_KV_HEADS):
        kv = slice(j * HEAD_DIM, (j + 1) * HEAD_DIM)
        qs = jnp.concatenate([qr[:, (j * GQA_GROUP + g) * HEAD_DIM:(j * GQA_GROUP + g + 1) * HEAD_DIM]
                              for g in range(GQA_GROUP)], axis=0)
        s_loc_all = _bdot_nt(qs, kl[:, kv])
        s_ctx_all = _bdot_nt(qs, kc[:, kv])
        p_loc, p_ctx, denoms = [], [], []
        for g in range(GQA_GROUP):
            rows = slice(g * BLOCK, (g + 1) * BLOCK)
            s_loc = s_loc_all[rows] + band
            s_ctx = s_ctx_all[rows]
            sink = sink_ref[j * GQA_GROUP + g] * LOG2E
            m = jnp.maximum(jnp.maximum(jnp.max(s_loc, axis=-1, keepdims=True),
                                        jnp.max(s_ctx, axis=-1, keepdims=True)), sink)
            pl_g = jnp.exp2(s_loc - m)
            pc_g = jnp.exp2(s_ctx - m)
            denoms.append(jnp.sum(pl_g, axis=-1, keepdims=True) + jnp.sum(pc_g, axis=-1, keepdims=True)
                          + jnp.exp2(sink - m))
            p_loc.append(pl_g.astype(BF16))
            p_ctx.append(pc_g.astype(BF16))
        o = (jnp.dot(jnp.concatenate(p_loc, axis=0), vl[:, kv], preferred_element_type=F32)
             + jnp.dot(jnp.concatenate(p_ctx, axis=0), vc[:, kv], preferred_element_type=F32))
        outs.extend(o[g * BLOCK:(g + 1) * BLOCK] / denoms[g] for g in range(GQA_GROUP))
    o_ref[0] = jnp.concatenate(outs, axis=-1)


def _latent_attention(q, k, v, cache_k, cache_v, layer, sink, cos_tab, sin_tab, qg, kg, bm):
    b, L, _ = q.shape
    past = cache_k.shape[2]
    blk = lambda n: pl.BlockSpec((1, BLOCK, n), lambda bi, i: (bi, i, 0))
    seq = lambda n: pl.BlockSpec((1, L, n), lambda bi, i: (bi, 0, 0))
    cache = pl.BlockSpec((1, 1, past, KV_WIDTH), lambda bi, i: (bi, layer, 0, 0))
    tabq = pl.BlockSpec((BLOCK, ATTN_WIDTH), lambda bi, i: (i, 0))
    tabk = pl.BlockSpec((L, KV_WIDTH), lambda bi, i: (0, 0))
    nb = L // BLOCK
    band_spec = pl.BlockSpec((1, BLOCK, 3 * BLOCK),
                             lambda bi, i: (jnp.minimum(i, 1) + jnp.maximum(i - (nb - 2), 0), 0, 0))
    return pl.pallas_call(
        functools.partial(_lat_attn_kernel, L=L),
        grid=(b, nb),
        in_specs=[pl.BlockSpec(memory_space=pltpu.SMEM), blk(ATTN_WIDTH), seq(KV_WIDTH), seq(KV_WIDTH),
                  cache, cache, tabq, tabq, tabk, tabk, band_spec,
                  _const_spec(qg.shape), _const_spec(kg.shape), _const_spec(bm.shape)],
        out_specs=blk(ATTN_WIDTH),
        out_shape=jax.ShapeDtypeStruct((b, L, ATTN_WIDTH), F32),
        scratch_shapes=[pltpu.VMEM((L + 2 * BLOCK, KV_WIDTH), BF16), pltpu.VMEM((L + 2 * BLOCK, KV_WIDTH), BF16),
                        pltpu.VMEM((past, KV_WIDTH), BF16), pltpu.VMEM((past, KV_WIDTH), BF16)],
        compiler_params=_params("parallel", "arbitrary"),
        name="latent_attention",
    )(sink, q, k, v, cache_k, cache_v, cos_tab, sin_tab, cos_tab, sin_tab, _band_bias(), qg, kg, bm)


def _band_bias():
    r = jnp.arange(BLOCK)[:, None]
    col = jnp.arange(3 * BLOCK)[None, :]
    band = jnp.abs(col - BLOCK - r) <= WINDOW
    first = band & (col >= BLOCK)
    last = band & (col < 2 * BLOCK)
    return jnp.where(jnp.stack([first, band, last]), 0.0, NEG_INF).astype(F32)


def _rope_tables(L):
    t = jnp.arange(L)
    row = (t // GRID_W).astype(F32)
    col = (t % GRID_W).astype(F32)
    nf = HEAD_DIM // 4
    inv = ROPE_BASE ** (-jnp.arange(nf, dtype=F32) / nf)
    ang_r = row[:, None] * inv[None, :]
    ang_c = col[:, None] * inv[None, :]
    cos = jnp.concatenate([jnp.cos(ang_r), jnp.cos(ang_r), jnp.cos(ang_c), jnp.cos(ang_c)], axis=-1)
    sin = jnp.concatenate([-jnp.sin(ang_r), jnp.sin(ang_r), -jnp.sin(ang_c), jnp.sin(ang_c)], axis=-1)
    return jnp.tile(cos, (1, N_Q_HEADS)), jnp.tile(sin, (1, N_Q_HEADS))


def _block_mean():
    i = jnp.arange(LANES) // HEAD_DIM
    return jnp.where(i[:, None] == i[None, :], 1.0 / HEAD_DIM, 0.0).astype(BF16)


def _expand_heads(table, expand):
    hi = table.astype(BF16)
    lo = (table - hi.astype(F32)).astype(BF16)
    return (jnp.dot(hi, expand, preferred_element_type=F32) + jnp.dot(lo, expand, preferred_element_type=F32))


def _ssd_conv_chunk(xbc_ref, xp_ref, xn_ref, cw_ref, cb_ref, xpad_ref, has_prev, has_next):
    xpad_ref[0:HALO, :] = jnp.where(has_prev, xp_ref[0], 0.0)
    xpad_ref[HALO:HALO + CHUNK, :] = xbc_ref[0]
    xpad_ref[HALO + CHUNK:2 * HALO + CHUNK, :] = jnp.where(has_next, xn_ref[0], 0.0)
    acc = cb_ref[...] + xpad_ref[HALO - CONV_W // 2:HALO - CONV_W // 2 + CHUNK, :] * cw_ref[0:1, :]
    for kk in range(1, CONV_W):
        off = HALO + kk - CONV_W // 2
        acc = acc + xpad_ref[off:off + CHUNK, :] * cw_ref[kk:kk + 1, :]
    return _silu(acc)


def _ssd_scan_prelude(xc, dt_raw, dtt_raw, tri, tri_t, expand, bias, bias_t, a_log, a_log_t, end_row):
    xs = xc[:, 0:SSD_INNER]
    a_neg = -jnp.exp(a_log) * LOG2E
    a_neg_t = -jnp.exp(a_log_t) * LOG2E
    dt = _softplus(dt_raw + bias)
    dtt = _softplus(dtt_raw + bias_t)
    acum = jnp.dot(tri, dt * a_neg, preferred_element_type=F32, precision=HIGHEST)
    acum_t = jnp.dot(dtt * a_neg_t, tri_t, preferred_element_type=F32, precision=HIGHEST)
    last = acum[end_row:end_row + 1, :]
    spread = _expand_heads(jnp.concatenate([jnp.exp2(acum), jnp.exp2(last - acum) * dt, dt], axis=0), expand)
    return {"xc": xc, "mask": tri > 0.0, "acum": acum, "acum_t": acum_t,
            "chunk_decay_t": jnp.exp2(acum_t[:, end_row:end_row + 1]),
            "decay_in": spread[0:CHUNK],
            "x_state": xs * spread[CHUNK:2 * CHUNK],
            "x_dt": (xs * spread[2 * CHUNK:]).astype(BF16)}


def _ssd_scan_chunks(scans):
    lane = lax.broadcasted_iota(jnp.int32, (CHUNK, LANES), 1)
    zero = jnp.zeros((CHUNK, LANES), BF16)
    ys = [[] for _ in scans]
    for g in range(SSD_GROUPS):
        hs = slice(g * HEADS_PER_GROUP * SSD_HEADDIM, (g + 1) * HEADS_PER_GROUP * SSD_HEADDIM)
        stage = []
        for t, s_ref in scans:
            xc = t["xc"]
            bm = xc[:, SSD_INNER + g * D_STATE:SSD_INNER + (g + 1) * D_STATE].astype(BF16)
            cm = xc[:, SSD_INNER + (SSD_GROUPS + g) * D_STATE:SSD_INNER + (SSD_GROUPS + g + 1) * D_STATE].astype(BF16)
            cbm = jnp.where(t["mask"], _bdot_nt(cm, bm), 0.0)
            state = s_ref[hs, :]
            y_off = _bdot_nt(cm, state) * t["decay_in"][:, hs]
            stage.append((bm, cbm, state, y_off, []))
        for pair in range(HEADS_PER_GROUP // 2):
            for (t, _), (_, cbm, _, _, y_diag) in zip(scans, stage):
                ws = []
                for h in (g * HEADS_PER_GROUP + 2 * pair, g * HEADS_PER_GROUP + 2 * pair + 1):
                    a_col = jnp.broadcast_to(t["acum"][:, h:h + 1], (CHUNK, CHUNK))
                    ws.append((cbm * jnp.exp2(jnp.minimum(a_col - t["acum_t"][h:h + 1, :], 0.0))).astype(BF16))
                slab = t["x_dt"][:, hs][:, pair * LANES:(pair + 1) * LANES]
                rhs = jnp.concatenate([jnp.where(lane < SSD_HEADDIM, slab, zero),
                                       jnp.where(lane >= SSD_HEADDIM, slab, zero)], axis=0)
                y_diag.append(jnp.dot(jnp.concatenate(ws, axis=1), rhs, preferred_element_type=F32))
        for i, ((t, s_ref), (bm, _, state, y_off, y_diag)) in enumerate(zip(scans, stage)):
            ys[i].append(jnp.concatenate(y_diag, axis=1) + y_off)
            update = _bdot(t["x_state"][:, hs].T, bm)
            for hg in range(HEADS_PER_GROUP):
                h = g * HEADS_PER_GROUP + hg
                hp = slice(h * SSD_HEADDIM, (h + 1) * SSD_HEADDIM)
                rel = slice(hg * SSD_HEADDIM, (hg + 1) * SSD_HEADDIM)
                s_ref[hp, :] = state[rel] * t["chunk_decay_t"][h:h + 1, :] + update[rel]
    return [jnp.concatenate(y, axis=-1) for y in ys]


def _ssd_kernel(xf_ref, xfp_ref, xfn_ref, xb_ref, xbp_ref, xbn_ref, zf_ref, zb_ref, dtf_ref, dtb_ref,
                dttf_ref, dttb_ref, init_ref, tri_ref, expand_ref, cw_ref, cb_ref, bias_ref, biast_ref,
                alog_ref, alogt_ref, dskip_ref, ng_ref, o_ref, fin_ref, xpadf_ref, xpadb_ref, xc_ref, sf_ref, sb_ref,
                *, nc, has_init):
    s = pl.program_id(1)
    half = nc // 2
    first_half = s < half
    cf = s
    cb = nc - 1 - s
    rf = pl.multiple_of(cf * CHUNK, CHUNK)
    rb = pl.multiple_of(cb * CHUNK, CHUNK)

    @pl.when(first_half)
    def _():
        xc_ref[pl.ds(rf, CHUNK), :] = _ssd_conv_chunk(xf_ref, xfp_ref, xfn_ref, cw_ref, cb_ref, xpadf_ref,
                                                      cf > 0, cf < nc - 1)
        xc_ref[pl.ds(rb, CHUNK), :] = _ssd_conv_chunk(xb_ref, xbp_ref, xbn_ref, cw_ref, cb_ref, xpadb_ref,
                                                      cb > 0, cb < nc - 1)

    @pl.when(s == 0)
    def _():
        if has_init:
            sf_ref[...] = init_ref[0, 0, 0]
            sb_ref[...] = init_ref[0, 0, 1]
        else:
            sf_ref[...] = jnp.zeros(sf_ref.shape, F32)
            sb_ref[...] = jnp.zeros(sb_ref.shape, F32)

    lower = tri_ref[0]
    upper = tri_ref[1]
    xcf = xc_ref[pl.ds(rf, CHUNK), :]
    xcb = xc_ref[pl.ds(rb, CHUNK), :]
    tf = _ssd_scan_prelude(xcf, dtf_ref[0], dttf_ref[0, 0], lower, upper, expand_ref[...], bias_ref[0],
                           biast_ref[0], alog_ref[0], alogt_ref[0], CHUNK - 1)
    tb = _ssd_scan_prelude(xcb, dtb_ref[0], dttb_ref[0, 0], upper, lower, expand_ref[...], bias_ref[1],
                           biast_ref[1], alog_ref[1], alogt_ref[1], 0)
    yf, yb = _ssd_scan_chunks([(tf, sf_ref), (tb, sb_ref)])

    @pl.when(first_half)
    def _():
        o_ref[0, pl.ds(rf, CHUNK), :] = yf
        o_ref[0, pl.ds(rb, CHUNK), :] = yb

    def finish(rows, y, xc, z):
        tot = o_ref[0, pl.ds(rows, CHUNK), :] + y + dskip_ref[...] * xc[:, 0:SSD_INNER]
        tot = tot * _silu(z)
        ms = jnp.mean(tot * tot, axis=-1, keepdims=True)
        o_ref[0, pl.ds(rows, CHUNK), :] = tot * lax.rsqrt(ms + EPS) * ng_ref[...]

    @pl.when(jnp.logical_not(first_half))
    def _():
        finish(rf, yf, xcf, zf_ref[0])
        finish(rb, yb, xcb, zb_ref[0])

    @pl.when(s == nc - 1)
    def _():
        fin_ref[0, 0] = sf_ref[...]
        fin_ref[0, 1] = sb_ref[...]


def _ssd_mixer(z, xbc, dt_raw, init_state, layer, consts, p):
    b, L, _ = z.shape
    nc = L // CHUNK
    has_init = init_state is not None
    dt_t = jnp.stack([dt_raw[:, :, 0:SSD_HEADS], dt_raw[:, :, LANES:LANES + SSD_HEADS]], axis=1)
    dt_t = jnp.swapaxes(dt_t, 2, 3)
    assert nc % 2 == 0
    half = nc // 2
    if not has_init:
        init_state = jnp.zeros((1, 1, 1, HALO, LANES), F32)
        init_spec = pl.BlockSpec((1, 1, 1, HALO, LANES), lambda bi, s: (0, 0, 0, 0, 0))
    else:
        init_spec = pl.BlockSpec((1, 1, 2, SSD_INNER, D_STATE), lambda bi, s: (bi, layer, 0, 0, 0))
    per = CHUNK // HALO
    last = L // HALO - 1
    conv_f = lambda s: jnp.minimum(s, half - 1)
    conv_b = lambda s: jnp.maximum(nc - 1 - s, half)
    fin_f = lambda s: jnp.maximum(s, half)
    fin_b = lambda s: jnp.minimum(nc - 1 - s, half - 1)

    def conv_specs(chunk):
        return [pl.BlockSpec((1, CHUNK, CONV_CH), lambda bi, s: (bi, chunk(s), 0)),
                pl.BlockSpec((1, HALO, CONV_CH), lambda bi, s: (bi, jnp.maximum(chunk(s) * per - 1, 0), 0)),
                pl.BlockSpec((1, HALO, CONV_CH), lambda bi, s: (bi, jnp.minimum((chunk(s) + 1) * per, last), 0))]

    in_specs = conv_specs(conv_f) + conv_specs(conv_b) + [
        pl.BlockSpec((1, CHUNK, SSD_INNER), lambda bi, s: (bi, fin_f(s), 0)),
        pl.BlockSpec((1, CHUNK, SSD_INNER), lambda bi, s: (bi, fin_b(s), 0)),
        pl.BlockSpec((1, CHUNK, LANES), lambda bi, s: (bi, s, 0)),
        pl.BlockSpec((1, CHUNK, LANES), lambda bi, s: (bi, nc - 1 - s, 1)),
        pl.BlockSpec((1, 1, SSD_HEADS, CHUNK), lambda bi, s: (bi, 0, 0, s)),
        pl.BlockSpec((1, 1, SSD_HEADS, CHUNK), lambda bi, s: (bi, 1, 0, nc - 1 - s)),
        init_spec,
        _const_spec((2, CHUNK, CHUNK)), _const_spec((LANES, SSD_INNER)),
        _const_spec((CONV_W, CONV_CH)), _const_spec((1, CONV_CH)),
        _const_spec((2, 1, LANES)), _const_spec((2, SSD_HEADS, 1)), _const_spec((2, 1, LANES)),
        _const_spec((2, SSD_HEADS, 1)), _const_spec((1, SSD_INNER)), _const_spec((1, SSD_INNER)),
    ]
    out, fin = pl.pallas_call(
        functools.partial(_ssd_kernel, nc=nc, has_init=has_init),
        grid=(b, nc),
        in_specs=in_specs,
        out_specs=[pl.BlockSpec((1, L, SSD_INNER), lambda bi, s: (bi, 0, 0)),
                   pl.BlockSpec((1, 2, SSD_INNER, D_STATE), lambda bi, s: (bi, 0, 0, 0))],
        out_shape=[jax.ShapeDtypeStruct((b, L, SSD_INNER), F32),
                   jax.ShapeDtypeStruct((b, 2, SSD_INNER, D_STATE), F32)],
        scratch_shapes=[pltpu.VMEM((CHUNK + 2 * HALO, CONV_CH), F32), pltpu.VMEM((CHUNK + 2 * HALO, CONV_CH), F32),
                        pltpu.VMEM((L, CONV_CH), F32), pltpu.VMEM((SSD_INNER, D_STATE), F32),
                        pltpu.VMEM((SSD_INNER, D_STATE), F32)],
        compiler_params=_params("parallel", "arbitrary"),
        name="ssd_mixer",
    )(xbc, xbc, xbc, xbc, xbc, xbc, z, z, dt_raw, dt_raw, dt_t, dt_t, init_state, consts["tri"], consts["expand"],
      p["conv_w"], p["conv_b"], p["dt_bias"], p["dt_bias_t"], p["a_log"], p["a_log_t"], p["d_skip"], p["ssd_norm_g"])
    return out, fin


def _ssd_consts():
    i = jnp.arange(CHUNK)
    lower = (i[None, :] <= i[:, None]).astype(F32)
    head_of_lane = jnp.arange(SSD_INNER) // SSD_HEADDIM
    expand = (jnp.arange(LANES)[:, None] == head_of_lane[None, :]).astype(BF16)
    return {"tri": jnp.stack([lower, lower.T]), "expand": expand}


def _merge_kernel(x_ref, mod_ref, pool_ref, attn_ref, ssd_ref, gate_ref, wp_ref, wa_ref, ws_ref, wo_ref, o_ref):
    gates = _sigmoid(gate_ref[0])
    merged = (gates[:, 0:D_MODEL] * _bdot(pool_ref[0], wp_ref[...])
              + gates[:, D_MODEL:2 * D_MODEL] * _bdot(attn_ref[0], wa_ref[...])
              + gates[:, 2 * D_MODEL:] * _bdot(ssd_ref[0], ws_ref[...]))
    g1 = mod_ref[0][:, 2 * D_MODEL:3 * D_MODEL]
    o_ref[0] = x_ref[0] + g1 * _bdot(merged, wo_ref[...])


def _merge(x, mods, layer, per_batch, pool, attn, ssd, gate, w):
    b, L, _ = x.shape
    tm = TOKEN_TILE
    tok = lambda n: pl.BlockSpec((1, tm, n), lambda bi, i: (bi, i, 0))
    return pl.pallas_call(
        _merge_kernel,
        grid=(b, L // tm),
        in_specs=[tok(D_MODEL), _mod_spec(layer, per_batch), tok(POOL_WIDTH), tok(ATTN_WIDTH), tok(SSD_INNER),
                  tok(N_BRANCH * D_MODEL), _const_spec(w["pool_o"].shape), _const_spec(w["attn_o"].shape),
                  _const_spec(w["ssd_o"].shape), _const_spec(w["out"].shape)],
        out_specs=tok(D_MODEL),
        out_shape=jax.ShapeDtypeStruct((b, L, D_MODEL), F32),
        compiler_params=_params("parallel", "parallel"),
        name="merge",
    )(x, mods, pool, attn, ssd, gate, w["pool_o"], w["attn_o"], w["ssd_o"], w["out"])


FF_CHUNK = 1024


def _mlp_kernel(x_ref, mod_ref, g_ref, w1_ref, w2_ref, o_ref):
    x = x_ref[0]
    mod = mod_ref[0]
    h = _rms_mod(x, g_ref[...], mod[:, 4 * D_MODEL:5 * D_MODEL], mod[:, 3 * D_MODEL:4 * D_MODEL]).astype(BF16)
    acc = jnp.zeros(x.shape, F32)
    for j in range(D_FF // FF_CHUNK):
        ff = slice(j * FF_CHUNK, (j + 1) * FF_CHUNK)
        a = jnp.dot(h, w1_ref[:, ff], preferred_element_type=F32)
        acc = acc + _bdot(jnp.square(jnp.maximum(a, 0.0)), w2_ref[ff, :])
    o_ref[0] = x + mod[:, 5 * D_MODEL:] * acc


def _mlp(x, mods, layer, per_batch, norm_g, w):
    b, L, _ = x.shape
    tm = TOKEN_TILE
    tok = pl.BlockSpec((1, tm, D_MODEL), lambda bi, i: (bi, i, 0))
    return pl.pallas_call(
        _mlp_kernel,
        grid=(b, L // tm),
        in_specs=[tok, _mod_spec(layer, per_batch), _const_spec((1, D_MODEL)),
                  _const_spec(w["mlp1"].shape), _const_spec(w["mlp2"].shape)],
        out_specs=tok,
        out_shape=jax.ShapeDtypeStruct((b, L, D_MODEL), F32),
        compiler_params=_params("parallel", "parallel"),
        name="mlp",
    )(x, mods, norm_g, w["mlp1"], w["mlp2"])


def _layer(x, mods, layer, per_batch, w, p, consts, ctx):
    b, L, _ = x.shape
    u, q, k, v, z, xbc, dt_raw, gate = _in_projection(x, mods, layer, per_batch, p["norm1_g"], w)
    pool = _pool_mixer(u, w["pool_w"], p["pool_scale"])
    if ctx is None:
        attn, k_norm = _context_attention(q, k, v, p["attn_sink"], p["q_norm_g"], p["k_norm_g"], consts["bm"])
        init = None
    else:
        cache_k, cache_v, init = ctx
        cos_tab, sin_tab = consts["rope"]
        attn = _latent_attention(q, k, v, cache_k, cache_v, layer, p["attn_sink"], cos_tab, sin_tab,
                                 p["q_norm_g"], p["k_norm_g"], consts["bm"])
        k_norm = None
    ssd, fin = _ssd_mixer(z, xbc, dt_raw, init, layer, consts, p)
    x = _merge(x, mods, layer, per_batch, pool, attn, ssd, gate, w)
    x = _mlp(x, mods, layer, per_batch, p["norm2_g"], w)
    return x, (k_norm, v, fin)


def _layer_weights(l, w_in, pool_w, w_pool_o, w_attn_o, w_ssd_o, w_out, w_mlp1, w_mlp2):
    wi = w_in[l]
    o_z = POOL_WIDTH + ATTN_WIDTH + 2 * KV_WIDTH
    o_x = o_z + SSD_INNER
    o_d = o_x + CONV_CH
    o_g = o_d + 2 * SSD_HEADS
    wd = wi[:, o_d:o_g]
    pad = jnp.zeros((D_MODEL, LANES - SSD_HEADS), F32)
    wd = jnp.concatenate([wd[:, :SSD_HEADS], pad, wd[:, SSD_HEADS:], pad], axis=1)
    cast = lambda a: a.astype(BF16)
    return {"a": cast(wi[:, :o_z]), "z": cast(wi[:, o_z:o_x]), "xbc": cast(wi[:, o_x:o_d]), "dt": cast(wd),
            "gate": cast(wi[:, o_g:]), "pool_w": cast(pool_w[l]), "pool_o": cast(w_pool_o[l]),
            "attn_o": cast(w_attn_o[l]), "ssd_o": cast(w_ssd_o[l]), "out": cast(w_out[l]),
            "mlp1": cast(w_mlp1[l]), "mlp2": cast(w_mlp2[l])}


def _pad_lanes(v):
    return jnp.pad(v, ((0, 0), (0, LANES - SSD_HEADS))).reshape(2, 1, LANES)


def kernel(x_prompt, x_sample, cache_k, cache_v, state_ssd, c, c_ctx, w_mod, b_mod, norm1_g, norm2_g, w_in, pool_w, pool_scale, w_pool_o, q_norm_g, k_norm_g, attn_sink, w_attn_o, conv_w, conv_b, dt_bias, a_log, d_skip, ssd_norm_g, w_ssd_o, w_out, w_mlp1, w_mlp2):
    batch, seq, _ = x_prompt.shape
    dec_batch, dec_seq, _ = x_sample.shape
    past = cache_k.shape[2]
    assert 1 + dec_batch <= MOD_ROWS

    cvecs = jnp.concatenate([c_ctx[None, :], c, jnp.zeros((MOD_ROWS - 1 - dec_batch, D_MODEL), F32)], axis=0)
    mods = _modulation(cvecs, w_mod, b_mod)

    consts = {"bm": _block_mean(), "rope": _rope_tables(dec_seq)}
    consts.update(_ssd_consts())
    cache_k = cache_k.reshape(dec_batch, DEPTH, past, KV_WIDTH)
    cache_v = cache_v.reshape(dec_batch, DEPTH, past, KV_WIDTH)
    state = state_ssd.reshape(dec_batch, DEPTH, 2, SSD_INNER, D_STATE)

    y_prompt, y_sample = x_prompt, x_sample
    ks, vs, ss = [], [], []
    for l in range(DEPTH):
        w = _layer_weights(l, w_in, pool_w, w_pool_o, w_attn_o, w_ssd_o, w_out, w_mlp1, w_mlp2)
        p = {"norm1_g": norm1_g[l][None, :], "norm2_g": norm2_g[l][None, :], "pool_scale": pool_scale[l][None, :],
             "q_norm_g": jnp.tile(q_norm_g[l], N_Q_HEADS)[None, :], "k_norm_g": jnp.tile(k_norm_g[l], N_KV_HEADS)[None, :],
             "attn_sink": attn_sink[l], "conv_w": conv_w[l], "conv_b": conv_b[l][None, :],
             "dt_bias": _pad_lanes(dt_bias[l]), "dt_bias_t": dt_bias[l][:, :, None],
             "a_log": _pad_lanes(a_log[l]), "a_log_t": a_log[l][:, :, None],
             "d_skip": jnp.repeat(d_skip[l], SSD_HEADDIM)[None, :], "ssd_norm_g": ssd_norm_g[l][None, :]}
        y_prompt, (k_l, v_l, s_l) = _layer(y_prompt, mods, l, False, w, p, consts, None)
        ks.append(k_l.reshape(batch, seq, N_KV_HEADS, HEAD_DIM))
        vs.append(v_l.reshape(batch, seq, N_KV_HEADS, HEAD_DIM))
        ss.append(s_l.reshape(batch, 2, SSD_HEADS, SSD_HEADDIM, D_STATE))
        y_sample, _ = _layer(y_sample, mods, l, True, w, p, consts, (cache_k, cache_v, state))
    return (y_prompt, y_sample, jnp.stack(ks, axis=1), jnp.stack(vs, axis=1), jnp.stack(ss, axis=1))
```

```python
import functools
import math

import jax
import jax.numpy as jnp
from jax import lax
from jax.experimental import pallas as pl
from jax.experimental.pallas import tpu as pltpu

F32 = jnp.float32
BF16 = jnp.bfloat16
HIGHEST = lax.Precision.HIGHEST

D_MODEL = 1024
DEPTH = 2
GRID_W = 64
MOD_CHUNKS = 6
EPS = 1e-6
POOL_WIDTH = D_MODEL // 2
POOL_GROUPS = 4
POOL_GROUP_W = POOL_WIDTH // POOL_GROUPS
POOL_WINDOWS = (2, 4, 8, 16)
N_Q_HEADS = 8
N_KV_HEADS = 2
GQA_GROUP = N_Q_HEADS // N_KV_HEADS
HEAD_DIM = 64
ATTN_WIDTH = N_Q_HEADS * HEAD_DIM
KV_WIDTH = N_KV_HEADS * HEAD_DIM
WINDOW = 128
BLOCK = 128
ROPE_BASE = 10000.0
ATTN_SCALE = HEAD_DIM ** -0.5
SSD_HEADS = 16
SSD_HEADDIM = 64
SSD_INNER = SSD_HEADS * SSD_HEADDIM
SSD_GROUPS = 2
HEADS_PER_GROUP = SSD_HEADS // SSD_GROUPS
D_STATE = 128
CONV_W = 5
CHUNK = 128
CONV_CH = SSD_INNER + 2 * SSD_GROUPS * D_STATE
N_BRANCH = 3
D_FF = 4 * D_MODEL

LANES = 128
HALO = 8
MOD_ROWS = 16
VMEM_LIMIT = 56 * 1024 * 1024
TOKEN_TILE = 256
NEG_INF = float("-inf")
LOG2E = math.log2(math.e)
SCORE_SCALE = ATTN_SCALE * LOG2E


def _params(*sem):
    return pltpu.CompilerParams(dimension_semantics=sem, vmem_limit_bytes=VMEM_LIMIT)


def _const_spec(shape):
    nd = len(shape)
    return pl.BlockSpec(shape, lambda *_: (0,) * nd, pipeline_mode=pl.Buffered(1))


def _bdot(a, b):
    return jnp.dot(a.astype(BF16), b.astype(BF16), preferred_element_type=F32)


def _bdot_nt(a, b):
    return lax.dot_general(a.astype(BF16), b.astype(BF16), (((1,), (1,)), ((), ())),
                           preferred_element_type=F32)


def _sigmoid(x):
    return 1.0 / (1.0 + jnp.exp2(x * -LOG2E))


def _silu(x):
    return x * _sigmoid(x)


def _softplus(x):
    return jnp.maximum(x, 0.0) + jnp.log1p(jnp.exp(-jnp.abs(x)))


def _rms_mod(x, g, scale, shift):
    ms = jnp.mean(x * x, axis=-1, keepdims=True)
    return (x * lax.rsqrt(ms + EPS)) * g * (1.0 + scale) + shift


def _mod_kernel(c_ref, w_ref, b_ref, o_ref):
    o_ref[0] = _bdot(_silu(c_ref[...]), w_ref[0]) + b_ref[0]


def _modulation(cvecs, w_mod, b_mod):
    n = MOD_CHUNKS * D_MODEL
    tn = n // 4
    out = pl.pallas_call(
        _mod_kernel,
        grid=(DEPTH, n // tn),
        in_specs=[pl.BlockSpec((MOD_ROWS, D_MODEL), lambda l, j: (0, 0)),
                  pl.BlockSpec((1, D_MODEL, tn), lambda l, j: (l, 0, j)),
                  pl.BlockSpec((1, 1, tn), lambda l, j: (l, 0, j))],
        out_specs=pl.BlockSpec((1, MOD_ROWS, tn), lambda l, j: (l, 0, j)),
        out_shape=jax.ShapeDtypeStruct((DEPTH, MOD_ROWS, n), F32),
        compiler_params=_params("arbitrary", "arbitrary"),
        name="modulation",
    )(cvecs, w_mod, b_mod.reshape(DEPTH, 1, n))
    return out.reshape(DEPTH * MOD_ROWS, 1, n)


def _mod_spec(layer, per_batch):
    base = layer * MOD_ROWS
    if per_batch:
        return pl.BlockSpec((1, 1, MOD_CHUNKS * D_MODEL), lambda b, i: (base + 1 + b, 0, 0))
    return pl.BlockSpec((1, 1, MOD_CHUNKS * D_MODEL), lambda b, i: (base, 0, 0))


DT_PAD = 2 * LANES


def _inproj_kernel(x_ref, mod_ref, g_ref, wa_ref, wz_ref, wx_ref, wd_ref, wg_ref,
                   u_ref, q_ref, k_ref, v_ref, z_ref, xbc_ref, dt_ref, gate_ref):
    mod = mod_ref[0]
    h = _rms_mod(x_ref[0], g_ref[...], mod[:, D_MODEL:2 * D_MODEL], mod[:, 0:D_MODEL]).astype(BF16)
    a = jnp.dot(h, wa_ref[...], preferred_element_type=F32)
    u_ref[0] = a[:, 0:POOL_WIDTH]
    q_ref[0] = a[:, POOL_WIDTH:POOL_WIDTH + ATTN_WIDTH]
    k_ref[0] = a[:, POOL_WIDTH + ATTN_WIDTH:POOL_WIDTH + ATTN_WIDTH + KV_WIDTH]
    v_ref[0] = a[:, POOL_WIDTH + ATTN_WIDTH + KV_WIDTH:]
    z_ref[0] = jnp.dot(h, wz_ref[...], preferred_element_type=F32)
    xbc_ref[0] = jnp.dot(h, wx_ref[...], preferred_element_type=F32)
    dt_ref[0] = jnp.dot(h, wd_ref[...], preferred_element_type=F32)
    gate_ref[0] = jnp.dot(h, wg_ref[...], preferred_element_type=F32)


def _in_projection(x, mods, layer, per_batch, norm_g, w):
    b, L, _ = x.shape
    tm = TOKEN_TILE
    widths = (POOL_WIDTH, ATTN_WIDTH, KV_WIDTH, KV_WIDTH, SSD_INNER, CONV_CH, DT_PAD, N_BRANCH * D_MODEL)
    tok = lambda n: pl.BlockSpec((1, tm, n), lambda bi, i: (bi, i, 0))
    return pl.pallas_call(
        _inproj_kernel,
        grid=(b, L // tm),
        in_specs=[tok(D_MODEL), _mod_spec(layer, per_batch), _const_spec((1, D_MODEL)),
                  _const_spec(w["a"].shape), _const_spec(w["z"].shape), _const_spec(w["xbc"].shape),
                  _const_spec(w["dt"].shape), _const_spec(w["gate"].shape)],
        out_specs=[tok(n) for n in widths],
        out_shape=[jax.ShapeDtypeStruct((b, L, n), F32) for n in widths],
        compiler_params=_params("parallel", "parallel"),
        name="in_projection",
    )(x, mods, norm_g, w["a"], w["z"], w["xbc"], w["dt"], w["gate"])


def _halo_specs(rows, cols, L):
    per = rows // HALO
    last = L // HALO - 1
    main = pl.BlockSpec((1, rows, cols), lambda b, i: (b, i, 0))
    prev = pl.BlockSpec((1, HALO, cols), lambda b, i: (b, jnp.maximum(i * per - 1, 0), 0))
    nxt = pl.BlockSpec((1, HALO, cols), lambda b, i: (b, jnp.minimum((i + 1) * per, last), 0))
    return main, prev, nxt


def _pool_kernel(u_ref, up_ref, un_ref, pw_ref, ps_ref, o_ref, pad_ref, *, rows, L):
    i = pl.program_id(1)
    n = pl.num_programs(1)
    pad_ref[0:HALO, :] = jnp.where(i > 0, up_ref[0], 0.0)
    pad_ref[HALO:HALO + rows, :] = u_ref[0]
    pad_ref[HALO + rows:2 * HALO + rows, :] = jnp.where(i < n - 1, un_ref[0], 0.0)
    t = i * rows + lax.broadcasted_iota(jnp.int32, (rows, POOL_GROUP_W), 0)
    for gi, w in enumerate(POOL_WINDOWS):
        cols = slice(gi * POOL_GROUP_W, (gi + 1) * POOL_GROUP_W)
        acc = pad_ref[HALO - w // 2:HALO - w // 2 + rows, cols]
        for d in range(-w // 2 + 1, w // 2):
            acc = acc + pad_ref[HALO + d:HALO + d + rows, cols]
        cnt = (jnp.minimum(t + w // 2, L) - jnp.maximum(t - w // 2, 0)).astype(F32)
        pooled = acc / cnt - pad_ref[HALO:HALO + rows, cols]
        o_ref[0, :, cols] = _bdot(pooled, pw_ref[gi]) * ps_ref[:, cols]


def _pool_mixer(u, pool_w, pool_scale):
    b, L, _ = u.shape
    rows = 256
    assert max(POOL_WINDOWS) // 2 <= HALO and L % rows == 0
    main, prev, nxt = _halo_specs(rows, POOL_WIDTH, L)
    return pl.pallas_call(
        functools.partial(_pool_kernel, rows=rows, L=L),
        grid=(b, L // rows),
        in_specs=[main, prev, nxt, _const_spec(pool_w.shape), _const_spec(pool_scale.shape)],
        out_specs=main,
        out_shape=jax.ShapeDtypeStruct((b, L, POOL_WIDTH), F32),
        scratch_shapes=[pltpu.VMEM((rows + 2 * HALO, POOL_WIDTH), F32)],
        compiler_params=_params("parallel", "parallel"),
        name="pool_mixer",
    )(u, u, u, pool_w, pool_scale)


def _head_rms(x, block_mean, g):
    sq = x * x
    hi = sq.astype(BF16)
    lo = (sq - hi.astype(F32)).astype(BF16)
    ms = jnp.concatenate(
        [jnp.dot(hi[:, c:c + LANES], block_mean, preferred_element_type=F32)
         + jnp.dot(lo[:, c:c + LANES], block_mean, preferred_element_type=F32)
         for c in range(0, x.shape[-1], LANES)], axis=-1)
    return x * lax.rsqrt(ms + EPS) * g


def _rope(x, cos, sin_signed):
    width = x.shape[-1]
    lane = lax.broadcasted_iota(jnp.int32, x.shape, 1)
    partner = jnp.where((lane & 16) == 0, pltpu.roll(x, width - 16, 1), pltpu.roll(x, 16, 1))
    return x * cos + partner * sin_signed


def _ctx_attn_kernel(sink_ref, q_ref, k_ref, v_ref, qg_ref, kg_ref, bm_ref, o_ref, kn_ref):
    kn = _head_rms(k_ref[0], bm_ref[...], kg_ref[...])
    kn_ref[0] = kn
    qn = (_head_rms(q_ref[0], bm_ref[...], qg_ref[...]) * SCORE_SCALE).astype(BF16)
    knb = kn.astype(BF16)
    vb = v_ref[0].astype(BF16)
    outs = []
    for h in range(N_Q_HEADS):
        j = h // GQA_GROUP
        kv = slice(j * HEAD_DIM, (j + 1) * HEAD_DIM)
        s = _bdot_nt(qn[:, h * HEAD_DIM:(h + 1) * HEAD_DIM], knb[:, kv])
        sink = sink_ref[h] * LOG2E
        m = jnp.maximum(jnp.max(s, axis=-1, keepdims=True), sink)
        p = jnp.exp2(s - m)
        denom = jnp.sum(p, axis=-1, keepdims=True) + jnp.exp2(sink - m)
        outs.append(_bdot(p, vb[:, kv]) / denom)
    o_ref[0] = jnp.concatenate(outs, axis=-1)


def _context_attention(q, k, v, sink, qg, kg, bm):
    b, L, _ = q.shape
    seq = lambda n: pl.BlockSpec((1, L, n), lambda bi: (bi, 0, 0))
    return pl.pallas_call(
        _ctx_attn_kernel,
        grid=(b,),
        in_specs=[pl.BlockSpec(memory_space=pltpu.SMEM), seq(ATTN_WIDTH), seq(KV_WIDTH), seq(KV_WIDTH),
                  _const_spec(qg.shape), _const_spec(kg.shape), _const_spec(bm.shape)],
        out_specs=[seq(ATTN_WIDTH), seq(KV_WIDTH)],
        out_shape=[jax.ShapeDtypeStruct((b, L, ATTN_WIDTH), F32), jax.ShapeDtypeStruct((b, L, KV_WIDTH), F32)],
        compiler_params=_params("parallel"),
        name="context_attention",
    )(sink, q, k, v, qg, kg, bm)


def _lat_attn_kernel(sink_ref, q_ref, k_ref, v_ref, kc_ref, vc_ref, cosq_ref, sinq_ref, cos_k_ref, sin_k_ref,
                     band_ref, qg_ref, kg_ref, bm_ref, o_ref, kpad_ref, vpad_ref, kcb_ref, vcb_ref, *, L):
    n = pl.program_id(1)

    @pl.when(n == 0)
    def _():
        kn = _head_rms(k_ref[0], bm_ref[...], kg_ref[...])
        zeros = jnp.zeros((BLOCK, KV_WIDTH), BF16)
        kpad_ref[0:BLOCK, :] = zeros
        kpad_ref[BLOCK + L:2 * BLOCK + L, :] = zeros
        kpad_ref[BLOCK:BLOCK + L, :] = _rope(kn, cos_k_ref[...], sin_k_ref[...]).astype(BF16)
        vpad_ref[0:BLOCK, :] = zeros
        vpad_ref[BLOCK + L:2 * BLOCK + L, :] = zeros
        vpad_ref[BLOCK:BLOCK + L, :] = v_ref[0].astype(BF16)
        kcb_ref[...] = kc_ref[0, 0].astype(BF16)
        vcb_ref[...] = vc_ref[0, 0].astype(BF16)

    qn = _head_rms(q_ref[0], bm_ref[...], qg_ref[...])
    qr = (_rope(qn, cosq_ref[...], sinq_ref[...]) * SCORE_SCALE).astype(BF16)
    start = pl.multiple_of(n * BLOCK, BLOCK)
    kl = kpad_ref[pl.ds(start, 3 * BLOCK), :]
    vl = vpad_ref[pl.ds(start, 3 * BLOCK), :]
    kc = kcb_ref[...]
    vc = vcb_ref[...]
    band = band_ref[0]

    outs = []
    for j in range(N_KV_HEADS):
        kv = slice(j * HEAD_DIM, (j + 1) * HEAD_DIM)
        qs = jnp.concatenate([qr[:, (j * GQA_GROUP + g) * HEAD_DIM:(j * GQA_GROUP + g + 1) * HEAD_DIM]
                              for g in range(GQA_GROUP)], axis=0)
        s_loc_all = _bdot_nt(qs, kl[:, kv])
        s_ctx_all = _bdot_nt(qs, kc[:, kv])
        p_loc, p_ctx, denoms = [], [], []
        for g in range(GQA_GROUP):
            rows = slice(g * BLOCK, (g + 1) * BLOCK)
            s_loc = s_loc_all[rows] + band
            s_ctx = s_ctx_all[rows]
            sink = sink_ref[j * GQA_GROUP + g] * LOG2E
            m = jnp.maximum(jnp.maximum(jnp.max(s_loc, axis=-1, keepdims=True),
                                        jnp.max(s_ctx, axis=-1, keepdims=True)), sink)
            pl_g = jnp.exp2(s_loc - m)
            pc_g = jnp.exp2(s_ctx - m)
            denoms.append(jnp.sum(pl_g, axis=-1, keepdims=True) + jnp.sum(pc_g, axis=-1, keepdims=True)
                          + jnp.exp2(sink - m))
            p_loc.append(pl_g.astype(BF16))
            p_ctx.append(pc_g.astype(BF16))
        o = (jnp.dot(jnp.concatenate(p_loc, axis=0), vl[:, kv], preferred_element_type=F32)
             + jnp.dot(jnp.concatenate(p_ctx, axis=0), vc[:, kv], preferred_element_type=F32))
        outs.extend(o[g * BLOCK:(g + 1) * BLOCK] / denoms[g] for g in range(GQA_GROUP))
    o_ref[0] = jnp.concatenate(outs, axis=-1)


def _latent_attention(q, k, v, cache_k, cache_v, layer, sink, cos_tab, sin_tab, qg, kg, bm):
    b, L, _ = q.shape
    past = cache_k.shape[2]
    blk = lambda n: pl.BlockSpec((1, BLOCK, n), lambda bi, i: (bi, i, 0))
    seq = lambda n: pl.BlockSpec((1, L, n), lambda bi, i: (bi, 0, 0))
    cache = pl.BlockSpec((1, 1, past, KV_WIDTH), lambda bi, i: (bi, layer, 0, 0))
    tabq = pl.BlockSpec((BLOCK, ATTN_WIDTH), lambda bi, i: (i, 0))
    tabk = pl.BlockSpec((L, KV_WIDTH), lambda bi, i: (0, 0))
    nb = L // BLOCK
    band_spec = pl.BlockSpec((1, BLOCK, 3 * BLOCK),
                             lambda bi, i: (jnp.minimum(i, 1) + jnp.maximum(i - (nb - 2), 0), 0, 0))
    return pl.pallas_call(
        functools.partial(_lat_attn_kernel, L=L),
        grid=(b, nb),
        in_specs=[pl.BlockSpec(memory_space=pltpu.SMEM), blk(ATTN_WIDTH), seq(KV_WIDTH), seq(KV_WIDTH),
                  cache, cache, tabq, tabq, tabk, tabk, band_spec,
                  _const_spec(qg.shape), _const_spec(kg.shape), _const_spec(bm.shape)],
        out_specs=blk(ATTN_WIDTH),
        out_shape=jax.ShapeDtypeStruct((b, L, ATTN_WIDTH), F32),
        scratch_shapes=[pltpu.VMEM((L + 2 * BLOCK, KV_WIDTH), BF16), pltpu.VMEM((L + 2 * BLOCK, KV_WIDTH), BF16),
                        pltpu.VMEM((past, KV_WIDTH), BF16), pltpu.VMEM((past, KV_WIDTH), BF16)],
        compiler_params=_params("parallel", "arbitrary"),
        name="latent_attention",
    )(sink, q, k, v, cache_k, cache_v, cos_tab, sin_tab, cos_tab, sin_tab, _band_bias(), qg, kg, bm)


def _band_bias():
    r = jnp.arange(BLOCK)[:, None]
    col = jnp.arange(3 * BLOCK)[None, :]
    band = jnp.abs(col - BLOCK - r) <= WINDOW
    first = band & (col >= BLOCK)
    last = band & (col < 2 * BLOCK)
    return jnp.where(jnp.stack([first, band, last]), 0.0, NEG_INF).astype(F32)


def _rope_tables(L):
    t = jnp.arange(L)
    row = (t // GRID_W).astype(F32)
    col = (t % GRID_W).astype(F32)
    nf = HEAD_DIM // 4
    inv = ROPE_BASE ** (-jnp.arange(nf, dtype=F32) / nf)
    ang_r = row[:, None] * inv[None, :]
    ang_c = col[:, None] * inv[None, :]
    cos = jnp.concatenate([jnp.cos(ang_r), jnp.cos(ang_r), jnp.cos(ang_c), jnp.cos(ang_c)], axis=-1)
    sin = jnp.concatenate([-jnp.sin(ang_r), jnp.sin(ang_r), -jnp.sin(ang_c), jnp.sin(ang_c)], axis=-1)
    return jnp.tile(cos, (1, N_Q_HEADS)), jnp.tile(sin, (1, N_Q_HEADS))


def _block_mean():
    i = jnp.arange(LANES) // HEAD_DIM
    return jnp.where(i[:, None] == i[None, :], 1.0 / HEAD_DIM, 0.0).astype(BF16)


SHIFT_K = 2 * LANES

def _ssd_conv_chunk(xbc_ref, xp_ref, xn_ref, cw_ref, cb_ref, shift_ref, has_prev, has_next):
    main = xbc_ref[0]
    rows = jnp.concatenate([jnp.where(has_prev, xp_ref[0], 0.0), main, jnp.where(has_next, xn_ref[0], 0.0),
                            jnp.zeros((SHIFT_K - CHUNK - 2 * HALO, CONV_CH), F32)], axis=0).astype(BF16)
    taps = jnp.dot(shift_ref[...], rows, preferred_element_type=F32)
    acc = cb_ref[...] + main * cw_ref[CONV_W // 2:CONV_W // 2 + 1, :]
    for i, kk in enumerate(k for k in range(CONV_W) if k != CONV_W // 2):
        acc = acc + taps[i * CHUNK:(i + 1) * CHUNK] * cw_ref[kk:kk + 1, :]
    return _silu(acc)


def _ssd_scan_prelude(xc, dt_raw, dtt_raw, tri, tri_t, bias, bias_t, a_log, a_log_t, end_row):
    a_neg = -jnp.exp(a_log) * LOG2E
    a_neg_t = -jnp.exp(a_log_t) * LOG2E
    dt = _softplus(dt_raw + bias)
    dtt = _softplus(dtt_raw + bias_t)
    acum = jnp.dot(tri, dt * a_neg, preferred_element_type=F32, precision=HIGHEST)
    acum_t = jnp.dot(dtt * a_neg_t, tri_t, preferred_element_type=F32, precision=HIGHEST)
    last_t = acum_t[:, end_row:end_row + 1]
    return {"xc": xc, "xs_t": xc[:, 0:SSD_INNER].T,
            "mask": tri > 0.0, "acum": acum, "acum_t": acum_t, "dt_t": dtt,
            "chunk_decay_t": jnp.exp2(last_t),
            "decay_in_t": jnp.exp2(acum_t),
            "coef_t": jnp.exp2(last_t - acum_t) * dtt}


def _ssd_scan_chunks(scans):
    zero = jnp.zeros((SSD_HEADDIM, CHUNK), BF16)
    ys = [[] for _ in scans]
    for g in range(SSD_GROUPS):
        hs = slice(g * HEADS_PER_GROUP * SSD_HEADDIM, (g + 1) * HEADS_PER_GROUP * SSD_HEADDIM)
        heads = range(g * HEADS_PER_GROUP, (g + 1) * HEADS_PER_GROUP)
        stage = []
        for t, s_ref in scans:
            xc = t["xc"]
            bm = xc[:, SSD_INNER + g * D_STATE:SSD_INNER + (g + 1) * D_STATE].astype(BF16)
            cm = xc[:, SSD_INNER + (SSD_GROUPS + g) * D_STATE:SSD_INNER + (SSD_GROUPS + g + 1) * D_STATE].astype(BF16)
            cbm = jnp.where(t["mask"], _bdot_nt(cm, bm), 0.0)
            state = s_ref[hs, :]
            y_off = _bdot_nt(state, cm)
            y_off = jnp.concatenate(
                [y_off[(h - heads[0]) * SSD_HEADDIM:(h - heads[0] + 1) * SSD_HEADDIM] * t["decay_in_t"][h:h + 1, :]
                 for h in heads], axis=0)
            stage.append((bm, cbm, state, y_off, []))
        for pair in range(HEADS_PER_GROUP // 2):
            for (t, _), (_, cbm, _, _, y_diag) in zip(scans, stage):
                ws, xs = [], []
                for h in (heads[0] + 2 * pair, heads[0] + 2 * pair + 1):
                    a_col = jnp.broadcast_to(t["acum"][:, h:h + 1], (CHUNK, CHUNK))
                    ws.append((cbm * jnp.exp2(jnp.minimum(a_col - t["acum_t"][h:h + 1, :], 0.0))).astype(BF16))
                    xs.append((t["xs_t"][h * SSD_HEADDIM:(h + 1) * SSD_HEADDIM, :] * t["dt_t"][h:h + 1, :]).astype(BF16))
                lhs = jnp.concatenate([jnp.concatenate([xs[0], zero], axis=1),
                                       jnp.concatenate([zero, xs[1]], axis=1)], axis=0)
                y_diag.append(_bdot_nt(lhs, jnp.concatenate(ws, axis=1)))
        for i, ((t, s_ref), (bm, _, state, y_off, y_diag)) in enumerate(zip(scans, stage)):
            ys[i].append(jnp.concatenate(y_diag, axis=0) + y_off)
            x_state = jnp.concatenate(
                [(t["xs_t"][h * SSD_HEADDIM:(h + 1) * SSD_HEADDIM, :] * t["coef_t"][h:h + 1, :]).astype(BF16)
                 for h in heads], axis=0)
            update = jnp.dot(x_state, bm, preferred_element_type=F32)
            for h in heads:
                hp = slice(h * SSD_HEADDIM, (h + 1) * SSD_HEADDIM)
                rel = slice((h - heads[0]) * SSD_HEADDIM, (h - heads[0] + 1) * SSD_HEADDIM)
                s_ref[hp, :] = state[rel] * t["chunk_decay_t"][h:h + 1, :] + update[rel]
    return [jnp.concatenate(y, axis=0).T for y in ys]


def _ssd_kernel(xf_ref, xfp_ref, xfn_ref, xb_ref, xbp_ref, xbn_ref, zf_ref, zb_ref, dtf_ref, dtb_ref,
                dttf_ref, dttb_ref, init_ref, tri_ref, shift_ref, cw_ref, cb_ref, bias_ref, biast_ref,
                alog_ref, alogt_ref, dskip_ref, ng_ref, o_ref, fin_ref, xc_ref, sf_ref, sb_ref, *, nc, has_init):
    s = pl.program_id(1)
    half = nc // 2
    first_half = s < half
    cf = s
    cb = nc - 1 - s
    rf = pl.multiple_of(cf * CHUNK, CHUNK)
    rb = pl.multiple_of(cb * CHUNK, CHUNK)

    @pl.when(first_half)
    def _():
        xc_ref[pl.ds(rf, CHUNK), :] = _ssd_conv_chunk(xf_ref, xfp_ref, xfn_ref, cw_ref, cb_ref, shift_ref,
                                                      cf > 0, cf < nc - 1)
        xc_ref[pl.ds(rb, CHUNK), :] = _ssd_conv_chunk(xb_ref, xbp_ref, xbn_ref, cw_ref, cb_ref, shift_ref,
                                                      cb > 0, cb < nc - 1)

    @pl.when(s == 0)
    def _():
        if has_init:
            sf_ref[...] = init_ref[0, 0, 0]
            sb_ref[...] = init_ref[0, 0, 1]
        else:
            sf_ref[...] = jnp.zeros(sf_ref.shape, F32)
            sb_ref[...] = jnp.zeros(sb_ref.shape, F32)

    lower = tri_ref[0]
    upper = tri_ref[1]
    xcf = xc_ref[pl.ds(rf, CHUNK), :]
    xcb = xc_ref[pl.ds(rb, CHUNK), :]
    tf = _ssd_scan_prelude(xcf, dtf_ref[0], dttf_ref[0, 0], lower, upper, bias_ref[0], biast_ref[0],
                           alog_ref[0], alogt_ref[0], CHUNK - 1)
    tb = _ssd_scan_prelude(xcb, dtb_ref[0], dttb_ref[0, 0], upper, lower, bias_ref[1], biast_ref[1],
                           alog_ref[1], alogt_ref[1], 0)
    yf, yb = _ssd_scan_chunks([(tf, sf_ref), (tb, sb_ref)])

    @pl.when(first_half)
    def _():
        o_ref[0, pl.ds(rf, CHUNK), :] = yf
        o_ref[0, pl.ds(rb, CHUNK), :] = yb

    def finish(rows, y, xc, z):
        tot = o_ref[0, pl.ds(rows, CHUNK), :] + y + dskip_ref[...] * xc[:, 0:SSD_INNER]
        tot = tot * _silu(z)
        ms = jnp.mean(tot * tot, axis=-1, keepdims=True)
        o_ref[0, pl.ds(rows, CHUNK), :] = tot * lax.rsqrt(ms + EPS) * ng_ref[...]

    @pl.when(jnp.logical_not(first_half))
    def _():
        finish(rf, yf, xcf, zf_ref[0])
        finish(rb, yb, xcb, zb_ref[0])

    @pl.when(s == nc - 1)
    def _():
        fin_ref[0, 0] = sf_ref[...]
        fin_ref[0, 1] = sb_ref[...]


def _ssd_mixer(z, xbc, dt_raw, init_state, layer, consts, p):
    b, L, _ = z.shape
    nc = L // CHUNK
    has_init = init_state is not None
    dt_t = jnp.stack([dt_raw[:, :, 0:SSD_HEADS], dt_raw[:, :, LANES:LANES + SSD_HEADS]], axis=1)
    dt_t = jnp.swapaxes(dt_t, 2, 3)
    assert nc % 2 == 0
    half = nc // 2
    if not has_init:
        init_state = jnp.zeros((1, 1, 1, HALO, LANES), F32)
        init_spec = pl.BlockSpec((1, 1, 1, HALO, LANES), lambda bi, s: (0, 0, 0, 0, 0))
    else:
        init_spec = pl.BlockSpec((1, 1, 2, SSD_INNER, D_STATE), lambda bi, s: (bi, layer, 0, 0, 0))
    per = CHUNK // HALO
    last = L // HALO - 1
    conv_f = lambda s: jnp.minimum(s, half - 1)
    conv_b = lambda s: jnp.maximum(nc - 1 - s, half)
    fin_f = lambda s: jnp.maximum(s, half)
    fin_b = lambda s: jnp.minimum(nc - 1 - s, half - 1)

    def conv_specs(chunk):
        return [pl.BlockSpec((1, CHUNK, CONV_CH), lambda bi, s: (bi, chunk(s), 0)),
                pl.BlockSpec((1, HALO, CONV_CH), lambda bi, s: (bi, jnp.maximum(chunk(s) * per - 1, 0), 0)),
                pl.BlockSpec((1, HALO, CONV_CH), lambda bi, s: (bi, jnp.minimum((chunk(s) + 1) * per, last), 0))]

    in_specs = conv_specs(conv_f) + conv_specs(conv_b) + [
        pl.BlockSpec((1, CHUNK, SSD_INNER), lambda bi, s: (bi, fin_f(s), 0)),
        pl.BlockSpec((1, CHUNK, SSD_INNER), lambda bi, s: (bi, fin_b(s), 0)),
        pl.BlockSpec((1, CHUNK, LANES), lambda bi, s: (bi, s, 0)),
        pl.BlockSpec((1, CHUNK, LANES), lambda bi, s: (bi, nc - 1 - s, 1)),
        pl.BlockSpec((1, 1, SSD_HEADS, CHUNK), lambda bi, s: (bi, 0, 0, s)),
        pl.BlockSpec((1, 1, SSD_HEADS, CHUNK), lambda bi, s: (bi, 1, 0, nc - 1 - s)),
        init_spec,
        _const_spec((2, CHUNK, CHUNK)), _const_spec(((CONV_W - 1) * CHUNK, SHIFT_K)),
        _const_spec((CONV_W, CONV_CH)), _const_spec((1, CONV_CH)),
        _const_spec((2, 1, LANES)), _const_spec((2, SSD_HEADS, 1)), _const_spec((2, 1, LANES)),
        _const_spec((2, SSD_HEADS, 1)), _const_spec((1, SSD_INNER)), _const_spec((1, SSD_INNER)),
    ]
    out, fin = pl.pallas_call(
        functools.partial(_ssd_kernel, nc=nc, has_init=has_init),
        grid=(b, nc),
        in_specs=in_specs,
        out_specs=[pl.BlockSpec((1, L, SSD_INNER), lambda bi, s: (bi, 0, 0)),
                   pl.BlockSpec((1, 2, SSD_INNER, D_STATE), lambda bi, s: (bi, 0, 0, 0))],
        out_shape=[jax.ShapeDtypeStruct((b, L, SSD_INNER), F32),
                   jax.ShapeDtypeStruct((b, 2, SSD_INNER, D_STATE), F32)],
        scratch_shapes=[pltpu.VMEM((L, CONV_CH), F32), pltpu.VMEM((SSD_INNER, D_STATE), F32),
                        pltpu.VMEM((SSD_INNER, D_STATE), F32)],
        compiler_params=_params("parallel", "arbitrary"),
        name="ssd_mixer",
    )(xbc, xbc, xbc, xbc, xbc, xbc, z, z, dt_raw, dt_raw, dt_t, dt_t, init_state, consts["tri"], consts["shift"],
      p["conv_w"], p["conv_b"], p["dt_bias"], p["dt_bias_t"], p["a_log"], p["a_log_t"], p["d_skip"], p["ssd_norm_g"])
    return out, fin


def _ssd_consts():
    i = jnp.arange(CHUNK)
    lower = (i[None, :] <= i[:, None]).astype(F32)
    t = jnp.arange(CHUNK)[:, None]
    col = jnp.arange(SHIFT_K)[None, :]
    shift = jnp.concatenate([(col == HALO + t + kk - CONV_W // 2) for kk in range(CONV_W) if kk != CONV_W // 2], axis=0)
    return {"tri": jnp.stack([lower, lower.T]), "shift": shift.astype(BF16)}


FF_CHUNK = 1024


def _merge_mlp_kernel(x_ref, mod_ref, pool_ref, attn_ref, ssd_ref, gate_ref, g_ref,
                      wp_ref, wa_ref, ws_ref, wo_ref, w1_ref, w2_ref, o_ref):
    mod = mod_ref[0]
    gates = _sigmoid(gate_ref[0])
    merged = (gates[:, 0:D_MODEL] * _bdot(pool_ref[0], wp_ref[...])
              + gates[:, D_MODEL:2 * D_MODEL] * _bdot(attn_ref[0], wa_ref[...])
              + gates[:, 2 * D_MODEL:] * _bdot(ssd_ref[0], ws_ref[...]))
    x = x_ref[0] + mod[:, 2 * D_MODEL:3 * D_MODEL] * _bdot(merged, wo_ref[...])
    h = _rms_mod(x, g_ref[...], mod[:, 4 * D_MODEL:5 * D_MODEL], mod[:, 3 * D_MODEL:4 * D_MODEL]).astype(BF16)
    acc = jnp.zeros(x.shape, F32)
    for j in range(D_FF // FF_CHUNK):
        ff = slice(j * FF_CHUNK, (j + 1) * FF_CHUNK)
        a = jnp.dot(h, w1_ref[:, ff], preferred_element_type=F32)
        acc = acc + _bdot(jnp.square(jnp.maximum(a, 0.0)), w2_ref[ff, :])
    o_ref[0] = x + mod[:, 5 * D_MODEL:] * acc


def _merge_mlp(x, mods, layer, per_batch, pool, attn, ssd, gate, norm_g, w):
    b, L, _ = x.shape
    tm = TOKEN_TILE
    tok = lambda n: pl.BlockSpec((1, tm, n), lambda bi, i: (bi, i, 0))
    weights = [w["pool_o"], w["attn_o"], w["ssd_o"], w["out"], w["mlp1"], w["mlp2"]]
    return pl.pallas_call(
        _merge_mlp_kernel,
        grid=(b, L // tm),
        in_specs=[tok(D_MODEL), _mod_spec(layer, per_batch), tok(POOL_WIDTH), tok(ATTN_WIDTH), tok(SSD_INNER),
                  tok(N_BRANCH * D_MODEL), _const_spec((1, D_MODEL))] + [_const_spec(a.shape) for a in weights],
        out_specs=tok(D_MODEL),
        out_shape=jax.ShapeDtypeStruct((b, L, D_MODEL), F32),
        compiler_params=_params("parallel", "parallel"),
        name="merge_mlp",
    )(x, mods, pool, attn, ssd, gate, norm_g, *weights)


def _layer(x, mods, layer, per_batch, w, p, consts, ctx):
    b, L, _ = x.shape
    u, q, k, v, z, xbc, dt_raw, gate = _in_projection(x, mods, layer, per_batch, p["norm1_g"], w)
    pool = _pool_mixer(u, w["pool_w"], p["pool_scale"])
    if ctx is None:
        attn, k_norm = _context_attention(q, k, v, p["attn_sink"], p["q_norm_g"], p["k_norm_g"], consts["bm"])
        init = None
    else:
        cache_k, cache_v, init = ctx
        cos_tab, sin_tab = consts["rope"]
        attn = _latent_attention(q, k, v, cache_k, cache_v, layer, p["attn_sink"], cos_tab, sin_tab,
                                 p["q_norm_g"], p["k_norm_g"], consts["bm"])
        k_norm = None
    ssd, fin = _ssd_mixer(z, xbc, dt_raw, init, layer, consts, p)
    x = _merge_mlp(x, mods, layer, per_batch, pool, attn, ssd, gate, p["norm2_g"], w)
    return x, (k_norm, v, fin)


def _layer_weights(l, w_in, pool_w, w_pool_o, w_attn_o, w_ssd_o, w_out, w_mlp1, w_mlp2):
    wi = w_in[l]
    o_z = POOL_WIDTH + ATTN_WIDTH + 2 * KV_WIDTH
    o_x = o_z + SSD_INNER
    o_d = o_x + CONV_CH
    o_g = o_d + 2 * SSD_HEADS
    wd = wi[:, o_d:o_g]
    pad = jnp.zeros((D_MODEL, LANES - SSD_HEADS), F32)
    wd = jnp.concatenate([wd[:, :SSD_HEADS], pad, wd[:, SSD_HEADS:], pad], axis=1)
    cast = lambda a: a.astype(BF16)
    return {"a": cast(wi[:, :o_z]), "z": cast(wi[:, o_z:o_x]), "xbc": cast(wi[:, o_x:o_d]), "dt": cast(wd),
            "gate": cast(wi[:, o_g:]), "pool_w": cast(pool_w[l]), "pool_o": cast(w_pool_o[l]),
            "attn_o": cast(w_attn_o[l]), "ssd_o": cast(w_ssd_o[l]), "out": cast(w_out[l]),
            "mlp1": cast(w_mlp1[l]), "mlp2": cast(w_mlp2[l])}


def _pad_lanes(v):
    return jnp.pad(v, ((0, 0), (0, LANES - SSD_HEADS))).reshape(2, 1, LANES)


def kernel(x_prompt, x_sample, cache_k, cache_v, state_ssd, c, c_ctx, w_mod, b_mod, norm1_g, norm2_g, w_in, pool_w, pool_scale, w_pool_o, q_norm_g, k_norm_g, attn_sink, w_attn_o, conv_w, conv_b, dt_bias, a_log, d_skip, ssd_norm_g, w_ssd_o, w_out, w_mlp1, w_mlp2):
    batch, seq, _ = x_prompt.shape
    dec_batch, dec_seq, _ = x_sample.shape
    past = cache_k.shape[2]
    assert 1 + dec_batch <= MOD_ROWS

    cvecs = jnp.concatenate([c_ctx[None, :], c, jnp.zeros((MOD_ROWS - 1 - dec_batch, D_MODEL), F32)], axis=0)
    mods = _modulation(cvecs, w_mod, b_mod)

    consts = {"bm": _block_mean(), "rope": _rope_tables(dec_seq)}
    consts.update(_ssd_consts())
    cache_k = cache_k.reshape(dec_batch, DEPTH, past, KV_WIDTH)
    cache_v = cache_v.reshape(dec_batch, DEPTH, past, KV_WIDTH)
    state = state_ssd.reshape(dec_batch, DEPTH, 2, SSD_INNER, D_STATE)

    y_prompt, y_sample = x_prompt, x_sample
    ks, vs, ss = [], [], []
    for l in range(DEPTH):
        w = _layer_weights(l, w_in, pool_w, w_pool_o, w_attn_o, w_ssd_o, w_out, w_mlp1, w_mlp2)
        p = {"norm1_g": norm1_g[l][None, :], "norm2_g": norm2_g[l][None, :], "pool_scale": pool_scale[l][None, :],
             "q_norm_g": jnp.tile(q_norm_g[l], N_Q_HEADS)[None, :], "k_norm_g": jnp.tile(k_norm_g[l], N_KV_HEADS)[None, :],
             "attn_sink": attn_sink[l], "conv_w": conv_w[l], "conv_b": conv_b[l][None, :],
             "dt_bias": _pad_lanes(dt_bias[l]), "dt_bias_t": dt_bias[l][:, :, None],
             "a_log": _pad_lanes(a_log[l]), "a_log_t": a_log[l][:, :, None],
             "d_skip": jnp.repeat(d_skip[l], SSD_HEADDIM)[None, :], "ssd_norm_g": ssd_norm_g[l][None, :]}
        y_prompt, (k_l, v_l, s_l) = _layer(y_prompt, mods, l, False, w, p, consts, None)
        ks.append(k_l.reshape(batch, seq, N_KV_HEADS, HEAD_DIM))
        vs.append(v_l.reshape(batch, seq, N_KV_HEADS, HEAD_DIM))
        ss.append(s_l.reshape(batch, 2, SSD_HEADS, SSD_HEADDIM, D_STATE))
        y_sample, _ = _layer(y_sample, mods, l, True, w, p, consts, (cache_k, cache_v, state))
    return (y_prompt, y_sample, jnp.stack(ks, axis=1), jnp.stack(vs, axis=1), jnp.stack(ss, axis=1))
```

```python
import functools
import math

import jax
import jax.numpy as jnp
from jax import lax
from jax.experimental import pallas as pl
from jax.experimental.pallas import tpu as pltpu

F32 = jnp.float32
BF16 = jnp.bfloat16
HIGHEST = lax.Precision.HIGHEST

D_MODEL = 1024
DEPTH = 2
GRID_W = 64
MOD_CHUNKS = 6
EPS = 1e-6
POOL_WIDTH = D_MODEL // 2
POOL_GROUPS = 4
POOL_GROUP_W = POOL_WIDTH // POOL_GROUPS
POOL_WINDOWS = (2, 4, 8, 16)
N_Q_HEADS = 8
N_KV_HEADS = 2
GQA_GROUP = N_Q_HEADS // N_KV_HEADS
HEAD_DIM = 64
ATTN_WIDTH = N_Q_HEADS * HEAD_DIM
KV_WIDTH = N_KV_HEADS * HEAD_DIM
WINDOW = 128
BLOCK = 128
ROPE_BASE = 10000.0
ATTN_SCALE = HEAD_DIM ** -0.5
SSD_HEADS = 16
SSD_HEADDIM = 64
SSD_INNER = SSD_HEADS * SSD_HEADDIM
SSD_GROUPS = 2
HEADS_PER_GROUP = SSD_HEADS // SSD_GROUPS
D_STATE = 128
CONV_W = 5
CHUNK = 128
CONV_CH = SSD_INNER + 2 * SSD_GROUPS * D_STATE
N_BRANCH = 3
D_FF = 4 * D_MODEL

LANES = 128
HALO = 8
MOD_ROWS = 16
VMEM_LIMIT = 56 * 1024 * 1024
TOKEN_TILE = 256
NEG_INF = float("-inf")
LOG2E = math.log2(math.e)
SCORE_SCALE = ATTN_SCALE * LOG2E


def _params(*sem):
    return pltpu.CompilerParams(dimension_semantics=sem, vmem_limit_bytes=VMEM_LIMIT)


def _const_spec(shape):
    nd = len(shape)
    return pl.BlockSpec(shape, lambda *_: (0,) * nd, pipeline_mode=pl.Buffered(1))


def _bdot(a, b):
    return jnp.dot(a.astype(BF16), b.astype(BF16), preferred_element_type=F32)


def _bdot_nt(a, b):
    return lax.dot_general(a.astype(BF16), b.astype(BF16), (((1,), (1,)), ((), ())),
                           preferred_element_type=F32)


def _sigmoid(x):
    return 1.0 / (1.0 + jnp.exp2(x * -LOG2E))


def _silu(x):
    return x * _sigmoid(x)


def _softplus(x):
    return jnp.maximum(x, 0.0) + jnp.log1p(jnp.exp(-jnp.abs(x)))


def _rms_mod(x, g, scale, shift):
    ms = jnp.mean(x * x, axis=-1, keepdims=True)
    return (x * lax.rsqrt(ms + EPS)) * g * (1.0 + scale) + shift


def _mod_kernel(c_ref, w_ref, b_ref, o_ref):
    o_ref[0] = _bdot(_silu(c_ref[...]), w_ref[0]) + b_ref[0]


def _modulation(cvecs, w_mod, b_mod):
    n = MOD_CHUNKS * D_MODEL
    tn = n // 4
    out = pl.pallas_call(
        _mod_kernel,
        grid=(DEPTH, n // tn),
        in_specs=[pl.BlockSpec((MOD_ROWS, D_MODEL), lambda l, j: (0, 0)),
                  pl.BlockSpec((1, D_MODEL, tn), lambda l, j: (l, 0, j)),
                  pl.BlockSpec((1, 1, tn), lambda l, j: (l, 0, j))],
        out_specs=pl.BlockSpec((1, MOD_ROWS, tn), lambda l, j: (l, 0, j)),
        out_shape=jax.ShapeDtypeStruct((DEPTH, MOD_ROWS, n), F32),
        compiler_params=_params("arbitrary", "arbitrary"),
        name="modulation",
    )(cvecs, w_mod, b_mod.reshape(DEPTH, 1, n))
    return out.reshape(DEPTH * MOD_ROWS, 1, n)


def _mod_spec(layer, per_batch):
    base = layer * MOD_ROWS
    if per_batch:
        return pl.BlockSpec((1, 1, MOD_CHUNKS * D_MODEL), lambda b, i: (base + 1 + b, 0, 0))
    return pl.BlockSpec((1, 1, MOD_CHUNKS * D_MODEL), lambda b, i: (base, 0, 0))


DT_PAD = 2 * LANES


def _inproj_kernel(*refs, rope):
    x_ref, mod_ref, g_ref, qg_ref, kg_ref, bm_ref = refs[:6]
    refs = refs[6:]
    if rope:
        cos_ref, sin_ref = refs[:2]
        refs = refs[2:]
    wa_ref, wz_ref, wx_ref, wd_ref, wg_ref, u_ref, q_ref, k_ref, v_ref, z_ref, xbc_ref, dt_ref, gate_ref = refs
    mod = mod_ref[0]
    h = _rms_mod(x_ref[0], g_ref[...], mod[:, D_MODEL:2 * D_MODEL], mod[:, 0:D_MODEL]).astype(BF16)
    a = jnp.dot(h, wa_ref[...], preferred_element_type=F32)
    u_ref[0] = a[:, 0:POOL_WIDTH]
    v_ref[0] = a[:, POOL_WIDTH + ATTN_WIDTH + KV_WIDTH:].astype(v_ref.dtype)
    z_ref[0] = jnp.dot(h, wz_ref[...], preferred_element_type=F32)
    xbc_ref[0] = jnp.dot(h, wx_ref[...], preferred_element_type=F32)
    dt_ref[0] = jnp.dot(h, wd_ref[...], preferred_element_type=F32)
    gate_ref[0] = jnp.dot(h, wg_ref[...], preferred_element_type=F32)
    qn = _head_rms(a[:, POOL_WIDTH:POOL_WIDTH + ATTN_WIDTH], bm_ref[...], qg_ref[...])
    kn = _head_rms(a[:, POOL_WIDTH + ATTN_WIDTH:POOL_WIDTH + ATTN_WIDTH + KV_WIDTH], bm_ref[...], kg_ref[...])
    if rope:
        qn = _rope(qn, cos_ref[...], sin_ref[...])
        kn = _rope(kn, cos_ref[:, 0:KV_WIDTH], sin_ref[:, 0:KV_WIDTH])
    q_ref[0] = (qn * SCORE_SCALE).astype(q_ref.dtype)
    k_ref[0] = kn.astype(k_ref.dtype)


def _in_projection(x, mods, layer, per_batch, norm_g, qg, kg, bm, rope_tabs, w):
    b, L, _ = x.shape
    tm = TOKEN_TILE
    rope = rope_tabs is not None
    kv_dtype = BF16 if rope else F32
    outs = ((POOL_WIDTH, F32), (ATTN_WIDTH, BF16), (KV_WIDTH, kv_dtype), (KV_WIDTH, kv_dtype), (SSD_INNER, F32),
            (CONV_CH, F32), (DT_PAD, F32), (N_BRANCH * D_MODEL, F32))
    tok = lambda n: pl.BlockSpec((1, tm, n), lambda bi, i: (bi, i, 0))
    tab = pl.BlockSpec((tm, ATTN_WIDTH), lambda bi, i: (i, 0))
    weights = [w["a"], w["z"], w["xbc"], w["dt"], w["gate"]]
    return pl.pallas_call(
        functools.partial(_inproj_kernel, rope=rope),
        grid=(b, L // tm),
        in_specs=[tok(D_MODEL), _mod_spec(layer, per_batch), _const_spec((1, D_MODEL)), _const_spec(qg.shape),
                  _const_spec(kg.shape), _const_spec(bm.shape)] + ([tab, tab] if rope else [])
                 + [_const_spec(a.shape) for a in weights],
        out_specs=[tok(n) for n, _ in outs],
        out_shape=[jax.ShapeDtypeStruct((b, L, n), dt) for n, dt in outs],
        compiler_params=_params("parallel", "parallel"),
        name="in_projection",
    )(x, mods, norm_g, qg, kg, bm, *(rope_tabs if rope else ()), *weights)


def _halo_specs(rows, cols, L):
    per = rows // HALO
    last = L // HALO - 1
    main = pl.BlockSpec((1, rows, cols), lambda b, i: (b, i, 0))
    prev = pl.BlockSpec((1, HALO, cols), lambda b, i: (b, jnp.maximum(i * per - 1, 0), 0))
    nxt = pl.BlockSpec((1, HALO, cols), lambda b, i: (b, jnp.minimum((i + 1) * per, last), 0))
    return main, prev, nxt


POOL_ROWS = 256
POOL_K = POOL_ROWS + LANES


def _pool_kernel(u_ref, up_ref, un_ref, band_ref, pw_ref, ps_ref, o_ref, *, L):
    i = pl.program_id(1)
    n = pl.num_programs(1)
    main = u_ref[0]
    rows = jnp.concatenate([jnp.where(i > 0, up_ref[0], 0.0), main, jnp.where(i < n - 1, un_ref[0], 0.0),
                            jnp.zeros((POOL_K - POOL_ROWS - 2 * HALO, POOL_WIDTH), F32)], axis=0)
    hi = rows.astype(BF16)
    lo = (rows - hi.astype(F32)).astype(BF16)
    t = i * POOL_ROWS + lax.broadcasted_iota(jnp.int32, (POOL_ROWS, POOL_GROUP_W), 0)
    groups = [slice(gi * POOL_GROUP_W, (gi + 1) * POOL_GROUP_W) for gi in range(POOL_GROUPS)]
    sums = [jnp.dot(band_ref[gi], hi[:, cols], preferred_element_type=F32)
            + jnp.dot(band_ref[gi], lo[:, cols], preferred_element_type=F32) for gi, cols in enumerate(groups)]
    pooled = []
    for w, cols, acc in zip(POOL_WINDOWS, groups, sums):
        cnt = (jnp.minimum(t + w // 2, L) - jnp.maximum(t - w // 2, 0)).astype(F32)
        pooled.append((acc / cnt - main[:, cols]).astype(BF16))
    mixed = [jnp.dot(pooled[gi], pw_ref[gi], preferred_element_type=F32) for gi in range(POOL_GROUPS)]
    for cols, m in zip(groups, mixed):
        o_ref[0, :, cols] = m * ps_ref[:, cols]


def _pool_bands():
    d = jnp.arange(POOL_K)[None, :] - HALO - jnp.arange(POOL_ROWS)[:, None]
    return jnp.stack([(d >= -(w // 2)) & (d < w // 2) for w in POOL_WINDOWS]).astype(BF16)


def _pool_mixer(u, pool_w, pool_scale):
    b, L, _ = u.shape
    assert max(POOL_WINDOWS) // 2 <= HALO and L % POOL_ROWS == 0
    main, prev, nxt = _halo_specs(POOL_ROWS, POOL_WIDTH, L)
    bands = _pool_bands()
    return pl.pallas_call(
        functools.partial(_pool_kernel, L=L),
        grid=(b, L // POOL_ROWS),
        in_specs=[main, prev, nxt, _const_spec(bands.shape), _const_spec(pool_w.shape),
                  _const_spec(pool_scale.shape)],
        out_specs=main,
        out_shape=jax.ShapeDtypeStruct((b, L, POOL_WIDTH), F32),
        compiler_params=_params("parallel", "parallel"),
        name="pool_mixer",
    )(u, u, u, bands, pool_w, pool_scale)


def _head_rms(x, block_mean, g):
    sq = x * x
    hi = sq.astype(BF16)
    lo = (sq - hi.astype(F32)).astype(BF16)
    ms = jnp.concatenate(
        [jnp.dot(hi[:, c:c + LANES], block_mean, preferred_element_type=F32)
         + jnp.dot(lo[:, c:c + LANES], block_mean, preferred_element_type=F32)
         for c in range(0, x.shape[-1], LANES)], axis=-1)
    return x * lax.rsqrt(ms + EPS) * g


def _rope(x, cos, sin_signed):
    width = x.shape[-1]
    lane = lax.broadcasted_iota(jnp.int32, x.shape, 1)
    partner = jnp.where((lane & 16) == 0, pltpu.roll(x, width - 16, 1), pltpu.roll(x, 16, 1))
    return x * cos + partner * sin_signed


def _ctx_attn_kernel(sink_ref, q_ref, k_ref, v_ref, o_ref):
    qn = q_ref[0]
    knb = k_ref[0].astype(BF16)
    vb = v_ref[0].astype(BF16)
    outs = []
    for h in range(N_Q_HEADS):
        j = h // GQA_GROUP
        kv = slice(j * HEAD_DIM, (j + 1) * HEAD_DIM)
        s = _bdot_nt(qn[:, h * HEAD_DIM:(h + 1) * HEAD_DIM], knb[:, kv])
        sink = sink_ref[h] * LOG2E
        m = jnp.maximum(jnp.max(s, axis=-1, keepdims=True), sink)
        p = jnp.exp2(s - m)
        denom = jnp.sum(p, axis=-1, keepdims=True) + jnp.exp2(sink - m)
        outs.append(_bdot(p, vb[:, kv]) / denom)
    o_ref[0] = jnp.concatenate(outs, axis=-1)


def _context_attention(q, k, v, sink):
    b, L, _ = q.shape
    seq = lambda n: pl.BlockSpec((1, L, n), lambda bi: (bi, 0, 0))
    return pl.pallas_call(
        _ctx_attn_kernel,
        grid=(b,),
        in_specs=[pl.BlockSpec(memory_space=pltpu.SMEM), seq(ATTN_WIDTH), seq(KV_WIDTH), seq(KV_WIDTH)],
        out_specs=seq(ATTN_WIDTH),
        out_shape=jax.ShapeDtypeStruct((b, L, ATTN_WIDTH), F32),
        compiler_params=_params("parallel"),
        name="context_attention",
    )(sink, q, k, v)


def _lat_attn_kernel(sink_ref, q_ref, kp_ref, k0_ref, kn_ref, vp_ref, v0_ref, vn_ref, kc_ref, vc_ref, band_ref, o_ref):
    qr = q_ref[0]
    kl = jnp.concatenate([kp_ref[0], k0_ref[0], kn_ref[0]], axis=0)
    vl = jnp.concatenate([vp_ref[0], v0_ref[0], vn_ref[0]], axis=0)
    kc = kc_ref[0, 0]
    vc = vc_ref[0, 0]
    band = band_ref[0]

    outs = []
    for j in range(N_KV_HEADS):
        kv = slice(j * HEAD_DIM, (j + 1) * HEAD_DIM)
        qs = jnp.concatenate([qr[:, (j * GQA_GROUP + g) * HEAD_DIM:(j * GQA_GROUP + g + 1) * HEAD_DIM]
                              for g in range(GQA_GROUP)], axis=0)
        s_loc_all = _bdot_nt(qs, kl[:, kv])
        s_ctx_all = _bdot_nt(qs, kc[:, kv])
        p_loc, p_ctx, denoms = [], [], []
        for g in range(GQA_GROUP):
            rows = slice(g * BLOCK, (g + 1) * BLOCK)
            s_loc = s_loc_all[rows] + band
            s_ctx = s_ctx_all[rows]
            sink = sink_ref[j * GQA_GROUP + g] * LOG2E
            m = jnp.maximum(jnp.maximum(jnp.max(s_loc, axis=-1, keepdims=True),
                                        jnp.max(s_ctx, axis=-1, keepdims=True)), sink)
            pl_g = jnp.exp2(s_loc - m)
            pc_g = jnp.exp2(s_ctx - m)
            denoms.append(jnp.sum(pl_g, axis=-1, keepdims=True) + jnp.sum(pc_g, axis=-1, keepdims=True)
                          + jnp.exp2(sink - m))
            p_loc.append(pl_g.astype(BF16))
            p_ctx.append(pc_g.astype(BF16))
        o = (jnp.dot(jnp.concatenate(p_loc, axis=0), vl[:, kv], preferred_element_type=F32)
             + jnp.dot(jnp.concatenate(p_ctx, axis=0), vc[:, kv], preferred_element_type=F32))
        outs.extend(o[g * BLOCK:(g + 1) * BLOCK] / denoms[g] for g in range(GQA_GROUP))
    o_ref[0] = jnp.concatenate(outs, axis=-1)


def _latent_attention(q, k, v, cache_k, cache_v, layer, sink):
    b, L, _ = q.shape
    past = cache_k.shape[2]
    nb = L // BLOCK
    assert nb >= 2
    blk = lambda n: pl.BlockSpec((1, BLOCK, n), lambda bi, i: (bi, i, 0))
    prev = pl.BlockSpec((1, BLOCK, KV_WIDTH), lambda bi, i: (bi, jnp.maximum(i - 1, 0), 0))
    nxt = pl.BlockSpec((1, BLOCK, KV_WIDTH), lambda bi, i: (bi, jnp.minimum(i + 1, nb - 1), 0))
    cache = pl.BlockSpec((1, 1, past, KV_WIDTH), lambda bi, i: (bi, layer, 0, 0))
    band_spec = pl.BlockSpec((1, BLOCK, 3 * BLOCK),
                             lambda bi, i: (jnp.minimum(i, 1) + jnp.maximum(i - (nb - 2), 0), 0, 0))
    return pl.pallas_call(
        _lat_attn_kernel,
        grid=(b, nb),
        in_specs=[pl.BlockSpec(memory_space=pltpu.SMEM), blk(ATTN_WIDTH), prev, blk(KV_WIDTH), nxt,
                  prev, blk(KV_WIDTH), nxt, cache, cache, band_spec],
        out_specs=blk(ATTN_WIDTH),
        out_shape=jax.ShapeDtypeStruct((b, L, ATTN_WIDTH), F32),
        compiler_params=_params("parallel", "parallel"),
        name="latent_attention",
    )(sink, q, k, k, k, v, v, v, cache_k, cache_v, _band_bias())


def _band_bias():
    r = jnp.arange(BLOCK)[:, None]
    col = jnp.arange(3 * BLOCK)[None, :]
    band = jnp.abs(col - BLOCK - r) <= WINDOW
    first = band & (col >= BLOCK)
    last = band & (col < 2 * BLOCK)
    return jnp.where(jnp.stack([first, band, last]), 0.0, NEG_INF).astype(F32)


def _rope_tables(L):
    t = jnp.arange(L)
    row = (t // GRID_W).astype(F32)
    col = (t % GRID_W).astype(F32)
    nf = HEAD_DIM // 4
    inv = ROPE_BASE ** (-jnp.arange(nf, dtype=F32) / nf)
    ang_r = row[:, None] * inv[None, :]
    ang_c = col[:, None] * inv[None, :]
    cos = jnp.concatenate([jnp.cos(ang_r), jnp.cos(ang_r), jnp.cos(ang_c), jnp.cos(ang_c)], axis=-1)
    sin = jnp.concatenate([-jnp.sin(ang_r), jnp.sin(ang_r), -jnp.sin(ang_c), jnp.sin(ang_c)], axis=-1)
    return jnp.tile(cos, (1, N_Q_HEADS)), jnp.tile(sin, (1, N_Q_HEADS))


def _block_mean():
    i = jnp.arange(LANES) // HEAD_DIM
    return jnp.where(i[:, None] == i[None, :], 1.0 / HEAD_DIM, 0.0).astype(BF16)


SHIFT_K = 2 * LANES

def _ssd_conv_chunk(xbc_ref, xp_ref, xn_ref, cw_ref, cb_ref, shift_ref, has_prev, has_next):
    main = xbc_ref[0]
    rows = jnp.concatenate([jnp.where(has_prev, xp_ref[0], 0.0), main, jnp.where(has_next, xn_ref[0], 0.0),
                            jnp.zeros((SHIFT_K - CHUNK - 2 * HALO, CONV_CH), F32)], axis=0).astype(BF16)
    taps = jnp.dot(shift_ref[...], rows, preferred_element_type=F32)
    acc = cb_ref[...] + main * cw_ref[CONV_W // 2:CONV_W // 2 + 1, :]
    for i, kk in enumerate(k for k in range(CONV_W) if k != CONV_W // 2):
        acc = acc + taps[i * CHUNK:(i + 1) * CHUNK] * cw_ref[kk:kk + 1, :]
    return _silu(acc)


def _ssd_scan_prelude(xs_t, bc, dt_raw, dtt_raw, tri, tri_t, bias, bias_t, a_log, a_log_t, end_row):
    a_neg = -jnp.exp(a_log) * LOG2E
    a_neg_t = -jnp.exp(a_log_t) * LOG2E
    dt = _softplus(dt_raw + bias)
    dtt = _softplus(dtt_raw + bias_t)
    acum = jnp.dot(tri, dt * a_neg, preferred_element_type=F32, precision=HIGHEST)
    acum_t = jnp.dot(dtt * a_neg_t, tri_t, preferred_element_type=F32, precision=HIGHEST)
    last_t = acum_t[:, end_row:end_row + 1]
    return {"bc": bc, "xs_t": xs_t, "mask": tri > 0.0, "acum": acum, "acum_t": acum_t, "dt_t": dtt,
            "chunk_decay_t": jnp.exp2(last_t),
            "decay_in_t": jnp.exp2(acum_t),
            "coef_t": jnp.exp2(last_t - acum_t) * dtt}


def _ssd_scan_chunks(scans):
    zero = jnp.zeros((SSD_HEADDIM, CHUNK), BF16)
    ys = [[] for _ in scans]
    for g in range(SSD_GROUPS):
        hs = slice(g * HEADS_PER_GROUP * SSD_HEADDIM, (g + 1) * HEADS_PER_GROUP * SSD_HEADDIM)
        heads = range(g * HEADS_PER_GROUP, (g + 1) * HEADS_PER_GROUP)
        stage = []
        for t, s_ref in scans:
            bm = t["bc"][:, g * D_STATE:(g + 1) * D_STATE].astype(BF16)
            cm = t["bc"][:, (SSD_GROUPS + g) * D_STATE:(SSD_GROUPS + g + 1) * D_STATE].astype(BF16)
            cbm = jnp.where(t["mask"], _bdot_nt(cm, bm), 0.0)
            state = s_ref[hs, :]
            y_off = _bdot_nt(state, cm)
            y_off = jnp.concatenate(
                [y_off[(h - heads[0]) * SSD_HEADDIM:(h - heads[0] + 1) * SSD_HEADDIM] * t["decay_in_t"][h:h + 1, :]
                 for h in heads], axis=0)
            stage.append((bm, cbm, state, y_off, []))
        for pair in range(HEADS_PER_GROUP // 2):
            for (t, _), (_, cbm, _, _, y_diag) in zip(scans, stage):
                ws, xs = [], []
                for h in (heads[0] + 2 * pair, heads[0] + 2 * pair + 1):
                    a_col = jnp.broadcast_to(t["acum"][:, h:h + 1], (CHUNK, CHUNK))
                    ws.append((cbm * jnp.exp2(jnp.minimum(a_col - t["acum_t"][h:h + 1, :], 0.0))).astype(BF16))
                    xs.append((t["xs_t"][h * SSD_HEADDIM:(h + 1) * SSD_HEADDIM, :] * t["dt_t"][h:h + 1, :]).astype(BF16))
                lhs = jnp.concatenate([jnp.concatenate([xs[0], zero], axis=1),
                                       jnp.concatenate([zero, xs[1]], axis=1)], axis=0)
                y_diag.append(_bdot_nt(lhs, jnp.concatenate(ws, axis=1)))
        for i, ((t, s_ref), (bm, _, state, y_off, y_diag)) in enumerate(zip(scans, stage)):
            ys[i].append(jnp.concatenate(y_diag, axis=0) + y_off)
            x_state = jnp.concatenate(
                [(t["xs_t"][h * SSD_HEADDIM:(h + 1) * SSD_HEADDIM, :] * t["coef_t"][h:h + 1, :]).astype(BF16)
                 for h in heads], axis=0)
            update = jnp.dot(x_state, bm, preferred_element_type=F32)
            for h in heads:
                hp = slice(h * SSD_HEADDIM, (h + 1) * SSD_HEADDIM)
                rel = slice((h - heads[0]) * SSD_HEADDIM, (h - heads[0] + 1) * SSD_HEADDIM)
                s_ref[hp, :] = state[rel] * t["chunk_decay_t"][h:h + 1, :] + update[rel]
    return [jnp.concatenate(y, axis=0) for y in ys]


def _ssd_kernel(xf_ref, xfp_ref, xfn_ref, xb_ref, xbp_ref, xbn_ref, zf_ref, zb_ref, dtf_ref, dtb_ref,
                dttf_ref, dttb_ref, init_ref, tri_ref, shift_ref, cw_ref, cb_ref, bias_ref, biast_ref,
                alog_ref, alogt_ref, dskip_ref, ng_ref, o_ref, fin_ref, xst_ref, bc_ref, sf_ref, sb_ref,
                *, nc, has_init):
    s = pl.program_id(1)
    half = nc // 2
    first_half = s < half
    cf = s
    cb = nc - 1 - s
    rf = pl.multiple_of(cf * CHUNK, CHUNK)
    rb = pl.multiple_of(cb * CHUNK, CHUNK)

    @pl.when(first_half)
    def _():
        for c, rows, refs in ((cf, rf, (xf_ref, xfp_ref, xfn_ref)), (cb, rb, (xb_ref, xbp_ref, xbn_ref))):
            xc = _ssd_conv_chunk(*refs, cw_ref, cb_ref, shift_ref, c > 0, c < nc - 1)
            xst_ref[c] = xc[:, 0:SSD_INNER].T
            bc_ref[pl.ds(rows, CHUNK), :] = xc[:, SSD_INNER:]

    @pl.when(s == 0)
    def _():
        if has_init:
            sf_ref[...] = init_ref[0, 0, 0]
            sb_ref[...] = init_ref[0, 0, 1]
        else:
            sf_ref[...] = jnp.zeros(sf_ref.shape, F32)
            sb_ref[...] = jnp.zeros(sb_ref.shape, F32)

    lower = tri_ref[0]
    upper = tri_ref[1]
    tf = _ssd_scan_prelude(xst_ref[cf], bc_ref[pl.ds(rf, CHUNK), :], dtf_ref[0], dttf_ref[0, 0], lower, upper,
                           bias_ref[0], biast_ref[0], alog_ref[0], alogt_ref[0], CHUNK - 1)
    tb = _ssd_scan_prelude(xst_ref[cb], bc_ref[pl.ds(rb, CHUNK), :], dtb_ref[0], dttb_ref[0, 0], upper, lower,
                           bias_ref[1], biast_ref[1], alog_ref[1], alogt_ref[1], 0)
    yf_t, yb_t = _ssd_scan_chunks([(tf, sf_ref), (tb, sb_ref)])
    yf = (yf_t + dskip_ref[...] * tf["xs_t"]).T
    yb = yb_t.T

    @pl.when(first_half)
    def _():
        o_ref[0, pl.ds(rf, CHUNK), :] = yf
        o_ref[0, pl.ds(rb, CHUNK), :] = yb

    def finish(rows, y, z):
        tot = (o_ref[0, pl.ds(rows, CHUNK), :] + y) * _silu(z)
        ms = jnp.mean(tot * tot, axis=-1, keepdims=True)
        o_ref[0, pl.ds(rows, CHUNK), :] = tot * lax.rsqrt(ms + EPS) * ng_ref[...]

    @pl.when(jnp.logical_not(first_half))
    def _():
        finish(rf, yf, zf_ref[0])
        finish(rb, yb, zb_ref[0])

    @pl.when(s == nc - 1)
    def _():
        fin_ref[0, 0] = sf_ref[...]
        fin_ref[0, 1] = sb_ref[...]


def _ssd_mixer(z, xbc, dt_raw, init_state, layer, consts, p):
    b, L, _ = z.shape
    nc = L // CHUNK
    has_init = init_state is not None
    dt_t = jnp.stack([dt_raw[:, :, 0:SSD_HEADS], dt_raw[:, :, LANES:LANES + SSD_HEADS]], axis=1)
    dt_t = jnp.swapaxes(dt_t, 2, 3)
    assert nc % 2 == 0
    half = nc // 2
    if not has_init:
        init_state = jnp.zeros((1, 1, 1, HALO, LANES), F32)
        init_spec = pl.BlockSpec((1, 1, 1, HALO, LANES), lambda bi, s: (0, 0, 0, 0, 0))
    else:
        init_spec = pl.BlockSpec((1, 1, 2, SSD_INNER, D_STATE), lambda bi, s: (bi, layer, 0, 0, 0))
    per = CHUNK // HALO
    last = L // HALO - 1
    conv_f = lambda s: jnp.minimum(s, half - 1)
    conv_b = lambda s: jnp.maximum(nc - 1 - s, half)
    fin_f = lambda s: jnp.maximum(s, half)
    fin_b = lambda s: jnp.minimum(nc - 1 - s, half - 1)

    def conv_specs(chunk):
        return [pl.BlockSpec((1, CHUNK, CONV_CH), lambda bi, s: (bi, chunk(s), 0)),
                pl.BlockSpec((1, HALO, CONV_CH), lambda bi, s: (bi, jnp.maximum(chunk(s) * per - 1, 0), 0)),
                pl.BlockSpec((1, HALO, CONV_CH), lambda bi, s: (bi, jnp.minimum((chunk(s) + 1) * per, last), 0))]

    in_specs = conv_specs(conv_f) + conv_specs(conv_b) + [
        pl.BlockSpec((1, CHUNK, SSD_INNER), lambda bi, s: (bi, fin_f(s), 0)),
        pl.BlockSpec((1, CHUNK, SSD_INNER), lambda bi, s: (bi, fin_b(s), 0)),
        pl.BlockSpec((1, CHUNK, LANES), lambda bi, s: (bi, s, 0)),
        pl.BlockSpec((1, CHUNK, LANES), lambda bi, s: (bi, nc - 1 - s, 1)),
        pl.BlockSpec((1, 1, SSD_HEADS, CHUNK), lambda bi, s: (bi, 0, 0, s)),
        pl.BlockSpec((1, 1, SSD_HEADS, CHUNK), lambda bi, s: (bi, 1, 0, nc - 1 - s)),
        init_spec,
        _const_spec((2, CHUNK, CHUNK)), _const_spec(((CONV_W - 1) * CHUNK, SHIFT_K)),
        _const_spec((CONV_W, CONV_CH)), _const_spec((1, CONV_CH)),
        _const_spec((2, 1, LANES)), _const_spec((2, SSD_HEADS, 1)), _const_spec((2, 1, LANES)),
        _const_spec((2, SSD_HEADS, 1)), _const_spec((SSD_INNER, CHUNK)), _const_spec((1, SSD_INNER)),
    ]
    out, fin = pl.pallas_call(
        functools.partial(_ssd_kernel, nc=nc, has_init=has_init),
        grid=(b, nc),
        in_specs=in_specs,
        out_specs=[pl.BlockSpec((1, L, SSD_INNER), lambda bi, s: (bi, 0, 0)),
                   pl.BlockSpec((1, 2, SSD_INNER, D_STATE), lambda bi, s: (bi, 0, 0, 0))],
        out_shape=[jax.ShapeDtypeStruct((b, L, SSD_INNER), F32),
                   jax.ShapeDtypeStruct((b, 2, SSD_INNER, D_STATE), F32)],
        scratch_shapes=[pltpu.VMEM((nc, SSD_INNER, CHUNK), F32), pltpu.VMEM((L, CONV_CH - SSD_INNER), F32),
                        pltpu.VMEM((SSD_INNER, D_STATE), F32), pltpu.VMEM((SSD_INNER, D_STATE), F32)],
        compiler_params=_params("parallel", "arbitrary"),
        name="ssd_mixer",
    )(xbc, xbc, xbc, xbc, xbc, xbc, z, z, dt_raw, dt_raw, dt_t, dt_t, init_state, consts["tri"], consts["shift"],
      p["conv_w"], p["conv_b"], p["dt_bias"], p["dt_bias_t"], p["a_log"], p["a_log_t"], p["d_skip"], p["ssd_norm_g"])
    return out, fin


def _ssd_consts():
    i = jnp.arange(CHUNK)
    lower = (i[None, :] <= i[:, None]).astype(F32)
    t = jnp.arange(CHUNK)[:, None]
    col = jnp.arange(SHIFT_K)[None, :]
    shift = jnp.concatenate([(col == HALO + t + kk - CONV_W // 2) for kk in range(CONV_W) if kk != CONV_W // 2], axis=0)
    return {"tri": jnp.stack([lower, lower.T]), "shift": shift.astype(BF16)}


FF_CHUNK = 1024


def _merge_mlp_kernel(x_ref, mod_ref, pool_ref, attn_ref, ssd_ref, gate_ref, g_ref,
                      wp_ref, wa_ref, ws_ref, wo_ref, w1_ref, w2_ref, o_ref):
    mod = mod_ref[0]
    gates = _sigmoid(gate_ref[0])
    merged = (gates[:, 0:D_MODEL] * _bdot(pool_ref[0], wp_ref[...])
              + gates[:, D_MODEL:2 * D_MODEL] * _bdot(attn_ref[0], wa_ref[...])
              + gates[:, 2 * D_MODEL:] * _bdot(ssd_ref[0], ws_ref[...]))
    x = x_ref[0] + mod[:, 2 * D_MODEL:3 * D_MODEL] * _bdot(merged, wo_ref[...])
    h = _rms_mod(x, g_ref[...], mod[:, 4 * D_MODEL:5 * D_MODEL], mod[:, 3 * D_MODEL:4 * D_MODEL]).astype(BF16)
    acc = jnp.zeros(x.shape, F32)
    for j in range(D_FF // FF_CHUNK):
        ff = slice(j * FF_CHUNK, (j + 1) * FF_CHUNK)
        a = jnp.dot(h, w1_ref[:, ff], preferred_element_type=F32)
        acc = acc + _bdot(jnp.square(jnp.maximum(a, 0.0)), w2_ref[ff, :])
    o_ref[0] = x + mod[:, 5 * D_MODEL:] * acc


def _merge_mlp(x, mods, layer, per_batch, pool, attn, ssd, gate, norm_g, w):
    b, L, _ = x.shape
    tm = TOKEN_TILE
    tok = lambda n: pl.BlockSpec((1, tm, n), lambda bi, i: (bi, i, 0))
    weights = [w["pool_o"], w["attn_o"], w["ssd_o"], w["out"], w["mlp1"], w["mlp2"]]
    return pl.pallas_call(
        _merge_mlp_kernel,
        grid=(b, L // tm),
        in_specs=[tok(D_MODEL), _mod_spec(layer, per_batch), tok(POOL_WIDTH), tok(ATTN_WIDTH), tok(SSD_INNER),
                  tok(N_BRANCH * D_MODEL), _const_spec((1, D_MODEL))] + [_const_spec(a.shape) for a in weights],
        out_specs=tok(D_MODEL),
        out_shape=jax.ShapeDtypeStruct((b, L, D_MODEL), F32),
        compiler_params=_params("parallel", "parallel"),
        name="merge_mlp",
    )(x, mods, pool, attn, ssd, gate, norm_g, *weights)


def _layer(x, mods, layer, per_batch, w, p, consts, ctx):
    u, q, k, v, z, xbc, dt_raw, gate = _in_projection(
        x, mods, layer, per_batch, p["norm1_g"], p["q_norm_g"], p["k_norm_g"], consts["bm"],
        None if ctx is None else consts["rope"], w)
    pool = _pool_mixer(u, w["pool_w"], p["pool_scale"])
    if ctx is None:
        attn = _context_attention(q, k, v, p["attn_sink"])
        init = None
    else:
        cache_k, cache_v, init = ctx
        attn = _latent_attention(q, k, v, cache_k, cache_v, layer, p["attn_sink"])
    ssd, fin = _ssd_mixer(z, xbc, dt_raw, init, layer, consts, p)
    x = _merge_mlp(x, mods, layer, per_batch, pool, attn, ssd, gate, p["norm2_g"], w)
    return x, (k, v, fin)


def _layer_weights(l, w_in, pool_w, w_pool_o, w_attn_o, w_ssd_o, w_out, w_mlp1, w_mlp2):
    wi = w_in[l]
    o_z = POOL_WIDTH + ATTN_WIDTH + 2 * KV_WIDTH
    o_x = o_z + SSD_INNER
    o_d = o_x + CONV_CH
    o_g = o_d + 2 * SSD_HEADS
    wd = wi[:, o_d:o_g]
    pad = jnp.zeros((D_MODEL, LANES - SSD_HEADS), F32)
    wd = jnp.concatenate([wd[:, :SSD_HEADS], pad, wd[:, SSD_HEADS:], pad], axis=1)
    cast = lambda a: a.astype(BF16)
    return {"a": cast(wi[:, :o_z]), "z": cast(wi[:, o_z:o_x]), "xbc": cast(wi[:, o_x:o_d]), "dt": cast(wd),
            "gate": cast(wi[:, o_g:]), "pool_w": cast(pool_w[l]), "pool_o": cast(w_pool_o[l]),
            "attn_o": cast(w_attn_o[l]), "ssd_o": cast(w_ssd_o[l]), "out": cast(w_out[l]),
            "mlp1": cast(w_mlp1[l]), "mlp2": cast(w_mlp2[l])}


def _pad_lanes(v):
    return jnp.pad(v, ((0, 0), (0, LANES - SSD_HEADS))).reshape(2, 1, LANES)


def kernel(x_prompt, x_sample, cache_k, cache_v, state_ssd, c, c_ctx, w_mod, b_mod, norm1_g, norm2_g, w_in, pool_w, pool_scale, w_pool_o, q_norm_g, k_norm_g, attn_sink, w_attn_o, conv_w, conv_b, dt_bias, a_log, d_skip, ssd_norm_g, w_ssd_o, w_out, w_mlp1, w_mlp2):
    batch, seq, _ = x_prompt.shape
    dec_batch, dec_seq, _ = x_sample.shape
    past = cache_k.shape[2]
    assert 1 + dec_batch <= MOD_ROWS

    cvecs = jnp.concatenate([c_ctx[None, :], c, jnp.zeros((MOD_ROWS - 1 - dec_batch, D_MODEL), F32)], axis=0)
    mods = _modulation(cvecs, w_mod, b_mod)

    consts = {"bm": _block_mean(), "rope": _rope_tables(dec_seq)}
    consts.update(_ssd_consts())
    cache_k = cache_k.reshape(dec_batch, DEPTH, past, KV_WIDTH).astype(BF16)
    cache_v = cache_v.reshape(dec_batch, DEPTH, past, KV_WIDTH).astype(BF16)
    state = state_ssd.reshape(dec_batch, DEPTH, 2, SSD_INNER, D_STATE)

    y_prompt, y_sample = x_prompt, x_sample
    ks, vs, ss = [], [], []
    for l in range(DEPTH):
        w = _layer_weights(l, w_in, pool_w, w_pool_o, w_attn_o, w_ssd_o, w_out, w_mlp1, w_mlp2)
        p = {"norm1_g": norm1_g[l][None, :], "norm2_g": norm2_g[l][None, :], "pool_scale": pool_scale[l][None, :],
             "q_norm_g": jnp.tile(q_norm_g[l], N_Q_HEADS)[None, :], "k_norm_g": jnp.tile(k_norm_g[l], N_KV_HEADS)[None, :],
             "attn_sink": attn_sink[l], "conv_w": conv_w[l], "conv_b": conv_b[l][None, :],
             "dt_bias": _pad_lanes(dt_bias[l]), "dt_bias_t": dt_bias[l][:, :, None],
             "a_log": _pad_lanes(a_log[l]), "a_log_t": a_log[l][:, :, None],
             "d_skip": jnp.broadcast_to(jnp.repeat(d_skip[l], SSD_HEADDIM)[:, None], (SSD_INNER, CHUNK)),
             "ssd_norm_g": ssd_norm_g[l][None, :]}
        y_prompt, (k_l, v_l, s_l) = _layer(y_prompt, mods, l, False, w, p, consts, None)
        ks.append(k_l.reshape(batch, seq, N_KV_HEADS, HEAD_DIM))
        vs.append(v_l.reshape(batch, seq, N_KV_HEADS, HEAD_DIM))
        ss.append(s_l.reshape(batch, 2, SSD_HEADS, SSD_HEADDIM, D_STATE))
        y_sample, _ = _layer(y_sample, mods, l, True, w, p, consts, (cache_k, cache_v, state))
    return (y_prompt, y_sample, jnp.stack(ks, axis=1), jnp.stack(vs, axis=1), jnp.stack(ss, axis=1))
```

```python
import functools
import math

import jax
import jax.numpy as jnp
from jax import lax
from jax.experimental import pallas as pl
from jax.experimental.pallas import tpu as pltpu

F32 = jnp.float32
BF16 = jnp.bfloat16
HIGHEST = lax.Precision.HIGHEST

D_MODEL = 1024
DEPTH = 2
GRID_W = 64
MOD_CHUNKS = 6
EPS = 1e-6
POOL_WIDTH = D_MODEL // 2
POOL_GROUPS = 4
POOL_GROUP_W = POOL_WIDTH // POOL_GROUPS
POOL_WINDOWS = (2, 4, 8, 16)
N_Q_HEADS = 8
N_KV_HEADS = 2
GQA_GROUP = N_Q_HEADS // N_KV_HEADS
HEAD_DIM = 64
ATTN_WIDTH = N_Q_HEADS * HEAD_DIM
KV_WIDTH = N_KV_HEADS * HEAD_DIM
WINDOW = 128
BLOCK = 128
ROPE_BASE = 10000.0
ATTN_SCALE = HEAD_DIM ** -0.5
SSD_HEADS = 16
SSD_HEADDIM = 64
SSD_INNER = SSD_HEADS * SSD_HEADDIM
SSD_GROUPS = 2
HEADS_PER_GROUP = SSD_HEADS // SSD_GROUPS
D_STATE = 128
CONV_W = 5
CHUNK = 128
CONV_CH = SSD_INNER + 2 * SSD_GROUPS * D_STATE
N_BRANCH = 3
D_FF = 4 * D_MODEL

LANES = 128
HALO = 8
MOD_ROWS = 16
VMEM_LIMIT = 56 * 1024 * 1024
TOKEN_TILE = 256
NEG_INF = float("-inf")
LOG2E = math.log2(math.e)
SCORE_SCALE = ATTN_SCALE * LOG2E


def _params(*sem):
    return pltpu.CompilerParams(dimension_semantics=sem, vmem_limit_bytes=VMEM_LIMIT)


def _const_spec(shape):
    nd = len(shape)
    return pl.BlockSpec(shape, lambda *_: (0,) * nd, pipeline_mode=pl.Buffered(1))


def _bdot(a, b):
    return jnp.dot(a.astype(BF16), b.astype(BF16), preferred_element_type=F32)


def _bdot_nt(a, b):
    return lax.dot_general(a.astype(BF16), b.astype(BF16), (((1,), (1,)), ((), ())),
                           preferred_element_type=F32)


def _sigmoid(x):
    return 1.0 / (1.0 + jnp.exp2(x * -LOG2E))


def _silu(x):
    return x * _sigmoid(x)


def _softplus(x):
    return jnp.maximum(x, 0.0) + jnp.log1p(jnp.exp(-jnp.abs(x)))


def _rms_mod(x, g, scale, shift):
    ms = jnp.mean(x * x, axis=-1, keepdims=True)
    return (x * lax.rsqrt(ms + EPS)) * g * (1.0 + scale) + shift


def _mod_kernel(c_ref, w_ref, b_ref, o_ref):
    o_ref[0] = _bdot(_silu(c_ref[...]), w_ref[0]) + b_ref[0]


def _modulation(cvecs, w_mod, b_mod):
    n = MOD_CHUNKS * D_MODEL
    tn = n // 4
    out = pl.pallas_call(
        _mod_kernel,
        grid=(DEPTH, n // tn),
        in_specs=[pl.BlockSpec((MOD_ROWS, D_MODEL), lambda l, j: (0, 0)),
                  pl.BlockSpec((1, D_MODEL, tn), lambda l, j: (l, 0, j)),
                  pl.BlockSpec((1, 1, tn), lambda l, j: (l, 0, j))],
        out_specs=pl.BlockSpec((1, MOD_ROWS, tn), lambda l, j: (l, 0, j)),
        out_shape=jax.ShapeDtypeStruct((DEPTH, MOD_ROWS, n), F32),
        compiler_params=_params("arbitrary", "arbitrary"),
        name="modulation",
    )(cvecs, w_mod, b_mod.reshape(DEPTH, 1, n))
    return out.reshape(DEPTH * MOD_ROWS, 1, n)


def _mod_spec(layer, per_batch):
    base = layer * MOD_ROWS
    if per_batch:
        return pl.BlockSpec((1, 1, MOD_CHUNKS * D_MODEL), lambda b, i: (base + 1 + b, 0, 0))
    return pl.BlockSpec((1, 1, MOD_CHUNKS * D_MODEL), lambda b, i: (base, 0, 0))


DT_PAD = 2 * LANES


def _inproj_kernel(*refs, rope):
    x_ref, mod_ref, g_ref, qg_ref, kg_ref, bm_ref = refs[:6]
    refs = refs[6:]
    if rope:
        cos_ref, sin_ref = refs[:2]
        refs = refs[2:]
    wa_ref, wz_ref, wx_ref, wd_ref, wg_ref, u_ref, q_ref, k_ref, v_ref, z_ref, xbc_ref, dt_ref, gate_ref = refs
    mod = mod_ref[0]
    h = _rms_mod(x_ref[0], g_ref[...], mod[:, D_MODEL:2 * D_MODEL], mod[:, 0:D_MODEL]).astype(BF16)
    a = jnp.dot(h, wa_ref[...], preferred_element_type=F32)
    u_ref[0] = a[:, 0:POOL_WIDTH]
    v_ref[0] = a[:, POOL_WIDTH + ATTN_WIDTH + KV_WIDTH:].astype(v_ref.dtype)
    z_ref[0] = jnp.dot(h, wz_ref[...], preferred_element_type=F32)
    qn = _head_rms(a[:, POOL_WIDTH:POOL_WIDTH + ATTN_WIDTH], bm_ref[...], qg_ref[...])
    kn = _head_rms(a[:, POOL_WIDTH + ATTN_WIDTH:POOL_WIDTH + ATTN_WIDTH + KV_WIDTH], bm_ref[...], kg_ref[...])
    xbc_ref[0] = jnp.dot(h, wx_ref[...], preferred_element_type=F32)
    if rope:
        qn = _rope(qn, cos_ref[...], sin_ref[...])
        kn = _rope(kn, cos_ref[:, 0:KV_WIDTH], sin_ref[:, 0:KV_WIDTH])
    q_ref[0] = (qn * SCORE_SCALE).astype(q_ref.dtype)
    k_ref[0] = kn.astype(k_ref.dtype)
    dt_ref[0] = jnp.dot(h, wd_ref[...], preferred_element_type=F32)
    gate_ref[0] = jnp.dot(h, wg_ref[...], preferred_element_type=F32)


def _in_projection(x, mods, layer, per_batch, norm_g, qg, kg, bm, rope_tabs, w):
    b, L, _ = x.shape
    tm = min(2 * TOKEN_TILE, L)
    rope = rope_tabs is not None
    kv_dtype = BF16 if rope else F32
    outs = ((POOL_WIDTH, F32), (ATTN_WIDTH, BF16), (KV_WIDTH, kv_dtype), (KV_WIDTH, kv_dtype), (SSD_INNER, F32),
            (CONV_CH, F32), (DT_PAD, F32), (N_BRANCH * D_MODEL, F32))
    tok = lambda n: pl.BlockSpec((1, tm, n), lambda bi, i: (bi, i, 0))
    tab = pl.BlockSpec((tm, ATTN_WIDTH), lambda bi, i: (i, 0))
    weights = [w["a"], w["z"], w["xbc"], w["dt"], w["gate"]]
    return pl.pallas_call(
        functools.partial(_inproj_kernel, rope=rope),
        grid=(b, L // tm),
        in_specs=[tok(D_MODEL), _mod_spec(layer, per_batch), _const_spec((1, D_MODEL)), _const_spec(qg.shape),
                  _const_spec(kg.shape), _const_spec(bm.shape)] + ([tab, tab] if rope else [])
                 + [_const_spec(a.shape) for a in weights],
        out_specs=[tok(n) for n, _ in outs],
        out_shape=[jax.ShapeDtypeStruct((b, L, n), dt) for n, dt in outs],
        compiler_params=_params("parallel", "parallel"),
        name="in_projection",
    )(x, mods, norm_g, qg, kg, bm, *(rope_tabs if rope else ()), *weights)


POOL_ROWS = 256
POOL_K = POOL_ROWS + LANES


def _pool_kernel(u_ref, band_ref, pw_ref, ps_ref, o_ref, *, L):
    halo_zeros = jnp.zeros((HALO, POOL_WIDTH), F32)
    tail_zeros = jnp.zeros((POOL_K - POOL_ROWS - 2 * HALO, POOL_WIDTH), F32)
    groups = [slice(gi * POOL_GROUP_W, (gi + 1) * POOL_GROUP_W) for gi in range(POOL_GROUPS)]
    for r0 in range(0, L, POOL_ROWS):
        main = u_ref[0, r0:r0 + POOL_ROWS, :]
        prev = u_ref[0, r0 - HALO:r0, :] if r0 > 0 else halo_zeros
        nxt = u_ref[0, r0 + POOL_ROWS:r0 + POOL_ROWS + HALO, :] if r0 + POOL_ROWS < L else halo_zeros
        rows = jnp.concatenate([prev, main, nxt, tail_zeros], axis=0)
        hi = rows.astype(BF16)
        lo = (rows - hi.astype(F32)).astype(BF16)
        t = r0 + lax.broadcasted_iota(jnp.int32, (POOL_ROWS, POOL_GROUP_W), 0)
        sums = [jnp.dot(band_ref[gi], hi[:, cols], preferred_element_type=F32)
                + jnp.dot(band_ref[gi], lo[:, cols], preferred_element_type=F32) for gi, cols in enumerate(groups)]
        pooled = []
        for w, cols, acc in zip(POOL_WINDOWS, groups, sums):
            cnt = (jnp.minimum(t + w // 2, L) - jnp.maximum(t - w // 2, 0)).astype(F32)
            pooled.append((acc / cnt - main[:, cols]).astype(BF16))
        mixed = [jnp.dot(pooled[gi], pw_ref[gi], preferred_element_type=F32) for gi in range(POOL_GROUPS)]
        for cols, m in zip(groups, mixed):
            o_ref[0, r0:r0 + POOL_ROWS, cols] = m * ps_ref[:, cols]


def _pool_bands():
    d = jnp.arange(POOL_K)[None, :] - HALO - jnp.arange(POOL_ROWS)[:, None]
    return jnp.stack([(d >= -(w // 2)) & (d < w // 2) for w in POOL_WINDOWS]).astype(BF16)


def _pool_mixer(u, pool_w, pool_scale):
    b, L, _ = u.shape
    assert max(POOL_WINDOWS) // 2 <= HALO and L % POOL_ROWS == 0
    seq = pl.BlockSpec((1, L, POOL_WIDTH), lambda bi: (bi, 0, 0))
    bands = _pool_bands()
    return pl.pallas_call(
        functools.partial(_pool_kernel, L=L),
        grid=(b,),
        in_specs=[seq, _const_spec(bands.shape), _const_spec(pool_w.shape), _const_spec(pool_scale.shape)],
        out_specs=seq,
        out_shape=jax.ShapeDtypeStruct((b, L, POOL_WIDTH), F32),
        compiler_params=_params("parallel"),
        name="pool_mixer",
    )(u, bands, pool_w, pool_scale)


def _head_rms(x, block_mean, g):
    sq = x * x
    hi = sq.astype(BF16)
    lo = (sq - hi.astype(F32)).astype(BF16)
    ms = jnp.concatenate(
        [jnp.dot(hi[:, c:c + LANES], block_mean, preferred_element_type=F32)
         + jnp.dot(lo[:, c:c + LANES], block_mean, preferred_element_type=F32)
         for c in range(0, x.shape[-1], LANES)], axis=-1)
    return x * lax.rsqrt(ms + EPS) * g


def _rope(x, cos, sin_signed):
    width = x.shape[-1]
    lane = lax.broadcasted_iota(jnp.int32, x.shape, 1)
    partner = jnp.where((lane & 16) == 0, pltpu.roll(x, width - 16, 1), pltpu.roll(x, 16, 1))
    return x * cos + partner * sin_signed


def _gqa_attention(q, keys, vals, sink_ref, rows, bias):
    kvs = [slice(j * HEAD_DIM, (j + 1) * HEAD_DIM) for j in range(N_KV_HEADS)]
    scores = []
    for j in range(N_KV_HEADS):
        qs = jnp.concatenate([q[:, (j * GQA_GROUP + g) * HEAD_DIM:(j * GQA_GROUP + g + 1) * HEAD_DIM]
                              for g in range(GQA_GROUP)], axis=0)
        scores.append(_bdot_nt(qs, keys[:, kvs[j]]))
    probs, denoms = [], []
    for j in range(N_KV_HEADS):
        p_rows = []
        for g in range(GQA_GROUP):
            s = scores[j][g * rows:(g + 1) * rows]
            if bias is not None:
                s = s + bias
            sink = sink_ref[j * GQA_GROUP + g] * LOG2E
            m = jnp.maximum(jnp.max(s, axis=-1, keepdims=True), sink)
            p = jnp.exp2(s - m)
            denoms.append(jnp.sum(p, axis=-1, keepdims=True) + jnp.exp2(sink - m))
            p_rows.append(p.astype(BF16))
        probs.append(jnp.concatenate(p_rows, axis=0))
    outs = []
    for j in range(N_KV_HEADS):
        o = jnp.dot(probs[j], vals[:, kvs[j]], preferred_element_type=F32)
        outs.extend(o[g * rows:(g + 1) * rows] / denoms[j * GQA_GROUP + g] for g in range(GQA_GROUP))
    return jnp.concatenate(outs, axis=-1)


def _ctx_attn_kernel(sink_ref, q_ref, k_ref, v_ref, o_ref):
    q = q_ref[0]
    o_ref[0] = _gqa_attention(q, k_ref[0].astype(BF16), v_ref[0].astype(BF16), sink_ref, q.shape[0], None)


def _context_attention(q, k, v, sink):
    b, L, _ = q.shape
    seq = lambda n: pl.BlockSpec((1, L, n), lambda bi: (bi, 0, 0))
    return pl.pallas_call(
        _ctx_attn_kernel,
        grid=(b,),
        in_specs=[pl.BlockSpec(memory_space=pltpu.SMEM), seq(ATTN_WIDTH), seq(KV_WIDTH), seq(KV_WIDTH)],
        out_specs=seq(ATTN_WIDTH),
        out_shape=jax.ShapeDtypeStruct((b, L, ATTN_WIDTH), F32),
        compiler_params=_params("parallel"),
        name="context_attention",
    )(sink, q, k, v)


def _lat_attn_kernel(sink_ref, q_ref, kp_ref, k0_ref, kn_ref, vp_ref, v0_ref, vn_ref, kc_ref, vc_ref, band_ref, o_ref):
    keys = jnp.concatenate([kp_ref[0], k0_ref[0], kn_ref[0], kc_ref[0, 0]], axis=0)
    vals = jnp.concatenate([vp_ref[0], v0_ref[0], vn_ref[0], vc_ref[0, 0]], axis=0)
    o_ref[0] = _gqa_attention(q_ref[0], keys, vals, sink_ref, BLOCK, band_ref[0])


def _latent_attention(q, k, v, cache_k, cache_v, layer, sink):
    b, L, _ = q.shape
    past = cache_k.shape[2]
    nb = L // BLOCK
    assert nb >= 2
    blk = lambda n: pl.BlockSpec((1, BLOCK, n), lambda bi, i: (bi, i, 0))
    prev = pl.BlockSpec((1, BLOCK, KV_WIDTH), lambda bi, i: (bi, jnp.maximum(i - 1, 0), 0))
    nxt = pl.BlockSpec((1, BLOCK, KV_WIDTH), lambda bi, i: (bi, jnp.minimum(i + 1, nb - 1), 0))
    cache = pl.BlockSpec((1, 1, past, KV_WIDTH), lambda bi, i: (bi, layer, 0, 0))
    band_spec = pl.BlockSpec((1, BLOCK, 3 * BLOCK + past),
                             lambda bi, i: (jnp.minimum(i, 1) + jnp.maximum(i - (nb - 2), 0), 0, 0))
    return pl.pallas_call(
        _lat_attn_kernel,
        grid=(b, nb),
        in_specs=[pl.BlockSpec(memory_space=pltpu.SMEM), blk(ATTN_WIDTH), prev, blk(KV_WIDTH), nxt,
                  prev, blk(KV_WIDTH), nxt, cache, cache, band_spec],
        out_specs=blk(ATTN_WIDTH),
        out_shape=jax.ShapeDtypeStruct((b, L, ATTN_WIDTH), F32),
        compiler_params=_params("parallel", "parallel"),
        name="latent_attention",
    )(sink, q, k, k, k, v, v, v, cache_k, cache_v, _band_bias(past))


def _band_bias(past):
    r = jnp.arange(BLOCK)[:, None]
    col = jnp.arange(3 * BLOCK + past)[None, :]
    band = (jnp.abs(col - BLOCK - r) <= WINDOW) | (col >= 3 * BLOCK)
    first = band & (col >= BLOCK)
    last = band & ((col < 2 * BLOCK) | (col >= 3 * BLOCK))
    return jnp.where(jnp.stack([first, band, last]), 0.0, NEG_INF).astype(F32)


def _rope_tables(L):
    t = jnp.arange(L)
    row = (t // GRID_W).astype(F32)
    col = (t % GRID_W).astype(F32)
    nf = HEAD_DIM // 4
    inv = ROPE_BASE ** (-jnp.arange(nf, dtype=F32) / nf)
    ang_r = row[:, None] * inv[None, :]
    ang_c = col[:, None] * inv[None, :]
    cos = jnp.concatenate([jnp.cos(ang_r), jnp.cos(ang_r), jnp.cos(ang_c), jnp.cos(ang_c)], axis=-1)
    sin = jnp.concatenate([-jnp.sin(ang_r), jnp.sin(ang_r), -jnp.sin(ang_c), jnp.sin(ang_c)], axis=-1)
    return jnp.tile(cos, (1, N_Q_HEADS)), jnp.tile(sin, (1, N_Q_HEADS))


def _block_mean():
    i = jnp.arange(LANES) // HEAD_DIM
    return jnp.where(i[:, None] == i[None, :], 1.0 / HEAD_DIM, 0.0).astype(BF16)


SHIFT_K = 2 * LANES

def _ssd_conv_chunk(xbc_ref, xp_ref, xn_ref, cw_ref, cb_ref, shift_ref, has_prev, has_next):
    main = xbc_ref[0]
    rows = jnp.concatenate([jnp.where(has_prev, xp_ref[0], 0.0), main, jnp.where(has_next, xn_ref[0], 0.0),
                            jnp.zeros((SHIFT_K - CHUNK - 2 * HALO, CONV_CH), F32)], axis=0).astype(BF16)
    taps = jnp.dot(shift_ref[...], rows, preferred_element_type=F32)
    acc = cb_ref[...] + main * cw_ref[CONV_W // 2:CONV_W // 2 + 1, :]
    for i, kk in enumerate(k for k in range(CONV_W) if k != CONV_W // 2):
        acc = acc + taps[i * CHUNK:(i + 1) * CHUNK] * cw_ref[kk:kk + 1, :]
    return _silu(acc)


def _ssd_scan_prelude(xs_t, bc, dt_raw, dtt_raw, tri, tri_t, bias, bias_t, a_log, a_log_t, end_row):
    a_neg = -jnp.exp(a_log) * LOG2E
    a_neg_t = -jnp.exp(a_log_t) * LOG2E
    dt = _softplus(dt_raw + bias)
    dtt = _softplus(dtt_raw + bias_t)
    acum = jnp.dot(tri, dt * a_neg, preferred_element_type=F32, precision=HIGHEST)
    acum_t = jnp.dot(dtt * a_neg_t, tri_t, preferred_element_type=F32, precision=HIGHEST)
    last_t = acum_t[:, end_row:end_row + 1]
    return {"bc": bc, "xs_t": xs_t, "mask": tri > 0.0, "acum": acum, "acum_t": acum_t, "dt_t": dtt,
            "chunk_decay_t": jnp.exp2(last_t),
            "decay_in_t": jnp.exp2(acum_t),
            "coef_t": jnp.exp2(last_t - acum_t) * dtt}


def _ssd_scan_chunks(scans):
    zero = jnp.zeros((SSD_HEADDIM, CHUNK), BF16)
    ys = [[] for _ in scans]
    for g in range(SSD_GROUPS):
        hs = slice(g * HEADS_PER_GROUP * SSD_HEADDIM, (g + 1) * HEADS_PER_GROUP * SSD_HEADDIM)
        heads = range(g * HEADS_PER_GROUP, (g + 1) * HEADS_PER_GROUP)
        stage = []
        for t, s_ref in scans:
            bm = t["bc"][:, g * D_STATE:(g + 1) * D_STATE].astype(BF16)
            cm = t["bc"][:, (SSD_GROUPS + g) * D_STATE:(SSD_GROUPS + g + 1) * D_STATE].astype(BF16)
            cbm = jnp.where(t["mask"], _bdot_nt(cm, bm), 0.0)
            state = s_ref[hs, :]
            y_off = _bdot_nt(state, cm)
            y_off = jnp.concatenate(
                [y_off[(h - heads[0]) * SSD_HEADDIM:(h - heads[0] + 1) * SSD_HEADDIM] * t["decay_in_t"][h:h + 1, :]
                 for h in heads], axis=0)
            stage.append((bm, cbm, state, y_off, []))
        for pair in range(HEADS_PER_GROUP // 2):
            for (t, _), (_, cbm, _, _, y_diag) in zip(scans, stage):
                ws, xs = [], []
                for h in (heads[0] + 2 * pair, heads[0] + 2 * pair + 1):
                    a_col = jnp.broadcast_to(t["acum"][:, h:h + 1], (CHUNK, CHUNK))
                    ws.append((cbm * jnp.exp2(jnp.minimum(a_col - t["acum_t"][h:h + 1, :], 0.0))).astype(BF16))
                    xs.append((t["xs_t"][h * SSD_HEADDIM:(h + 1) * SSD_HEADDIM, :] * t["dt_t"][h:h + 1, :]).astype(BF16))
                lhs = jnp.concatenate([jnp.concatenate([xs[0], zero], axis=1),
                                       jnp.concatenate([zero, xs[1]], axis=1)], axis=0)
                y_diag.append(_bdot_nt(lhs, jnp.concatenate(ws, axis=1)))
        for i, ((t, s_ref), (bm, _, state, y_off, y_diag)) in enumerate(zip(scans, stage)):
            ys[i].append(jnp.concatenate(y_diag, axis=0) + y_off)
            x_state = jnp.concatenate(
                [(t["xs_t"][h * SSD_HEADDIM:(h + 1) * SSD_HEADDIM, :] * t["coef_t"][h:h + 1, :]).astype(BF16)
                 for h in heads], axis=0)
            update = jnp.dot(x_state, bm, preferred_element_type=F32)
            for h in heads:
                hp = slice(h * SSD_HEADDIM, (h + 1) * SSD_HEADDIM)
                rel = slice((h - heads[0]) * SSD_HEADDIM, (h - heads[0] + 1) * SSD_HEADDIM)
                s_ref[hp, :] = state[rel] * t["chunk_decay_t"][h:h + 1, :] + update[rel]
    return [jnp.concatenate(y, axis=0) for y in ys]


def _ssd_kernel(xf_ref, xfp_ref, xfn_ref, xb_ref, xbp_ref, xbn_ref, zf_ref, zb_ref, dtf_ref, dtb_ref,
                dttf_ref, dttb_ref, init_ref, tri_ref, shift_ref, cw_ref, cb_ref, bias_ref, biast_ref,
                alog_ref, alogt_ref, dskip_ref, ng_ref, o_ref, fin_ref, xst_ref, bc_ref, sf_ref, sb_ref,
                *, nc, has_init):
    s = pl.program_id(1)
    half = nc // 2
    first_half = s < half
    cf = s
    cb = nc - 1 - s
    rf = pl.multiple_of(cf * CHUNK, CHUNK)
    rb = pl.multiple_of(cb * CHUNK, CHUNK)

    @pl.when(first_half)
    def _():
        for c, rows, refs in ((cf, rf, (xf_ref, xfp_ref, xfn_ref)), (cb, rb, (xb_ref, xbp_ref, xbn_ref))):
            xc = _ssd_conv_chunk(*refs, cw_ref, cb_ref, shift_ref, c > 0, c < nc - 1)
            xst_ref[c] = xc[:, 0:SSD_INNER].T
            bc_ref[pl.ds(rows, CHUNK), :] = xc[:, SSD_INNER:]

    @pl.when(s == 0)
    def _():
        if has_init:
            sf_ref[...] = init_ref[0, 0, 0]
            sb_ref[...] = init_ref[0, 0, 1]
        else:
            sf_ref[...] = jnp.zeros(sf_ref.shape, F32)
            sb_ref[...] = jnp.zeros(sb_ref.shape, F32)

    lower = tri_ref[0]
    upper = tri_ref[1]
    tf = _ssd_scan_prelude(xst_ref[cf], bc_ref[pl.ds(rf, CHUNK), :], dtf_ref[0], dttf_ref[0, 0], lower, upper,
                           bias_ref[0], biast_ref[0], alog_ref[0], alogt_ref[0], CHUNK - 1)
    tb = _ssd_scan_prelude(xst_ref[cb], bc_ref[pl.ds(rb, CHUNK), :], dtb_ref[0], dttb_ref[0, 0], upper, lower,
                           bias_ref[1], biast_ref[1], alog_ref[1], alogt_ref[1], 0)
    yf_t, yb_t = _ssd_scan_chunks([(tf, sf_ref), (tb, sb_ref)])
    yf = (yf_t + dskip_ref[...] * tf["xs_t"]).T
    yb = yb_t.T

    @pl.when(first_half)
    def _():
        o_ref[0, pl.ds(rf, CHUNK), :] = yf
        o_ref[0, pl.ds(rb, CHUNK), :] = yb

    def finish(rows, y, z):
        tot = (o_ref[0, pl.ds(rows, CHUNK), :] + y) * _silu(z)
        ms = jnp.mean(tot * tot, axis=-1, keepdims=True)
        o_ref[0, pl.ds(rows, CHUNK), :] = tot * lax.rsqrt(ms + EPS) * ng_ref[...]

    @pl.when(jnp.logical_not(first_half))
    def _():
        finish(rf, yf, zf_ref[0])
        finish(rb, yb, zb_ref[0])

    @pl.when(s == nc - 1)
    def _():
        fin_ref[0, 0] = sf_ref[...]
        fin_ref[0, 1] = sb_ref[...]


def _ssd_mixer(z, xbc, dt_raw, init_state, layer, consts, p):
    b, L, _ = z.shape
    nc = L // CHUNK
    has_init = init_state is not None
    dt_t = jnp.stack([dt_raw[:, :, 0:SSD_HEADS], dt_raw[:, :, LANES:LANES + SSD_HEADS]], axis=1)
    dt_t = jnp.swapaxes(dt_t, 2, 3)
    assert nc % 2 == 0
    half = nc // 2
    if not has_init:
        init_state = jnp.zeros((1, 1, 1, HALO, LANES), F32)
        init_spec = pl.BlockSpec((1, 1, 1, HALO, LANES), lambda bi, s: (0, 0, 0, 0, 0))
    else:
        init_spec = pl.BlockSpec((1, 1, 2, SSD_INNER, D_STATE), lambda bi, s: (bi, layer, 0, 0, 0))
    per = CHUNK // HALO
    last = L // HALO - 1
    conv_f = lambda s: jnp.minimum(s, half - 1)
    conv_b = lambda s: jnp.maximum(nc - 1 - s, half)
    fin_f = lambda s: jnp.maximum(s, half)
    fin_b = lambda s: jnp.minimum(nc - 1 - s, half - 1)

    def conv_specs(chunk):
        return [pl.BlockSpec((1, CHUNK, CONV_CH), lambda bi, s: (bi, chunk(s), 0)),
                pl.BlockSpec((1, HALO, CONV_CH), lambda bi, s: (bi, jnp.maximum(chunk(s) * per - 1, 0), 0)),
                pl.BlockSpec((1, HALO, CONV_CH), lambda bi, s: (bi, jnp.minimum((chunk(s) + 1) * per, last), 0))]

    in_specs = conv_specs(conv_f) + conv_specs(conv_b) + [
        pl.BlockSpec((1, CHUNK, SSD_INNER), lambda bi, s: (bi, fin_f(s), 0)),
        pl.BlockSpec((1, CHUNK, SSD_INNER), lambda bi, s: (bi, fin_b(s), 0)),
        pl.BlockSpec((1, CHUNK, LANES), lambda bi, s: (bi, s, 0)),
        pl.BlockSpec((1, CHUNK, LANES), lambda bi, s: (bi, nc - 1 - s, 1)),
        pl.BlockSpec((1, 1, SSD_HEADS, CHUNK), lambda bi, s: (bi, 0, 0, s)),
        pl.BlockSpec((1, 1, SSD_HEADS, CHUNK), lambda bi, s: (bi, 1, 0, nc - 1 - s)),
        init_spec,
        _const_spec((2, CHUNK, CHUNK)), _const_spec(((CONV_W - 1) * CHUNK, SHIFT_K)),
        _const_spec((CONV_W, CONV_CH)), _const_spec((1, CONV_CH)),
        _const_spec((2, 1, LANES)), _const_spec((2, SSD_HEADS, 1)), _const_spec((2, 1, LANES)),
        _const_spec((2, SSD_HEADS, 1)), _const_spec((SSD_INNER, CHUNK)), _const_spec((1, SSD_INNER)),
    ]
    out, fin = pl.pallas_call(
        functools.partial(_ssd_kernel, nc=nc, has_init=has_init),
        grid=(b, nc),
        in_specs=in_specs,
        out_specs=[pl.BlockSpec((1, L, SSD_INNER), lambda bi, s: (bi, 0, 0)),
                   pl.BlockSpec((1, 2, SSD_INNER, D_STATE), lambda bi, s: (bi, 0, 0, 0))],
        out_shape=[jax.ShapeDtypeStruct((b, L, SSD_INNER), F32),
                   jax.ShapeDtypeStruct((b, 2, SSD_INNER, D_STATE), F32)],
        scratch_shapes=[pltpu.VMEM((nc, SSD_INNER, CHUNK), F32), pltpu.VMEM((L, CONV_CH - SSD_INNER), F32),
                        pltpu.VMEM((SSD_INNER, D_STATE), F32), pltpu.VMEM((SSD_INNER, D_STATE), F32)],
        compiler_params=_params("parallel", "arbitrary"),
        name="ssd_mixer",
    )(xbc, xbc, xbc, xbc, xbc, xbc, z, z, dt_raw, dt_raw, dt_t, dt_t, init_state, consts["tri"], consts["shift"],
      p["conv_w"], p["conv_b"], p["dt_bias"], p["dt_bias_t"], p["a_log"], p["a_log_t"], p["d_skip"], p["ssd_norm_g"])
    return out, fin


def _ssd_consts():
    i = jnp.arange(CHUNK)
    lower = (i[None, :] <= i[:, None]).astype(F32)
    t = jnp.arange(CHUNK)[:, None]
    col = jnp.arange(SHIFT_K)[None, :]
    shift = jnp.concatenate([(col == HALO + t + kk - CONV_W // 2) for kk in range(CONV_W) if kk != CONV_W // 2], axis=0)
    return {"tri": jnp.stack([lower, lower.T]), "shift": shift.astype(BF16)}


FF_CHUNK = 1024


def _merge_mlp_kernel(x_ref, mod_ref, pool_ref, attn_ref, ssd_ref, gate_ref, g_ref,
                      wp_ref, wa_ref, ws_ref, wo_ref, w1_ref, w2_ref, o_ref):
    mod = mod_ref[0]
    gates = _sigmoid(gate_ref[0])
    merged = (gates[:, 0:D_MODEL] * _bdot(pool_ref[0], wp_ref[...])
              + gates[:, D_MODEL:2 * D_MODEL] * _bdot(attn_ref[0], wa_ref[...])
              + gates[:, 2 * D_MODEL:] * _bdot(ssd_ref[0], ws_ref[...]))
    x = x_ref[0] + mod[:, 2 * D_MODEL:3 * D_MODEL] * _bdot(merged, wo_ref[...])
    h = _rms_mod(x, g_ref[...], mod[:, 4 * D_MODEL:5 * D_MODEL], mod[:, 3 * D_MODEL:4 * D_MODEL]).astype(BF16)
    acc = jnp.zeros(x.shape, F32)
    for j in range(D_FF // FF_CHUNK):
        ff = slice(j * FF_CHUNK, (j + 1) * FF_CHUNK)
        a = jnp.dot(h, w1_ref[:, ff], preferred_element_type=F32)
        acc = acc + _bdot(jnp.square(jnp.maximum(a, 0.0)), w2_ref[ff, :])
    o_ref[0] = x + mod[:, 5 * D_MODEL:] * acc


def _merge_mlp(x, mods, layer, per_batch, pool, attn, ssd, gate, norm_g, w):
    b, L, _ = x.shape
    tm = TOKEN_TILE
    tok = lambda n: pl.BlockSpec((1, tm, n), lambda bi, i: (bi, i, 0))
    weights = [w["pool_o"], w["attn_o"], w["ssd_o"], w["out"], w["mlp1"], w["mlp2"]]
    return pl.pallas_call(
        _merge_mlp_kernel,
        grid=(b, L // tm),
        in_specs=[tok(D_MODEL), _mod_spec(layer, per_batch), tok(POOL_WIDTH), tok(ATTN_WIDTH), tok(SSD_INNER),
                  tok(N_BRANCH * D_MODEL), _const_spec((1, D_MODEL))] + [_const_spec(a.shape) for a in weights],
        out_specs=tok(D_MODEL),
        out_shape=jax.ShapeDtypeStruct((b, L, D_MODEL), F32),
        compiler_params=_params("parallel", "parallel"),
        name="merge_mlp",
    )(x, mods, pool, attn, ssd, gate, norm_g, *weights)


def _layer(x, mods, layer, per_batch, w, p, consts, ctx):
    u, q, k, v, z, xbc, dt_raw, gate = _in_projection(
        x, mods, layer, per_batch, p["norm1_g"], p["q_norm_g"], p["k_norm_g"], consts["bm"],
        None if ctx is None else consts["rope"], w)
    pool = _pool_mixer(u, w["pool_w"], p["pool_scale"])
    if ctx is None:
        attn = _context_attention(q, k, v, p["attn_sink"])
        init = None
    else:
        cache_k, cache_v, init = ctx
        attn = _latent_attention(q, k, v, cache_k, cache_v, layer, p["attn_sink"])
    ssd, fin = _ssd_mixer(z, xbc, dt_raw, init, layer, consts, p)
    x = _merge_mlp(x, mods, layer, per_batch, pool, attn, ssd, gate, p["norm2_g"], w)
    return x, (k, v, fin)


def _layer_weights(l, w_in, pool_w, w_pool_o, w_attn_o, w_ssd_o, w_out, w_mlp1, w_mlp2):
    wi = w_in[l]
    o_z = POOL_WIDTH + ATTN_WIDTH + 2 * KV_WIDTH
    o_x = o_z + SSD_INNER
    o_d = o_x + CONV_CH
    o_g = o_d + 2 * SSD_HEADS
    wd = wi[:, o_d:o_g]
    pad = jnp.zeros((D_MODEL, LANES - SSD_HEADS), F32)
    wd = jnp.concatenate([wd[:, :SSD_HEADS], pad, wd[:, SSD_HEADS:], pad], axis=1)
    cast = lambda a: a.astype(BF16)
    return {"a": cast(wi[:, :o_z]), "z": cast(wi[:, o_z:o_x]), "xbc": cast(wi[:, o_x:o_d]), "dt": cast(wd),
            "gate": cast(wi[:, o_g:]), "pool_w": cast(pool_w[l]), "pool_o": cast(w_pool_o[l]),
            "attn_o": cast(w_attn_o[l]), "ssd_o": cast(w_ssd_o[l]), "out": cast(w_out[l]),
            "mlp1": cast(w_mlp1[l]), "mlp2": cast(w_mlp2[l])}


def _pad_lanes(v):
    return jnp.pad(v, ((0, 0), (0, LANES - SSD_HEADS))).reshape(2, 1, LANES)


def kernel(x_prompt, x_sample, cache_k, cache_v, state_ssd, c, c_ctx, w_mod, b_mod, norm1_g, norm2_g, w_in, pool_w, pool_scale, w_pool_o, q_norm_g, k_norm_g, attn_sink, w_attn_o, conv_w, conv_b, dt_bias, a_log, d_skip, ssd_norm_g, w_ssd_o, w_out, w_mlp1, w_mlp2):
    batch, seq, _ = x_prompt.shape
    dec_batch, dec_seq, _ = x_sample.shape
    past = cache_k.shape[2]
    assert 1 + dec_batch <= MOD_ROWS

    cvecs = jnp.concatenate([c_ctx[None, :], c, jnp.zeros((MOD_ROWS - 1 - dec_batch, D_MODEL), F32)], axis=0)
    mods = _modulation(cvecs, w_mod, b_mod)

    consts = {"bm": _block_mean(), "rope": _rope_tables(dec_seq)}
    consts.update(_ssd_consts())
    cache_k = cache_k.reshape(dec_batch, DEPTH, past, KV_WIDTH).astype(BF16)
    cache_v = cache_v.reshape(dec_batch, DEPTH, past, KV_WIDTH).astype(BF16)
    state = state_ssd.reshape(dec_batch, DEPTH, 2, SSD_INNER, D_STATE)

    y_prompt, y_sample = x_prompt, x_sample
    ks, vs, ss = [], [], []
    for l in range(DEPTH):
        w = _layer_weights(l, w_in, pool_w, w_pool_o, w_attn_o, w_ssd_o, w_out, w_mlp1, w_mlp2)
        p = {"norm1_g": norm1_g[l][None, :], "norm2_g": norm2_g[l][None, :], "pool_scale": pool_scale[l][None, :],
             "q_norm_g": jnp.tile(q_norm_g[l], N_Q_HEADS)[None, :], "k_norm_g": jnp.tile(k_norm_g[l], N_KV_HEADS)[None, :],
             "attn_sink": attn_sink[l], "conv_w": conv_w[l], "conv_b": conv_b[l][None, :],
             "dt_bias": _pad_lanes(dt_bias[l]), "dt_bias_t": dt_bias[l][:, :, None],
             "a_log": _pad_lanes(a_log[l]), "a_log_t": a_log[l][:, :, None],
             "d_skip": jnp.broadcast_to(jnp.repeat(d_skip[l], SSD_HEADDIM)[:, None], (SSD_INNER, CHUNK)),
             "ssd_norm_g": ssd_norm_g[l][None, :]}
        y_prompt, (k_l, v_l, s_l) = _layer(y_prompt, mods, l, False, w, p, consts, None)
        ks.append(k_l.reshape(batch, seq, N_KV_HEADS, HEAD_DIM))
        vs.append(v_l.reshape(batch, seq, N_KV_HEADS, HEAD_DIM))
        ss.append(s_l.reshape(batch, 2, SSD_HEADS, SSD_HEADDIM, D_STATE))
        y_sample, _ = _layer(y_sample, mods, l, True, w, p, consts, (cache_k, cache_v, state))
    return (y_prompt, y_sample, jnp.stack(ks, axis=1), jnp.stack(vs, axis=1), jnp.stack(ss, axis=1))
```

```python
import functools
import math

import jax
import jax.numpy as jnp
from jax import lax
from jax.experimental import pallas as pl
from jax.experimental.pallas import tpu as pltpu

F32 = jnp.float32
BF16 = jnp.bfloat16

D_MODEL = 1024
DEPTH = 2
GRID_W = 64
MOD_CHUNKS = 6
EPS = 1e-6
POOL_WIDTH = D_MODEL // 2
POOL_GROUPS = 4
POOL_GROUP_W = POOL_WIDTH // POOL_GROUPS
POOL_WINDOWS = (2, 4, 8, 16)
N_Q_HEADS = 8
N_KV_HEADS = 2
GQA_GROUP = N_Q_HEADS // N_KV_HEADS
HEAD_DIM = 64
ATTN_WIDTH = N_Q_HEADS * HEAD_DIM
KV_WIDTH = N_KV_HEADS * HEAD_DIM
WINDOW = 128
BLOCK = 128
ROPE_BASE = 10000.0
ATTN_SCALE = HEAD_DIM ** -0.5
SSD_HEADS = 16
SSD_HEADDIM = 64
SSD_INNER = SSD_HEADS * SSD_HEADDIM
SSD_GROUPS = 2
HEADS_PER_GROUP = SSD_HEADS // SSD_GROUPS
D_STATE = 128
CONV_W = 5
CHUNK = 128
CONV_CH = SSD_INNER + 2 * SSD_GROUPS * D_STATE
N_BRANCH = 3
D_FF = 4 * D_MODEL

LANES = 128
HALO = 8
MOD_ROWS = 16
VMEM_LIMIT = 56 * 1024 * 1024
TOKEN_TILE = 256
NEG_INF = float("-inf")
LOG2E = math.log2(math.e)
SCORE_SCALE = ATTN_SCALE * LOG2E


def _params(*sem):
    return pltpu.CompilerParams(dimension_semantics=sem, vmem_limit_bytes=VMEM_LIMIT)


def _const_spec(shape):
    nd = len(shape)
    return pl.BlockSpec(shape, lambda *_: (0,) * nd, pipeline_mode=pl.Buffered(1))


def _bdot(a, b):
    return jnp.dot(a.astype(BF16), b.astype(BF16), preferred_element_type=F32)


def _bdot_nt(a, b):
    return lax.dot_general(a.astype(BF16), b.astype(BF16), (((1,), (1,)), ((), ())),
                           preferred_element_type=F32)


def _sigmoid(x):
    return 1.0 / (1.0 + jnp.exp2(x * -LOG2E))


def _silu(x):
    return x * _sigmoid(x)


def _softplus(x):
    return jnp.maximum(x, 0.0) + jnp.log1p(jnp.exp(-jnp.abs(x)))


def _rms_mod(x, g, scale, shift):
    ms = jnp.mean(x * x, axis=-1, keepdims=True)
    return (x * lax.rsqrt(ms + EPS)) * g * (1.0 + scale) + shift


def _mod_kernel(c_ref, w_ref, b_ref, o_ref):
    o_ref[0] = _bdot(_silu(c_ref[...]), w_ref[0]) + b_ref[0]


def _modulation(cvecs, w_mod, b_mod):
    n = MOD_CHUNKS * D_MODEL
    tn = n // 4
    out = pl.pallas_call(
        _mod_kernel,
        grid=(DEPTH, n // tn),
        in_specs=[pl.BlockSpec((MOD_ROWS, D_MODEL), lambda l, j: (0, 0)),
                  pl.BlockSpec((1, D_MODEL, tn), lambda l, j: (l, 0, j)),
                  pl.BlockSpec((1, 1, tn), lambda l, j: (l, 0, j))],
        out_specs=pl.BlockSpec((1, MOD_ROWS, tn), lambda l, j: (l, 0, j)),
        out_shape=jax.ShapeDtypeStruct((DEPTH, MOD_ROWS, n), F32),
        compiler_params=_params("arbitrary", "arbitrary"),
        name="modulation",
    )(cvecs, w_mod, b_mod.reshape(DEPTH, 1, n))
    return out.reshape(DEPTH * MOD_ROWS, 1, n)


def _mod_spec(layer, per_batch):
    base = layer * MOD_ROWS
    if per_batch:
        return pl.BlockSpec((1, 1, MOD_CHUNKS * D_MODEL), lambda b, i: (base + 1 + b, 0, 0))
    return pl.BlockSpec((1, 1, MOD_CHUNKS * D_MODEL), lambda b, i: (base, 0, 0))


DT_PAD = LANES


def _inproj_kernel(*refs, rope):
    x_ref, mod_ref, g_ref, qg_ref, kg_ref, bm_ref = refs[:6]
    refs = refs[6:]
    if rope:
        cos_ref, sin_ref = refs[:2]
        refs = refs[2:]
    wa_ref, wz_ref, wx_ref, wd_ref, wg_ref, u_ref, q_ref, k_ref, v_ref, z_ref, xbc_ref, dt_ref, gate_ref = refs
    mod = mod_ref[0]
    h = _rms_mod(x_ref[0], g_ref[...], mod[:, D_MODEL:2 * D_MODEL], mod[:, 0:D_MODEL]).astype(BF16)
    a = jnp.dot(h, wa_ref[...], preferred_element_type=F32)
    u_ref[0] = a[:, 0:POOL_WIDTH]
    v_ref[0] = a[:, POOL_WIDTH + ATTN_WIDTH + KV_WIDTH:].astype(v_ref.dtype)
    z_ref[0] = jnp.dot(h, wz_ref[...], preferred_element_type=F32)
    qn = _head_rms(a[:, POOL_WIDTH:POOL_WIDTH + ATTN_WIDTH], bm_ref[...], qg_ref[...])
    kn = _head_rms(a[:, POOL_WIDTH + ATTN_WIDTH:POOL_WIDTH + ATTN_WIDTH + KV_WIDTH], bm_ref[...], kg_ref[...])
    xbc_ref[0] = jnp.dot(h, wx_ref[...], preferred_element_type=F32)
    if rope:
        qn = _rope(qn, cos_ref[...], sin_ref[...])
        kn = _rope(kn, cos_ref[:, 0:KV_WIDTH], sin_ref[:, 0:KV_WIDTH])
    q_ref[0] = (qn * SCORE_SCALE).astype(q_ref.dtype)
    k_ref[0] = kn.astype(k_ref.dtype)
    dt_ref[0] = jnp.dot(h, wd_ref[...], preferred_element_type=F32)
    gate_ref[0] = jnp.dot(h, wg_ref[...], preferred_element_type=F32)


def _in_projection(x, mods, layer, per_batch, norm_g, qg, kg, bm, rope_tabs, w):
    b, L, _ = x.shape
    tm = min(2 * TOKEN_TILE, L)
    rope = rope_tabs is not None
    kv_dtype = BF16 if rope else F32
    outs = ((POOL_WIDTH, F32), (ATTN_WIDTH, BF16), (KV_WIDTH, kv_dtype), (KV_WIDTH, kv_dtype), (SSD_INNER, F32),
            (CONV_CH, F32), (DT_PAD, F32), (N_BRANCH * D_MODEL, F32))
    tok = lambda n: pl.BlockSpec((1, tm, n), lambda bi, i: (bi, i, 0))
    tab = pl.BlockSpec((tm, ATTN_WIDTH), lambda bi, i: (i, 0))
    weights = [w["a"], w["z"], w["xbc"], w["dt"], w["gate"]]
    return pl.pallas_call(
        functools.partial(_inproj_kernel, rope=rope),
        grid=(b, L // tm),
        in_specs=[tok(D_MODEL), _mod_spec(layer, per_batch), _const_spec((1, D_MODEL)), _const_spec(qg.shape),
                  _const_spec(kg.shape), _const_spec(bm.shape)] + ([tab, tab] if rope else [])
                 + [_const_spec(a.shape) for a in weights],
        out_specs=[tok(n) for n, _ in outs],
        out_shape=[jax.ShapeDtypeStruct((b, L, n), dt) for n, dt in outs],
        compiler_params=_params("parallel", "parallel"),
        name="in_projection",
    )(x, mods, norm_g, qg, kg, bm, *(rope_tabs if rope else ()), *weights)


POOL_ROWS = 256
POOL_K = POOL_ROWS + LANES


def _pool_kernel(u_ref, band_ref, pw_ref, ps_ref, o_ref, *, L):
    halo_zeros = jnp.zeros((HALO, POOL_WIDTH), F32)
    tail_zeros = jnp.zeros((POOL_K - POOL_ROWS - 2 * HALO, POOL_WIDTH), F32)
    groups = [slice(gi * POOL_GROUP_W, (gi + 1) * POOL_GROUP_W) for gi in range(POOL_GROUPS)]
    for r0 in range(0, L, POOL_ROWS):
        main = u_ref[0, r0:r0 + POOL_ROWS, :]
        prev = u_ref[0, r0 - HALO:r0, :] if r0 > 0 else halo_zeros
        nxt = u_ref[0, r0 + POOL_ROWS:r0 + POOL_ROWS + HALO, :] if r0 + POOL_ROWS < L else halo_zeros
        rows = jnp.concatenate([prev, main, nxt, tail_zeros], axis=0)
        hi = rows.astype(BF16)
        lo = (rows - hi.astype(F32)).astype(BF16)
        t = r0 + lax.broadcasted_iota(jnp.int32, (POOL_ROWS, POOL_GROUP_W), 0)
        sums = [jnp.dot(band_ref[gi], hi[:, cols], preferred_element_type=F32)
                + jnp.dot(band_ref[gi], lo[:, cols], preferred_element_type=F32) for gi, cols in enumerate(groups)]
        pooled = []
        for w, cols, acc in zip(POOL_WINDOWS, groups, sums):
            cnt = (jnp.minimum(t + w // 2, L) - jnp.maximum(t - w // 2, 0)).astype(F32)
            pooled.append((acc / cnt - main[:, cols]).astype(BF16))
        mixed = [jnp.dot(pooled[gi], pw_ref[gi], preferred_element_type=F32) for gi in range(POOL_GROUPS)]
        for cols, m in zip(groups, mixed):
            o_ref[0, r0:r0 + POOL_ROWS, cols] = m * ps_ref[:, cols]


def _pool_bands():
    d = jnp.arange(POOL_K)[None, :] - HALO - jnp.arange(POOL_ROWS)[:, None]
    return jnp.stack([(d >= -(w // 2)) & (d < w // 2) for w in POOL_WINDOWS]).astype(BF16)


def _pool_mixer(u, pool_w, pool_scale):
    b, L, _ = u.shape
    assert max(POOL_WINDOWS) // 2 <= HALO and L % POOL_ROWS == 0
    seq = pl.BlockSpec((1, L, POOL_WIDTH), lambda bi: (bi, 0, 0))
    bands = _pool_bands()
    return pl.pallas_call(
        functools.partial(_pool_kernel, L=L),
        grid=(b,),
        in_specs=[seq, _const_spec(bands.shape), _const_spec(pool_w.shape), _const_spec(pool_scale.shape)],
        out_specs=seq,
        out_shape=jax.ShapeDtypeStruct((b, L, POOL_WIDTH), F32),
        compiler_params=_params("parallel"),
        name="pool_mixer",
    )(u, bands, pool_w, pool_scale)


def _head_rms(x, block_mean, g):
    sq = x * x
    hi = sq.astype(BF16)
    lo = (sq - hi.astype(F32)).astype(BF16)
    ms = jnp.concatenate(
        [jnp.dot(hi[:, c:c + LANES], block_mean, preferred_element_type=F32)
         + jnp.dot(lo[:, c:c + LANES], block_mean, preferred_element_type=F32)
         for c in range(0, x.shape[-1], LANES)], axis=-1)
    return x * lax.rsqrt(ms + EPS) * g


def _rope(x, cos, sin_signed):
    width = x.shape[-1]
    lane = lax.broadcasted_iota(jnp.int32, x.shape, 1)
    partner = jnp.where((lane & 16) == 0, pltpu.roll(x, width - 16, 1), pltpu.roll(x, 16, 1))
    return x * cos + partner * sin_signed


def _gqa_attention(q, keys, vals, sink_ref, rows, bias):
    kvs = [slice(j * HEAD_DIM, (j + 1) * HEAD_DIM) for j in range(N_KV_HEADS)]
    scores = []
    for j in range(N_KV_HEADS):
        qs = jnp.concatenate([q[:, (j * GQA_GROUP + g) * HEAD_DIM:(j * GQA_GROUP + g + 1) * HEAD_DIM]
                              for g in range(GQA_GROUP)], axis=0)
        scores.append(_bdot_nt(qs, keys[:, kvs[j]]))
    probs, denoms = [], []
    for j in range(N_KV_HEADS):
        p_rows = []
        for g in range(GQA_GROUP):
            s = scores[j][g * rows:(g + 1) * rows]
            if bias is not None:
                s = s + bias
            sink = sink_ref[j * GQA_GROUP + g] * LOG2E
            m = jnp.maximum(jnp.max(s, axis=-1, keepdims=True), sink)
            p = jnp.exp2(s - m)
            denoms.append(jnp.sum(p, axis=-1, keepdims=True) + jnp.exp2(sink - m))
            p_rows.append(p.astype(BF16))
        probs.append(jnp.concatenate(p_rows, axis=0))
    outs = []
    for j in range(N_KV_HEADS):
        o = jnp.dot(probs[j], vals[:, kvs[j]], preferred_element_type=F32)
        outs.extend(o[g * rows:(g + 1) * rows] / denoms[j * GQA_GROUP + g] for g in range(GQA_GROUP))
    return jnp.concatenate(outs, axis=-1)


def _ctx_attn_kernel(sink_ref, q_ref, k_ref, v_ref, o_ref):
    q = q_ref[0]
    o_ref[0] = _gqa_attention(q, k_ref[0].astype(BF16), v_ref[0].astype(BF16), sink_ref, q.shape[0], None)


def _context_attention(q, k, v, sink):
    b, L, _ = q.shape
    seq = lambda n: pl.BlockSpec((1, L, n), lambda bi: (bi, 0, 0))
    return pl.pallas_call(
        _ctx_attn_kernel,
        grid=(b,),
        in_specs=[pl.BlockSpec(memory_space=pltpu.SMEM), seq(ATTN_WIDTH), seq(KV_WIDTH), seq(KV_WIDTH)],
        out_specs=seq(ATTN_WIDTH),
        out_shape=jax.ShapeDtypeStruct((b, L, ATTN_WIDTH), F32),
        compiler_params=_params("parallel"),
        name="context_attention",
    )(sink, q, k, v)


def _lat_attn_kernel(sink_ref, q_ref, kp_ref, k0_ref, kn_ref, vp_ref, v0_ref, vn_ref, kc_ref, vc_ref, band_ref, o_ref):
    keys = jnp.concatenate([kp_ref[0], k0_ref[0], kn_ref[0], kc_ref[0, 0]], axis=0)
    vals = jnp.concatenate([vp_ref[0], v0_ref[0], vn_ref[0], vc_ref[0, 0]], axis=0)
    o_ref[0] = _gqa_attention(q_ref[0], keys, vals, sink_ref, BLOCK, band_ref[0])


def _latent_attention(q, k, v, cache_k, cache_v, layer, sink):
    b, L, _ = q.shape
    past = cache_k.shape[2]
    nb = L // BLOCK
    assert nb >= 2
    blk = lambda n: pl.BlockSpec((1, BLOCK, n), lambda bi, i: (bi, i, 0))
    prev = pl.BlockSpec((1, BLOCK, KV_WIDTH), lambda bi, i: (bi, jnp.maximum(i - 1, 0), 0))
    nxt = pl.BlockSpec((1, BLOCK, KV_WIDTH), lambda bi, i: (bi, jnp.minimum(i + 1, nb - 1), 0))
    cache = pl.BlockSpec((1, 1, past, KV_WIDTH), lambda bi, i: (bi, layer, 0, 0))
    band_spec = pl.BlockSpec((1, BLOCK, 3 * BLOCK + past),
                             lambda bi, i: (jnp.minimum(i, 1) + jnp.maximum(i - (nb - 2), 0), 0, 0))
    return pl.pallas_call(
        _lat_attn_kernel,
        grid=(b, nb),
        in_specs=[pl.BlockSpec(memory_space=pltpu.SMEM), blk(ATTN_WIDTH), prev, blk(KV_WIDTH), nxt,
                  prev, blk(KV_WIDTH), nxt, cache, cache, band_spec],
        out_specs=blk(ATTN_WIDTH),
        out_shape=jax.ShapeDtypeStruct((b, L, ATTN_WIDTH), F32),
        compiler_params=_params("parallel", "parallel"),
        name="latent_attention",
    )(sink, q, k, k, k, v, v, v, cache_k, cache_v, _band_bias(past))


def _band_bias(past):
    r = jnp.arange(BLOCK)[:, None]
    col = jnp.arange(3 * BLOCK + past)[None, :]
    band = (jnp.abs(col - BLOCK - r) <= WINDOW) | (col >= 3 * BLOCK)
    first = band & (col >= BLOCK)
    last = band & ((col < 2 * BLOCK) | (col >= 3 * BLOCK))
    return jnp.where(jnp.stack([first, band, last]), 0.0, NEG_INF).astype(F32)


def _rope_tables(L):
    t = jnp.arange(L)
    row = (t // GRID_W).astype(F32)
    col = (t % GRID_W).astype(F32)
    nf = HEAD_DIM // 4
    inv = ROPE_BASE ** (-jnp.arange(nf, dtype=F32) / nf)
    ang_r = row[:, None] * inv[None, :]
    ang_c = col[:, None] * inv[None, :]
    cos = jnp.concatenate([jnp.cos(ang_r), jnp.cos(ang_r), jnp.cos(ang_c), jnp.cos(ang_c)], axis=-1)
    sin = jnp.concatenate([-jnp.sin(ang_r), jnp.sin(ang_r), -jnp.sin(ang_c), jnp.sin(ang_c)], axis=-1)
    return jnp.tile(cos, (1, N_Q_HEADS)), jnp.tile(sin, (1, N_Q_HEADS))


def _block_mean():
    i = jnp.arange(LANES) // HEAD_DIM
    return jnp.where(i[:, None] == i[None, :], 1.0 / HEAD_DIM, 0.0).astype(BF16)


SHIFT_K = 2 * LANES

def _ssd_conv_chunk(xbc_ref, xp_ref, xn_ref, cw_ref, cb_ref, shift_ref, has_prev, has_next):
    main = xbc_ref[0]
    rows = jnp.concatenate([jnp.where(has_prev, xp_ref[0], 0.0), main, jnp.where(has_next, xn_ref[0], 0.0),
                            jnp.zeros((SHIFT_K - CHUNK - 2 * HALO, CONV_CH), F32)], axis=0).astype(BF16)
    taps = jnp.dot(shift_ref[...], rows, preferred_element_type=F32)
    acc = cb_ref[...] + main * cw_ref[CONV_W // 2:CONV_W // 2 + 1, :]
    for i, kk in enumerate(k for k in range(CONV_W) if k != CONV_W // 2):
        acc = acc + taps[i * CHUNK:(i + 1) * CHUNK] * cw_ref[kk:kk + 1, :]
    return _silu(acc)


def _split3(x):
    hi = x.astype(BF16)
    rest = x - hi.astype(F32)
    mid = rest.astype(BF16)
    return hi, mid, (rest - mid.astype(F32)).astype(BF16)


SSD_END_ROWS = (CHUNK - 1, 0)


def _ssd_decay_table_stages(dtt_refs, tri_ref, biast_ref, alogt_ref, tabt_ref, terms_ref):
    steps = []
    for d, dtt_ref in enumerate(dtt_refs):
        dtt = _softplus(dtt_ref[0, 0] + biast_ref[d])
        steps.append((dtt, _split3(dtt * (-jnp.exp(alogt_ref[d]) * LOG2E))))
    yield
    sums = []
    for d, (_, parts_t) in enumerate(steps):
        sums.append(sum(jnp.dot(part, tri_ref[1 - d], preferred_element_type=F32) for part in parts_t))
    yield
    pad = jnp.zeros((CHUNK - 3 * SSD_HEADS, CHUNK), F32)
    terms = [jnp.concatenate([p.astype(F32) for p in _split3(acum_t)] + [pad], axis=0).T.astype(BF16)
             for acum_t in sums]
    yield
    for d, ((dtt, _), acum_t) in enumerate(zip(steps, sums)):
        last_t = acum_t[:, SSD_END_ROWS[d]:SSD_END_ROWS[d] + 1]
        terms_ref[d] = terms[d]
        tabt_ref[d, 0] = acum_t
        tabt_ref[d, 1] = dtt
        tabt_ref[d, 2] = jnp.exp2(acum_t)
        tabt_ref[d, 3] = jnp.exp2(last_t - acum_t) * dtt


def _ssd_load_decay_tables(operands, tri_ref, spread_ref, tabt_ref, terms_ref):
    a_cols = jnp.dot(terms_ref[...].reshape(2 * CHUNK, CHUNK), spread_ref[...], preferred_element_type=F32)
    tables = []
    for d, (xs_t, bc) in enumerate(operands):
        acum_t = tabt_ref[d, 0]
        end = SSD_END_ROWS[d]
        tables.append({"bc": bc, "xs_t": xs_t, "mask": tri_ref[d] > 0, "acum_t": acum_t,
                       "a_col": a_cols[d * CHUNK:(d + 1) * CHUNK],
                       "dt_t": tabt_ref[d, 1], "decay_in_t": tabt_ref[d, 2], "coef_t": tabt_ref[d, 3],
                       "chunk_decay_t": jnp.exp2(acum_t[:, end:end + 1])})
    return tables


BACKGROUND_POINTS = ((0, 0), (0, 2), (1, 0))


def _ssd_scan_chunks(scans, background):
    zero = jnp.zeros((SSD_HEADDIM, CHUNK), BF16)
    ys = [[] for _ in scans]
    next(background, None)
    for g in range(SSD_GROUPS):
        hs = slice(g * HEADS_PER_GROUP * SSD_HEADDIM, (g + 1) * HEADS_PER_GROUP * SSD_HEADDIM)
        heads = range(g * HEADS_PER_GROUP, (g + 1) * HEADS_PER_GROUP)
        stage = []
        for t, s_ref in scans:
            bm = t["bc"][:, g * D_STATE:(g + 1) * D_STATE].astype(BF16)
            cm = t["bc"][:, (SSD_GROUPS + g) * D_STATE:(SSD_GROUPS + g + 1) * D_STATE].astype(BF16)
            cbm = jnp.where(t["mask"], _bdot_nt(cm, bm), 0.0)
            state = s_ref[hs, :]
            y_off = _bdot_nt(state, cm)
            y_off = jnp.concatenate(
                [y_off[(h - heads[0]) * SSD_HEADDIM:(h - heads[0] + 1) * SSD_HEADDIM] * t["decay_in_t"][h:h + 1, :]
                 for h in heads], axis=0)
            stage.append((bm, cbm, state, y_off, []))
        for pair in range(HEADS_PER_GROUP // 2):
            if (g, pair) in BACKGROUND_POINTS:
                next(background, None)
            for (t, _), (_, cbm, _, _, y_diag) in zip(scans, stage):
                ws, xs = [], []
                for h in (heads[0] + 2 * pair, heads[0] + 2 * pair + 1):
                    a_col = t["a_col"][:, h * CHUNK:(h + 1) * CHUNK]
                    ws.append((cbm * jnp.exp2(jnp.minimum(a_col - t["acum_t"][h:h + 1, :], 0.0))).astype(BF16))
                    xs.append((t["xs_t"][h * SSD_HEADDIM:(h + 1) * SSD_HEADDIM, :] * t["dt_t"][h:h + 1, :]).astype(BF16))
                lhs = jnp.concatenate([jnp.concatenate([xs[0], zero], axis=1),
                                       jnp.concatenate([zero, xs[1]], axis=1)], axis=0)
                y_diag.append(_bdot_nt(lhs, jnp.concatenate(ws, axis=1)))
        for i, ((t, s_ref), (bm, _, state, y_off, y_diag)) in enumerate(zip(scans, stage)):
            ys[i].append(jnp.concatenate(y_diag, axis=0) + y_off)
            x_state = jnp.concatenate(
                [(t["xs_t"][h * SSD_HEADDIM:(h + 1) * SSD_HEADDIM, :] * t["coef_t"][h:h + 1, :]).astype(BF16)
                 for h in heads], axis=0)
            update = jnp.dot(x_state, bm, preferred_element_type=F32)
            for h in heads:
                hp = slice(h * SSD_HEADDIM, (h + 1) * SSD_HEADDIM)
                rel = slice((h - heads[0]) * SSD_HEADDIM, (h - heads[0] + 1) * SSD_HEADDIM)
                s_ref[hp, :] = state[rel] * t["chunk_decay_t"][h:h + 1, :] + update[rel]
    return [jnp.concatenate(y, axis=0) for y in ys]


def _ssd_kernel(xf_ref, xfp_ref, xfn_ref, xb_ref, xbp_ref, xbn_ref, zf_ref, zb_ref, dtt0f_ref, dtt0b_ref,
                dttnf_ref, dttnb_ref, init_ref, tri_ref, spread_ref, shift_ref, cw_ref, cb_ref, biast_ref,
                alogt_ref, dskip_ref, ng_ref, o_ref, fin_ref, xst_ref, bc_ref, sf_ref, sb_ref, tabt_ref, terms_ref,
                *, nc, has_init):
    s = pl.program_id(1)
    half = nc // 2
    first_half = s < half
    cf = s
    cb = nc - 1 - s
    rf = pl.multiple_of(cf * CHUNK, CHUNK)
    rb = pl.multiple_of(cb * CHUNK, CHUNK)

    @pl.when(first_half)
    def _():
        for c, rows, refs in ((cf, rf, (xf_ref, xfp_ref, xfn_ref)), (cb, rb, (xb_ref, xbp_ref, xbn_ref))):
            xc = _ssd_conv_chunk(*refs, cw_ref, cb_ref, shift_ref, c > 0, c < nc - 1)
            xst_ref[c] = xc[:, 0:SSD_INNER].T
            bc_ref[pl.ds(rows, CHUNK), :] = xc[:, SSD_INNER:]

    table_refs = (tri_ref, biast_ref, alogt_ref, tabt_ref, terms_ref)

    @pl.when(s == 0)
    def _():
        if has_init:
            sf_ref[...] = init_ref[0, 0, 0]
            sb_ref[...] = init_ref[0, 0, 1]
        else:
            sf_ref[...] = jnp.zeros(sf_ref.shape, F32)
            sb_ref[...] = jnp.zeros(sb_ref.shape, F32)
        for _ in _ssd_decay_table_stages((dtt0f_ref, dtt0b_ref), *table_refs):
            pass

    tf, tb = _ssd_load_decay_tables([(xst_ref[cf], bc_ref[pl.ds(rf, CHUNK), :]),
                                     (xst_ref[cb], bc_ref[pl.ds(rb, CHUNK), :])],
                                    tri_ref, spread_ref, tabt_ref, terms_ref)
    next_tables = _ssd_decay_table_stages((dttnf_ref, dttnb_ref), *table_refs)
    yf_t, yb_t = _ssd_scan_chunks([(tf, sf_ref), (tb, sb_ref)], next_tables)
    for _ in next_tables:
        pass
    yf = (yf_t + dskip_ref[...] * tf["xs_t"]).T
    yb = yb_t.T

    @pl.when(first_half)
    def _():
        o_ref[0, pl.ds(rf, CHUNK), :] = yf
        o_ref[0, pl.ds(rb, CHUNK), :] = yb

    def finish(rows, y, z):
        tot = (o_ref[0, pl.ds(rows, CHUNK), :] + y) * _silu(z)
        ms = jnp.mean(tot * tot, axis=-1, keepdims=True)
        o_ref[0, pl.ds(rows, CHUNK), :] = tot * lax.rsqrt(ms + EPS) * ng_ref[...]

    @pl.when(jnp.logical_not(first_half))
    def _():
        finish(rf, yf, zf_ref[0])
        finish(rb, yb, zb_ref[0])

    @pl.when(s == nc - 1)
    def _():
        fin_ref[0, 0] = sf_ref[...]
        fin_ref[0, 1] = sb_ref[...]


def _ssd_mixer(z, xbc, dt_raw, init_state, layer, consts, p):
    b, L, _ = z.shape
    nc = L // CHUNK
    has_init = init_state is not None
    dt_t = jnp.stack([dt_raw[:, :, 0:SSD_HEADS], dt_raw[:, :, SSD_HEADS:2 * SSD_HEADS]], axis=1)
    dt_t = jnp.swapaxes(dt_t, 2, 3)
    assert nc % 2 == 0
    half = nc // 2
    if not has_init:
        init_state = jnp.zeros((1, 1, 1, HALO, LANES), F32)
        init_spec = pl.BlockSpec((1, 1, 1, HALO, LANES), lambda bi, s: (0, 0, 0, 0, 0))
    else:
        init_spec = pl.BlockSpec((1, 1, 2, SSD_INNER, D_STATE), lambda bi, s: (bi, layer, 0, 0, 0))
    per = CHUNK // HALO
    last = L // HALO - 1
    conv_f = lambda s: jnp.minimum(s, half - 1)
    conv_b = lambda s: jnp.maximum(nc - 1 - s, half)
    fin_f = lambda s: jnp.maximum(s, half)
    fin_b = lambda s: jnp.minimum(nc - 1 - s, half - 1)

    def conv_specs(chunk):
        return [pl.BlockSpec((1, CHUNK, CONV_CH), lambda bi, s: (bi, chunk(s), 0)),
                pl.BlockSpec((1, HALO, CONV_CH), lambda bi, s: (bi, jnp.maximum(chunk(s) * per - 1, 0), 0)),
                pl.BlockSpec((1, HALO, CONV_CH), lambda bi, s: (bi, jnp.minimum((chunk(s) + 1) * per, last), 0))]

    in_specs = conv_specs(conv_f) + conv_specs(conv_b) + [
        pl.BlockSpec((1, CHUNK, SSD_INNER), lambda bi, s: (bi, fin_f(s), 0)),
        pl.BlockSpec((1, CHUNK, SSD_INNER), lambda bi, s: (bi, fin_b(s), 0)),
        pl.BlockSpec((1, 1, SSD_HEADS, CHUNK), lambda bi, s: (bi, 0, 0, 0)),
        pl.BlockSpec((1, 1, SSD_HEADS, CHUNK), lambda bi, s: (bi, 1, 0, nc - 1)),
        pl.BlockSpec((1, 1, SSD_HEADS, CHUNK), lambda bi, s: (bi, 0, 0, jnp.minimum(s + 1, nc - 1))),
        pl.BlockSpec((1, 1, SSD_HEADS, CHUNK), lambda bi, s: (bi, 1, 0, jnp.maximum(nc - 2 - s, 0))),
        init_spec,
        _const_spec((2, CHUNK, CHUNK)), _const_spec((CHUNK, SSD_HEADS * CHUNK)),
        _const_spec(((CONV_W - 1) * CHUNK, SHIFT_K)), _const_spec((CONV_W, CONV_CH)), _const_spec((1, CONV_CH)),
        _const_spec((2, SSD_HEADS, 1)), _const_spec((2, SSD_HEADS, 1)), _const_spec((SSD_INNER, CHUNK)),
        _const_spec((1, SSD_INNER)),
    ]
    out, fin = pl.pallas_call(
        functools.partial(_ssd_kernel, nc=nc, has_init=has_init),
        grid=(b, nc),
        in_specs=in_specs,
        out_specs=[pl.BlockSpec((1, L, SSD_INNER), lambda bi, s: (bi, 0, 0)),
                   pl.BlockSpec((1, 2, SSD_INNER, D_STATE), lambda bi, s: (bi, 0, 0, 0))],
        out_shape=[jax.ShapeDtypeStruct((b, L, SSD_INNER), F32),
                   jax.ShapeDtypeStruct((b, 2, SSD_INNER, D_STATE), F32)],
        scratch_shapes=[pltpu.VMEM((nc, SSD_INNER, CHUNK), F32), pltpu.VMEM((L, CONV_CH - SSD_INNER), F32),
                        pltpu.VMEM((SSD_INNER, D_STATE), F32), pltpu.VMEM((SSD_INNER, D_STATE), F32),
                        pltpu.VMEM((2, 4, SSD_HEADS, CHUNK), F32), pltpu.VMEM((2, CHUNK, CHUNK), BF16)],
        compiler_params=_params("parallel", "arbitrary"),
        name="ssd_mixer",
    )(xbc, xbc, xbc, xbc, xbc, xbc, z, z, dt_t, dt_t, dt_t, dt_t, init_state, consts["tri"], consts["spread"],
      consts["shift"],
      p["conv_w"], p["conv_b"], p["dt_bias_t"], p["a_log_t"], p["d_skip"], p["ssd_norm_g"])
    return out, fin


def _ssd_consts():
    i = jnp.arange(CHUNK)
    lower = (i[None, :] <= i[:, None]).astype(F32)
    t = jnp.arange(CHUNK)[:, None]
    col = jnp.arange(SHIFT_K)[None, :]
    shift = jnp.concatenate([(col == HALO + t + kk - CONV_W // 2) for kk in range(CONV_W) if kk != CONV_W // 2], axis=0)
    k = jnp.arange(CHUNK)[:, None]
    lane_head = jnp.arange(SSD_HEADS * CHUNK)[None, :] // CHUNK
    spread = (k < 3 * SSD_HEADS) & (k % SSD_HEADS == lane_head)
    return {"tri": jnp.stack([lower, lower.T]).astype(BF16), "shift": shift.astype(BF16),
            "spread": spread.astype(BF16)}


FF_CHUNK = 1024


def _merge_mlp_kernel(x_ref, mod_ref, pool_ref, attn_ref, ssd_ref, gate_ref, g_ref,
                      wp_ref, wa_ref, ws_ref, wo_ref, w1_ref, w2_ref, o_ref):
    mod = mod_ref[0]
    gates = _sigmoid(gate_ref[0])
    merged = (gates[:, 0:D_MODEL] * _bdot(pool_ref[0], wp_ref[...])
              + gates[:, D_MODEL:2 * D_MODEL] * _bdot(attn_ref[0], wa_ref[...])
              + gates[:, 2 * D_MODEL:] * _bdot(ssd_ref[0], ws_ref[...]))
    x = x_ref[0] + mod[:, 2 * D_MODEL:3 * D_MODEL] * _bdot(merged, wo_ref[...])
    h = _rms_mod(x, g_ref[...], mod[:, 4 * D_MODEL:5 * D_MODEL], mod[:, 3 * D_MODEL:4 * D_MODEL]).astype(BF16)
    acc = jnp.zeros(x.shape, F32)
    for j in range(D_FF // FF_CHUNK):
        ff = slice(j * FF_CHUNK, (j + 1) * FF_CHUNK)
        a = jnp.dot(h, w1_ref[:, ff], preferred_element_type=F32)
        acc = acc + _bdot(jnp.square(jnp.maximum(a, 0.0)), w2_ref[ff, :])
    o_ref[0] = x + mod[:, 5 * D_MODEL:] * acc


def _merge_mlp(x, mods, layer, per_batch, pool, attn, ssd, gate, norm_g, w):
    b, L, _ = x.shape
    tm = TOKEN_TILE
    tok = lambda n: pl.BlockSpec((1, tm, n), lambda bi, i: (bi, i, 0))
    weights = [w["pool_o"], w["attn_o"], w["ssd_o"], w["out"], w["mlp1"], w["mlp2"]]
    return pl.pallas_call(
        _merge_mlp_kernel,
        grid=(b, L // tm),
        in_specs=[tok(D_MODEL), _mod_spec(layer, per_batch), tok(POOL_WIDTH), tok(ATTN_WIDTH), tok(SSD_INNER),
                  tok(N_BRANCH * D_MODEL), _const_spec((1, D_MODEL))] + [_const_spec(a.shape) for a in weights],
        out_specs=tok(D_MODEL),
        out_shape=jax.ShapeDtypeStruct((b, L, D_MODEL), F32),
        compiler_params=_params("parallel", "parallel"),
        name="merge_mlp",
    )(x, mods, pool, attn, ssd, gate, norm_g, *weights)


def _layer(x, mods, layer, per_batch, w, p, consts, ctx):
    u, q, k, v, z, xbc, dt_raw, gate = _in_projection(
        x, mods, layer, per_batch, p["norm1_g"], p["q_norm_g"], p["k_norm_g"], consts["bm"],
        None if ctx is None else consts["rope"], w)
    pool = _pool_mixer(u, w["pool_w"], p["pool_scale"])
    if ctx is None:
        attn = _context_attention(q, k, v, p["attn_sink"])
        init = None
    else:
        cache_k, cache_v, init = ctx
        attn = _latent_attention(q, k, v, cache_k, cache_v, layer, p["attn_sink"])
    ssd, fin = _ssd_mixer(z, xbc, dt_raw, init, layer, consts, p)
    x = _merge_mlp(x, mods, layer, per_batch, pool, attn, ssd, gate, p["norm2_g"], w)
    return x, (k, v, fin)


def _layer_weights(l, w_in, pool_w, w_pool_o, w_attn_o, w_ssd_o, w_out, w_mlp1, w_mlp2):
    wi = w_in[l]
    o_z = POOL_WIDTH + ATTN_WIDTH + 2 * KV_WIDTH
    o_x = o_z + SSD_INNER
    o_d = o_x + CONV_CH
    o_g = o_d + 2 * SSD_HEADS
    wd = jnp.pad(wi[:, o_d:o_g], ((0, 0), (0, DT_PAD - 2 * SSD_HEADS)))
    cast = lambda a: a.astype(BF16)
    return {"a": cast(wi[:, :o_z]), "z": cast(wi[:, o_z:o_x]), "xbc": cast(wi[:, o_x:o_d]), "dt": cast(wd),
            "gate": cast(wi[:, o_g:]), "pool_w": cast(pool_w[l]), "pool_o": cast(w_pool_o[l]),
            "attn_o": cast(w_attn_o[l]), "ssd_o": cast(w_ssd_o[l]), "out": cast(w_out[l]),
            "mlp1": cast(w_mlp1[l]), "mlp2": cast(w_mlp2[l])}


def kernel(x_prompt, x_sample, cache_k, cache_v, state_ssd, c, c_ctx, w_mod, b_mod, norm1_g, norm2_g, w_in, pool_w, pool_scale, w_pool_o, q_norm_g, k_norm_g, attn_sink, w_attn_o, conv_w, conv_b, dt_bias, a_log, d_skip, ssd_norm_g, w_ssd_o, w_out, w_mlp1, w_mlp2):
    batch, seq, _ = x_prompt.shape
    dec_batch, dec_seq, _ = x_sample.shape
    past = cache_k.shape[2]
    assert 1 + dec_batch <= MOD_ROWS

    cvecs = jnp.concatenate([c_ctx[None, :], c, jnp.zeros((MOD_ROWS - 1 - dec_batch, D_MODEL), F32)], axis=0)
    mods = _modulation(cvecs, w_mod, b_mod)

    consts = {"bm": _block_mean(), "rope": _rope_tables(dec_seq)}
    consts.update(_ssd_consts())
    cache_k = cache_k.reshape(dec_batch, DEPTH, past, KV_WIDTH).astype(BF16)
    cache_v = cache_v.reshape(dec_batch, DEPTH, past, KV_WIDTH).astype(BF16)
    state = state_ssd.reshape(dec_batch, DEPTH, 2, SSD_INNER, D_STATE)

    y_prompt, y_sample = x_prompt, x_sample
    ks, vs, ss = [], [], []
    for l in range(DEPTH):
        w = _layer_weights(l, w_in, pool_w, w_pool_o, w_attn_o, w_ssd_o, w_out, w_mlp1, w_mlp2)
        p = {"norm1_g": norm1_g[l][None, :], "norm2_g": norm2_g[l][None, :], "pool_scale": pool_scale[l][None, :],
             "q_norm_g": jnp.tile(q_norm_g[l], N_Q_HEADS)[None, :], "k_norm_g": jnp.tile(k_norm_g[l], N_KV_HEADS)[None, :],
             "attn_sink": attn_sink[l], "conv_w": conv_w[l], "conv_b": conv_b[l][None, :],
             "dt_bias_t": dt_bias[l][:, :, None], "a_log_t": a_log[l][:, :, None],
             "d_skip": jnp.broadcast_to(jnp.repeat(d_skip[l], SSD_HEADDIM)[:, None], (SSD_INNER, CHUNK)),
             "ssd_norm_g": ssd_norm_g[l][None, :]}
        y_prompt, (k_l, v_l, s_l) = _layer(y_prompt, mods, l, False, w, p, consts, None)
        ks.append(k_l.reshape(batch, seq, N_KV_HEADS, HEAD_DIM))
        vs.append(v_l.reshape(batch, seq, N_KV_HEADS, HEAD_DIM))
        ss.append(s_l.reshape(batch, 2, SSD_HEADS, SSD_HEADDIM, D_STATE))
        y_sample, _ = _layer(y_sample, mods, l, True, w, p, consts, (cache_k, cache_v, state))
    return (y_prompt, y_sample, jnp.stack(ks, axis=1), jnp.stack(vs, axis=1), jnp.stack(ss, axis=1))
```

```python
import functools
import math

import jax
import jax.numpy as jnp
from jax import lax
from jax.experimental import pallas as pl
from jax.experimental.pallas import tpu as pltpu

F32 = jnp.float32
BF16 = jnp.bfloat16

D_MODEL = 1024
DEPTH = 2
GRID_W = 64
MOD_CHUNKS = 6
EPS = 1e-6
POOL_WIDTH = D_MODEL // 2
POOL_GROUPS = 4
POOL_GROUP_W = POOL_WIDTH // POOL_GROUPS
POOL_WINDOWS = (2, 4, 8, 16)
N_Q_HEADS = 8
N_KV_HEADS = 2
GQA_GROUP = N_Q_HEADS // N_KV_HEADS
HEAD_DIM = 64
ATTN_WIDTH = N_Q_HEADS * HEAD_DIM
KV_WIDTH = N_KV_HEADS * HEAD_DIM
WINDOW = 128
BLOCK = 128
ROPE_BASE = 10000.0
ATTN_SCALE = HEAD_DIM ** -0.5
SSD_HEADS = 16
SSD_HEADDIM = 64
SSD_INNER = SSD_HEADS * SSD_HEADDIM
SSD_GROUPS = 2
HEADS_PER_GROUP = SSD_HEADS // SSD_GROUPS
D_STATE = 128
CONV_W = 5
CHUNK = 128
CONV_CH = SSD_INNER + 2 * SSD_GROUPS * D_STATE
N_BRANCH = 3
D_FF = 4 * D_MODEL

LANES = 128
HALO = 8
MOD_ROWS = 16
VMEM_LIMIT = 56 * 1024 * 1024
TOKEN_TILE = 256
NEG_INF = float("-inf")
LOG2E = math.log2(math.e)
SCORE_SCALE = ATTN_SCALE * LOG2E


def _params(*sem):
    return pltpu.CompilerParams(dimension_semantics=sem, vmem_limit_bytes=VMEM_LIMIT)


def _const_spec(shape):
    nd = len(shape)
    return pl.BlockSpec(shape, lambda *_: (0,) * nd, pipeline_mode=pl.Buffered(1))


def _bdot(a, b):
    return jnp.dot(a.astype(BF16), b.astype(BF16), preferred_element_type=F32)


def _bdot_nt(a, b):
    return lax.dot_general(a.astype(BF16), b.astype(BF16), (((1,), (1,)), ((), ())),
                           preferred_element_type=F32)


def _sigmoid(x):
    return 1.0 / (1.0 + jnp.exp2(x * -LOG2E))


def _silu(x):
    return x * _sigmoid(x)


def _softplus(x):
    return jnp.maximum(x, 0.0) + jnp.log1p(jnp.exp(-jnp.abs(x)))


def _rms_mod(x, g, scale, shift):
    ms = jnp.mean(x * x, axis=-1, keepdims=True)
    return (x * lax.rsqrt(ms + EPS)) * g * (1.0 + scale) + shift


def _mod_kernel(c_ref, w_ref, b_ref, o_ref):
    o_ref[0] = _bdot(_silu(c_ref[...]), w_ref[0]) + b_ref[0]


def _modulation(cvecs, w_mod, b_mod):
    n = MOD_CHUNKS * D_MODEL
    tn = n // 4
    out = pl.pallas_call(
        _mod_kernel,
        grid=(DEPTH, n // tn),
        in_specs=[pl.BlockSpec((MOD_ROWS, D_MODEL), lambda l, j: (0, 0)),
                  pl.BlockSpec((1, D_MODEL, tn), lambda l, j: (l, 0, j)),
                  pl.BlockSpec((1, 1, tn), lambda l, j: (l, 0, j))],
        out_specs=pl.BlockSpec((1, MOD_ROWS, tn), lambda l, j: (l, 0, j)),
        out_shape=jax.ShapeDtypeStruct((DEPTH, MOD_ROWS, n), F32),
        compiler_params=_params("arbitrary", "arbitrary"),
        name="modulation",
    )(cvecs, w_mod, b_mod.reshape(DEPTH, 1, n))
    return out.reshape(DEPTH * MOD_ROWS, 1, n)


def _mod_spec(layer, per_batch):
    base = layer * MOD_ROWS
    if per_batch:
        return pl.BlockSpec((1, 1, MOD_CHUNKS * D_MODEL), lambda b, i: (base + 1 + b, 0, 0))
    return pl.BlockSpec((1, 1, MOD_CHUNKS * D_MODEL), lambda b, i: (base, 0, 0))


DT_PAD = LANES


def _inproj_kernel(*refs, rope):
    x_ref, mod_ref, g_ref, qg_ref, kg_ref, bm_ref = refs[:6]
    refs = refs[6:]
    if rope:
        cos_ref, sin_ref = refs[:2]
        refs = refs[2:]
    wa_ref, wz_ref, wx_ref, wd_ref, wg_ref, u_ref, q_ref, k_ref, v_ref, z_ref, xbc_ref, dt_ref, gate_ref = refs
    mod = mod_ref[0]
    h = _rms_mod(x_ref[0], g_ref[...], mod[:, D_MODEL:2 * D_MODEL], mod[:, 0:D_MODEL]).astype(BF16)
    a = jnp.dot(h, wa_ref[...], preferred_element_type=F32)
    u_ref[0] = a[:, 0:POOL_WIDTH]
    v_ref[0] = a[:, POOL_WIDTH + ATTN_WIDTH + KV_WIDTH:].astype(v_ref.dtype)
    z_ref[0] = jnp.dot(h, wz_ref[...], preferred_element_type=F32)
    qn = _head_rms(a[:, POOL_WIDTH:POOL_WIDTH + ATTN_WIDTH], bm_ref[...], qg_ref[...])
    kn = _head_rms(a[:, POOL_WIDTH + ATTN_WIDTH:POOL_WIDTH + ATTN_WIDTH + KV_WIDTH], bm_ref[...], kg_ref[...])
    xbc_ref[0] = jnp.dot(h, wx_ref[...], preferred_element_type=F32)
    if rope:
        qn = _rope(qn, cos_ref[...], sin_ref[...])
        kn = _rope(kn, cos_ref[:, 0:KV_WIDTH], sin_ref[:, 0:KV_WIDTH])
    q_ref[0] = (qn * SCORE_SCALE).astype(q_ref.dtype)
    k_ref[0] = kn.astype(k_ref.dtype)
    dt_ref[0] = jnp.dot(h, wd_ref[...], preferred_element_type=F32)
    gate_ref[0] = jnp.dot(h, wg_ref[...], preferred_element_type=F32)


def _in_projection(x, mods, layer, per_batch, norm_g, qg, kg, bm, rope_tabs, w):
    b, L, _ = x.shape
    tm = min(2 * TOKEN_TILE, L)
    rope = rope_tabs is not None
    kv_dtype = BF16 if rope else F32
    outs = ((POOL_WIDTH, F32), (ATTN_WIDTH, BF16), (KV_WIDTH, kv_dtype), (KV_WIDTH, kv_dtype), (SSD_INNER, F32),
            (CONV_CH, F32), (DT_PAD, F32), (N_BRANCH * D_MODEL, F32))
    tok = lambda n: pl.BlockSpec((1, tm, n), lambda bi, i: (bi, i, 0))
    tab = pl.BlockSpec((tm, ATTN_WIDTH), lambda bi, i: (i, 0))
    weights = [w["a"], w["z"], w["xbc"], w["dt"], w["gate"]]
    return pl.pallas_call(
        functools.partial(_inproj_kernel, rope=rope),
        grid=(b, L // tm),
        in_specs=[tok(D_MODEL), _mod_spec(layer, per_batch), _const_spec((1, D_MODEL)), _const_spec(qg.shape),
                  _const_spec(kg.shape), _const_spec(bm.shape)] + ([tab, tab] if rope else [])
                 + [_const_spec(a.shape) for a in weights],
        out_specs=[tok(n) for n, _ in outs],
        out_shape=[jax.ShapeDtypeStruct((b, L, n), dt) for n, dt in outs],
        compiler_params=_params("parallel", "parallel"),
        name="in_projection",
    )(x, mods, norm_g, qg, kg, bm, *(rope_tabs if rope else ()), *weights)


POOL_ROWS = 256
POOL_K = POOL_ROWS + LANES


def _pool_kernel(u_ref, band_ref, pw_ref, ps_ref, o_ref, *, L):
    halo_zeros = jnp.zeros((HALO, POOL_WIDTH), F32)
    tail_zeros = jnp.zeros((POOL_K - POOL_ROWS - 2 * HALO, POOL_WIDTH), F32)
    groups = [slice(gi * POOL_GROUP_W, (gi + 1) * POOL_GROUP_W) for gi in range(POOL_GROUPS)]
    for r0 in range(0, L, POOL_ROWS):
        main = u_ref[0, r0:r0 + POOL_ROWS, :]
        prev = u_ref[0, r0 - HALO:r0, :] if r0 > 0 else halo_zeros
        nxt = u_ref[0, r0 + POOL_ROWS:r0 + POOL_ROWS + HALO, :] if r0 + POOL_ROWS < L else halo_zeros
        rows = jnp.concatenate([prev, main, nxt, tail_zeros], axis=0)
        hi = rows.astype(BF16)
        lo = (rows - hi.astype(F32)).astype(BF16)
        t = r0 + lax.broadcasted_iota(jnp.int32, (POOL_ROWS, POOL_GROUP_W), 0)
        sums = [jnp.dot(band_ref[gi], hi[:, cols], preferred_element_type=F32)
                + jnp.dot(band_ref[gi], lo[:, cols], preferred_element_type=F32) for gi, cols in enumerate(groups)]
        pooled = []
        for w, cols, acc in zip(POOL_WINDOWS, groups, sums):
            cnt = (jnp.minimum(t + w // 2, L) - jnp.maximum(t - w // 2, 0)).astype(F32)
            pooled.append((acc / cnt - main[:, cols]).astype(BF16))
        mixed = [jnp.dot(pooled[gi], pw_ref[gi], preferred_element_type=F32) for gi in range(POOL_GROUPS)]
        for cols, m in zip(groups, mixed):
            o_ref[0, r0:r0 + POOL_ROWS, cols] = m * ps_ref[:, cols]


def _pool_bands():
    d = jnp.arange(POOL_K)[None, :] - HALO - jnp.arange(POOL_ROWS)[:, None]
    return jnp.stack([(d >= -(w // 2)) & (d < w // 2) for w in POOL_WINDOWS]).astype(BF16)


def _pool_mixer(u, pool_w, pool_scale):
    b, L, _ = u.shape
    assert max(POOL_WINDOWS) // 2 <= HALO and L % POOL_ROWS == 0
    seq = pl.BlockSpec((1, L, POOL_WIDTH), lambda bi: (bi, 0, 0))
    bands = _pool_bands()
    return pl.pallas_call(
        functools.partial(_pool_kernel, L=L),
        grid=(b,),
        in_specs=[seq, _const_spec(bands.shape), _const_spec(pool_w.shape), _const_spec(pool_scale.shape)],
        out_specs=seq,
        out_shape=jax.ShapeDtypeStruct((b, L, POOL_WIDTH), F32),
        compiler_params=_params("parallel"),
        name="pool_mixer",
    )(u, bands, pool_w, pool_scale)


def _head_rms(x, block_mean, g):
    sq = x * x
    hi = sq.astype(BF16)
    lo = (sq - hi.astype(F32)).astype(BF16)
    ms = jnp.concatenate(
        [jnp.dot(hi[:, c:c + LANES], block_mean, preferred_element_type=F32)
         + jnp.dot(lo[:, c:c + LANES], block_mean, preferred_element_type=F32)
         for c in range(0, x.shape[-1], LANES)], axis=-1)
    return x * lax.rsqrt(ms + EPS) * g


def _rope(x, cos, sin_signed):
    width = x.shape[-1]
    lane = lax.broadcasted_iota(jnp.int32, x.shape, 1)
    partner = jnp.where((lane & 16) == 0, pltpu.roll(x, width - 16, 1), pltpu.roll(x, 16, 1))
    return x * cos + partner * sin_signed


def _gqa_attention(q, keys, vals, sink_ref, rows, bias):
    kvs = [slice(j * HEAD_DIM, (j + 1) * HEAD_DIM) for j in range(N_KV_HEADS)]
    scores = []
    for j in range(N_KV_HEADS):
        qs = jnp.concatenate([q[:, (j * GQA_GROUP + g) * HEAD_DIM:(j * GQA_GROUP + g + 1) * HEAD_DIM]
                              for g in range(GQA_GROUP)], axis=0)
        scores.append(_bdot_nt(qs, keys[:, kvs[j]]))
    probs, denoms = [], []
    for j in range(N_KV_HEADS):
        p_rows = []
        for g in range(GQA_GROUP):
            s = scores[j][g * rows:(g + 1) * rows]
            if bias is not None:
                s = s + bias
            sink = sink_ref[j * GQA_GROUP + g] * LOG2E
            m = jnp.maximum(jnp.max(s, axis=-1, keepdims=True), sink)
            p = jnp.exp2(s - m)
            denoms.append(jnp.sum(p, axis=-1, keepdims=True) + jnp.exp2(sink - m))
            p_rows.append(p.astype(BF16))
        probs.append(jnp.concatenate(p_rows, axis=0))
    outs = []
    for j in range(N_KV_HEADS):
        o = jnp.dot(probs[j], vals[:, kvs[j]], preferred_element_type=F32)
        outs.extend(o[g * rows:(g + 1) * rows] / denoms[j * GQA_GROUP + g] for g in range(GQA_GROUP))
    return jnp.concatenate(outs, axis=-1)


def _ctx_attn_kernel(sink_ref, q_ref, k_ref, v_ref, o_ref):
    q = q_ref[0]
    o_ref[0] = _gqa_attention(q, k_ref[0].astype(BF16), v_ref[0].astype(BF16), sink_ref, q.shape[0], None)


def _context_attention(q, k, v, sink):
    b, L, _ = q.shape
    seq = lambda n: pl.BlockSpec((1, L, n), lambda bi: (bi, 0, 0))
    return pl.pallas_call(
        _ctx_attn_kernel,
        grid=(b,),
        in_specs=[pl.BlockSpec(memory_space=pltpu.SMEM), seq(ATTN_WIDTH), seq(KV_WIDTH), seq(KV_WIDTH)],
        out_specs=seq(ATTN_WIDTH),
        out_shape=jax.ShapeDtypeStruct((b, L, ATTN_WIDTH), F32),
        compiler_params=_params("parallel"),
        name="context_attention",
    )(sink, q, k, v)


Q_BLOCKS = 2
LOCAL_BLOCKS = Q_BLOCKS + 2


def _lat_attn_kernel(sink_ref, q_ref, *refs):
    k_refs, v_refs = refs[:LOCAL_BLOCKS], refs[LOCAL_BLOCKS:2 * LOCAL_BLOCKS]
    kc_ref, vc_ref, band_ref, o_ref = refs[2 * LOCAL_BLOCKS:]
    keys = jnp.concatenate([r[0] for r in k_refs] + [kc_ref[0, 0]], axis=0)
    vals = jnp.concatenate([r[0] for r in v_refs] + [vc_ref[0, 0]], axis=0)
    o_ref[0] = _gqa_attention(q_ref[0], keys, vals, sink_ref, Q_BLOCKS * BLOCK, band_ref[0])


def _latent_attention(q, k, v, cache_k, cache_v, layer, sink):
    b, L, _ = q.shape
    past = cache_k.shape[2]
    nb = L // BLOCK
    steps = nb // Q_BLOCKS
    assert nb % Q_BLOCKS == 0 and steps >= 2
    rows = lambda n: pl.BlockSpec((1, Q_BLOCKS * BLOCK, n), lambda bi, i: (bi, i, 0))
    local = [pl.BlockSpec((1, BLOCK, KV_WIDTH),
                          functools.partial(lambda bi, i, j: (bi, jnp.clip(Q_BLOCKS * i - 1 + j, 0, nb - 1), 0), j=j))
             for j in range(LOCAL_BLOCKS)]
    cache = pl.BlockSpec((1, 1, past, KV_WIDTH), lambda bi, i: (bi, layer, 0, 0))
    band_spec = pl.BlockSpec((1, Q_BLOCKS * BLOCK, LOCAL_BLOCKS * BLOCK + past),
                             lambda bi, i: (jnp.minimum(i, 1) + jnp.maximum(i - (steps - 2), 0), 0, 0))
    return pl.pallas_call(
        _lat_attn_kernel,
        grid=(b, steps),
        in_specs=[pl.BlockSpec(memory_space=pltpu.SMEM), rows(ATTN_WIDTH)] + local + local + [cache, cache, band_spec],
        out_specs=rows(ATTN_WIDTH),
        out_shape=jax.ShapeDtypeStruct((b, L, ATTN_WIDTH), F32),
        compiler_params=_params("parallel", "parallel"),
        name="latent_attention",
    )(sink, q, *([k] * LOCAL_BLOCKS), *([v] * LOCAL_BLOCKS), cache_k, cache_v, _band_bias(past))


def _band_bias(past):
    n_local = LOCAL_BLOCKS * BLOCK
    r = jnp.arange(Q_BLOCKS * BLOCK)[:, None]
    col = jnp.arange(n_local + past)[None, :]
    band = (jnp.abs(col - BLOCK - r) <= WINDOW) | (col >= n_local)
    first = band & (col >= BLOCK)
    last = band & ((col < n_local - BLOCK) | (col >= n_local))
    return jnp.where(jnp.stack([first, band, last]), 0.0, NEG_INF).astype(F32)


def _rope_tables(L):
    t = jnp.arange(L)
    row = (t // GRID_W).astype(F32)
    col = (t % GRID_W).astype(F32)
    nf = HEAD_DIM // 4
    inv = ROPE_BASE ** (-jnp.arange(nf, dtype=F32) / nf)
    ang_r = row[:, None] * inv[None, :]
    ang_c = col[:, None] * inv[None, :]
    cos = jnp.concatenate([jnp.cos(ang_r), jnp.cos(ang_r), jnp.cos(ang_c), jnp.cos(ang_c)], axis=-1)
    sin = jnp.concatenate([-jnp.sin(ang_r), jnp.sin(ang_r), -jnp.sin(ang_c), jnp.sin(ang_c)], axis=-1)
    return jnp.tile(cos, (1, N_Q_HEADS)), jnp.tile(sin, (1, N_Q_HEADS))


def _block_mean():
    i = jnp.arange(LANES) // HEAD_DIM
    return jnp.where(i[:, None] == i[None, :], 1.0 / HEAD_DIM, 0.0).astype(BF16)


SHIFT_K = 2 * LANES

def _ssd_conv_chunk(xbc_ref, xp_ref, xn_ref, cw_ref, cb_ref, shift_ref, has_prev, has_next):
    main = xbc_ref[0]
    rows = jnp.concatenate([jnp.where(has_prev, xp_ref[0], 0.0), main, jnp.where(has_next, xn_ref[0], 0.0),
                            jnp.zeros((SHIFT_K - CHUNK - 2 * HALO, CONV_CH), F32)], axis=0).astype(BF16)
    taps = jnp.dot(shift_ref[...], rows, preferred_element_type=F32)
    acc = cb_ref[...] + main * cw_ref[CONV_W // 2:CONV_W // 2 + 1, :]
    for i, kk in enumerate(k for k in range(CONV_W) if k != CONV_W // 2):
        acc = acc + taps[i * CHUNK:(i + 1) * CHUNK] * cw_ref[kk:kk + 1, :]
    return _silu(acc)


def _split3(x):
    hi = x.astype(BF16)
    rest = x - hi.astype(F32)
    mid = rest.astype(BF16)
    return hi, mid, (rest - mid.astype(F32)).astype(BF16)


SSD_END_ROWS = (CHUNK - 1, 0)


def _ssd_decay_table_stages(dtt_refs, tri_ref, biast_ref, alogt_ref, tabt_ref, terms_ref):
    steps = []
    for d, dtt_ref in enumerate(dtt_refs):
        dtt = _softplus(dtt_ref[0, 0] + biast_ref[d])
        steps.append((dtt, _split3(dtt * (-jnp.exp(alogt_ref[d]) * LOG2E))))
    yield
    sums = []
    for d, (_, parts_t) in enumerate(steps):
        sums.append(sum(jnp.dot(part, tri_ref[1 - d], preferred_element_type=F32) for part in parts_t))
    yield
    pad = jnp.zeros((CHUNK - 3 * SSD_HEADS, CHUNK), F32)
    terms = [jnp.concatenate([p.astype(F32) for p in _split3(acum_t)] + [pad], axis=0).T.astype(BF16)
             for acum_t in sums]
    yield
    for d, ((dtt, _), acum_t) in enumerate(zip(steps, sums)):
        last_t = acum_t[:, SSD_END_ROWS[d]:SSD_END_ROWS[d] + 1]
        terms_ref[d] = terms[d]
        tabt_ref[d, 0] = acum_t
        tabt_ref[d, 1] = dtt
        tabt_ref[d, 2] = jnp.exp2(acum_t)
        tabt_ref[d, 3] = jnp.exp2(last_t - acum_t) * dtt


def _ssd_load_decay_tables(operands, tri_ref, spread_ref, tabt_ref, terms_ref):
    a_cols = jnp.dot(terms_ref[...].reshape(2 * CHUNK, CHUNK), spread_ref[...], preferred_element_type=F32)
    tables = []
    for d, (xs_t, bc) in enumerate(operands):
        acum_t = tabt_ref[d, 0]
        end = SSD_END_ROWS[d]
        tables.append({"bc": bc, "xs_t": xs_t, "mask": tri_ref[d] > 0, "acum_t": acum_t,
                       "a_col": a_cols[d * CHUNK:(d + 1) * CHUNK],
                       "dt_t": tabt_ref[d, 1], "decay_in_t": tabt_ref[d, 2], "coef_t": tabt_ref[d, 3],
                       "chunk_decay_t": jnp.exp2(acum_t[:, end:end + 1])})
    return tables


BACKGROUND_POINTS = ((0, 0), (0, 2), (1, 0))


def _ssd_scan_chunks(scans, background):
    zero = jnp.zeros((SSD_HEADDIM, CHUNK), BF16)
    ys = [[] for _ in scans]
    next(background, None)
    for g in range(SSD_GROUPS):
        hs = slice(g * HEADS_PER_GROUP * SSD_HEADDIM, (g + 1) * HEADS_PER_GROUP * SSD_HEADDIM)
        heads = range(g * HEADS_PER_GROUP, (g + 1) * HEADS_PER_GROUP)
        stage = []
        for t, s_ref in scans:
            bm = t["bc"][:, g * D_STATE:(g + 1) * D_STATE].astype(BF16)
            cm = t["bc"][:, (SSD_GROUPS + g) * D_STATE:(SSD_GROUPS + g + 1) * D_STATE].astype(BF16)
            cbm = jnp.where(t["mask"], _bdot_nt(cm, bm), 0.0)
            state = s_ref[hs, :]
            y_off = _bdot_nt(state, cm)
            y_off = jnp.concatenate(
                [y_off[(h - heads[0]) * SSD_HEADDIM:(h - heads[0] + 1) * SSD_HEADDIM] * t["decay_in_t"][h:h + 1, :]
                 for h in heads], axis=0)
            stage.append((bm, cbm, state, y_off, []))
        for pair in range(HEADS_PER_GROUP // 2):
            if (g, pair) in BACKGROUND_POINTS:
                next(background, None)
            for (t, _), (_, cbm, _, _, y_diag) in zip(scans, stage):
                ws, xs = [], []
                for h in (heads[0] + 2 * pair, heads[0] + 2 * pair + 1):
                    a_col = t["a_col"][:, h * CHUNK:(h + 1) * CHUNK]
                    ws.append((cbm * jnp.exp2(jnp.minimum(a_col - t["acum_t"][h:h + 1, :], 0.0))).astype(BF16))
                    xs.append((t["xs_t"][h * SSD_HEADDIM:(h + 1) * SSD_HEADDIM, :] * t["dt_t"][h:h + 1, :]).astype(BF16))
                lhs = jnp.concatenate([jnp.concatenate([xs[0], zero], axis=1),
                                       jnp.concatenate([zero, xs[1]], axis=1)], axis=0)
                y_diag.append(_bdot_nt(lhs, jnp.concatenate(ws, axis=1)))
        for i, ((t, s_ref), (bm, _, state, y_off, y_diag)) in enumerate(zip(scans, stage)):
            ys[i].append(jnp.concatenate(y_diag, axis=0) + y_off)
            x_state = jnp.concatenate(
                [(t["xs_t"][h * SSD_HEADDIM:(h + 1) * SSD_HEADDIM, :] * t["coef_t"][h:h + 1, :]).astype(BF16)
                 for h in heads], axis=0)
            update = jnp.dot(x_state, bm, preferred_element_type=F32)
            for h in heads:
                hp = slice(h * SSD_HEADDIM, (h + 1) * SSD_HEADDIM)
                rel = slice((h - heads[0]) * SSD_HEADDIM, (h - heads[0] + 1) * SSD_HEADDIM)
                s_ref[hp, :] = state[rel] * t["chunk_decay_t"][h:h + 1, :] + update[rel]
    return [jnp.concatenate(y, axis=0) for y in ys]


def _ssd_kernel(xf_ref, xfp_ref, xfn_ref, xb_ref, xbp_ref, xbn_ref, zf_ref, zb_ref, dtt0f_ref, dtt0b_ref,
                dttnf_ref, dttnb_ref, init_ref, tri_ref, spread_ref, shift_ref, cw_ref, cb_ref, biast_ref,
                alogt_ref, dskip_ref, ng_ref, o_ref, fin_ref, xst_ref, bc_ref, sf_ref, sb_ref, tabt_ref, terms_ref,
                *, nc, has_init):
    s = pl.program_id(1)
    half = nc // 2
    first_half = s < half
    cf = s
    cb = nc - 1 - s
    rf = pl.multiple_of(cf * CHUNK, CHUNK)
    rb = pl.multiple_of(cb * CHUNK, CHUNK)

    @pl.when(first_half)
    def _():
        for c, rows, refs in ((cf, rf, (xf_ref, xfp_ref, xfn_ref)), (cb, rb, (xb_ref, xbp_ref, xbn_ref))):
            xc = _ssd_conv_chunk(*refs, cw_ref, cb_ref, shift_ref, c > 0, c < nc - 1)
            xst_ref[c] = xc[:, 0:SSD_INNER].T
            bc_ref[pl.ds(rows, CHUNK), :] = xc[:, SSD_INNER:]

    table_refs = (tri_ref, biast_ref, alogt_ref, tabt_ref, terms_ref)

    @pl.when(s == 0)
    def _():
        if has_init:
            sf_ref[...] = init_ref[0, 0, 0]
            sb_ref[...] = init_ref[0, 0, 1]
        else:
            sf_ref[...] = jnp.zeros(sf_ref.shape, F32)
            sb_ref[...] = jnp.zeros(sb_ref.shape, F32)
        for _ in _ssd_decay_table_stages((dtt0f_ref, dtt0b_ref), *table_refs):
            pass

    tf, tb = _ssd_load_decay_tables([(xst_ref[cf], bc_ref[pl.ds(rf, CHUNK), :]),
                                     (xst_ref[cb], bc_ref[pl.ds(rb, CHUNK), :])],
                                    tri_ref, spread_ref, tabt_ref, terms_ref)
    next_tables = _ssd_decay_table_stages((dttnf_ref, dttnb_ref), *table_refs)
    yf_t, yb_t = _ssd_scan_chunks([(tf, sf_ref), (tb, sb_ref)], next_tables)
    for _ in next_tables:
        pass
    yf = (yf_t + dskip_ref[...] * tf["xs_t"]).T
    yb = yb_t.T

    @pl.when(first_half)
    def _():
        o_ref[0, pl.ds(rf, CHUNK), :] = yf
        o_ref[0, pl.ds(rb, CHUNK), :] = yb

    def finish(rows, y, z):
        tot = (o_ref[0, pl.ds(rows, CHUNK), :] + y) * _silu(z)
        ms = jnp.mean(tot * tot, axis=-1, keepdims=True)
        o_ref[0, pl.ds(rows, CHUNK), :] = tot * lax.rsqrt(ms + EPS) * ng_ref[...]

    @pl.when(jnp.logical_not(first_half))
    def _():
        finish(rf, yf, zf_ref[0])
        finish(rb, yb, zb_ref[0])

    @pl.when(s == nc - 1)
    def _():
        fin_ref[0, 0] = sf_ref[...]
        fin_ref[0, 1] = sb_ref[...]


def _ssd_mixer(z, xbc, dt_raw, init_state, layer, consts, p):
    b, L, _ = z.shape
    nc = L // CHUNK
    has_init = init_state is not None
    dt_t = jnp.stack([dt_raw[:, :, 0:SSD_HEADS], dt_raw[:, :, SSD_HEADS:2 * SSD_HEADS]], axis=1)
    dt_t = jnp.swapaxes(dt_t, 2, 3)
    assert nc % 2 == 0
    half = nc // 2
    if not has_init:
        init_state = jnp.zeros((1, 1, 1, HALO, LANES), F32)
        init_spec = pl.BlockSpec((1, 1, 1, HALO, LANES), lambda bi, s: (0, 0, 0, 0, 0))
    else:
        init_spec = pl.BlockSpec((1, 1, 2, SSD_INNER, D_STATE), lambda bi, s: (bi, layer, 0, 0, 0))
    per = CHUNK // HALO
    last = L // HALO - 1
    conv_f = lambda s: jnp.minimum(s, half - 1)
    conv_b = lambda s: jnp.maximum(nc - 1 - s, half)
    fin_f = lambda s: jnp.maximum(s, half)
    fin_b = lambda s: jnp.minimum(nc - 1 - s, half - 1)

    def conv_specs(chunk):
        return [pl.BlockSpec((1, CHUNK, CONV_CH), lambda bi, s: (bi, chunk(s), 0)),
                pl.BlockSpec((1, HALO, CONV_CH), lambda bi, s: (bi, jnp.maximum(chunk(s) * per - 1, 0), 0)),
                pl.BlockSpec((1, HALO, CONV_CH), lambda bi, s: (bi, jnp.minimum((chunk(s) + 1) * per, last), 0))]

    in_specs = conv_specs(conv_f) + conv_specs(conv_b) + [
        pl.BlockSpec((1, CHUNK, SSD_INNER), lambda bi, s: (bi, fin_f(s), 0)),
        pl.BlockSpec((1, CHUNK, SSD_INNER), lambda bi, s: (bi, fin_b(s), 0)),
        pl.BlockSpec((1, 1, SSD_HEADS, CHUNK), lambda bi, s: (bi, 0, 0, 0)),
        pl.BlockSpec((1, 1, SSD_HEADS, CHUNK), lambda bi, s: (bi, 1, 0, nc - 1)),
        pl.BlockSpec((1, 1, SSD_HEADS, CHUNK), lambda bi, s: (bi, 0, 0, jnp.minimum(s + 1, nc - 1))),
        pl.BlockSpec((1, 1, SSD_HEADS, CHUNK), lambda bi, s: (bi, 1, 0, jnp.maximum(nc - 2 - s, 0))),
        init_spec,
        _const_spec((2, CHUNK, CHUNK)), _const_spec((CHUNK, SSD_HEADS * CHUNK)),
        _const_spec(((CONV_W - 1) * CHUNK, SHIFT_K)), _const_spec((CONV_W, CONV_CH)), _const_spec((1, CONV_CH)),
        _const_spec((2, SSD_HEADS, 1)), _const_spec((2, SSD_HEADS, 1)), _const_spec((SSD_INNER, CHUNK)),
        _const_spec((1, SSD_INNER)),
    ]
    out, fin = pl.pallas_call(
        functools.partial(_ssd_kernel, nc=nc, has_init=has_init),
        grid=(b, nc),
        in_specs=in_specs,
        out_specs=[pl.BlockSpec((1, L, SSD_INNER), lambda bi, s: (bi, 0, 0)),
                   pl.BlockSpec((1, 2, SSD_INNER, D_STATE), lambda bi, s: (bi, 0, 0, 0))],
        out_shape=[jax.ShapeDtypeStruct((b, L, SSD_INNER), F32),
                   jax.ShapeDtypeStruct((b, 2, SSD_INNER, D_STATE), F32)],
        scratch_shapes=[pltpu.VMEM((nc, SSD_INNER, CHUNK), F32), pltpu.VMEM((L, CONV_CH - SSD_INNER), F32),
                        pltpu.VMEM((SSD_INNER, D_STATE), F32), pltpu.VMEM((SSD_INNER, D_STATE), F32),
                        pltpu.VMEM((2, 4, SSD_HEADS, CHUNK), F32), pltpu.VMEM((2, CHUNK, CHUNK), BF16)],
        compiler_params=_params("parallel", "arbitrary"),
        name="ssd_mixer",
    )(xbc, xbc, xbc, xbc, xbc, xbc, z, z, dt_t, dt_t, dt_t, dt_t, init_state, consts["tri"], consts["spread"],
      consts["shift"],
      p["conv_w"], p["conv_b"], p["dt_bias_t"], p["a_log_t"], p["d_skip"], p["ssd_norm_g"])
    return out, fin


def _ssd_consts():
    i = jnp.arange(CHUNK)
    lower = (i[None, :] <= i[:, None]).astype(F32)
    t = jnp.arange(CHUNK)[:, None]
    col = jnp.arange(SHIFT_K)[None, :]
    shift = jnp.concatenate([(col == HALO + t + kk - CONV_W // 2) for kk in range(CONV_W) if kk != CONV_W // 2], axis=0)
    k = jnp.arange(CHUNK)[:, None]
    lane_head = jnp.arange(SSD_HEADS * CHUNK)[None, :] // CHUNK
    spread = (k < 3 * SSD_HEADS) & (k % SSD_HEADS == lane_head)
    return {"tri": jnp.stack([lower, lower.T]).astype(BF16), "shift": shift.astype(BF16),
            "spread": spread.astype(BF16)}


FF_CHUNK = 1024


def _merge_mlp_kernel(x_ref, mod_ref, pool_ref, attn_ref, ssd_ref, gate_ref, g_ref,
                      wp_ref, wa_ref, ws_ref, wo_ref, w1_ref, w2_ref, o_ref):
    mod = mod_ref[0]
    gates = _sigmoid(gate_ref[0])
    merged = (gates[:, 0:D_MODEL] * _bdot(pool_ref[0], wp_ref[...])
              + gates[:, D_MODEL:2 * D_MODEL] * _bdot(attn_ref[0], wa_ref[...])
              + gates[:, 2 * D_MODEL:] * _bdot(ssd_ref[0], ws_ref[...]))
    x = x_ref[0] + mod[:, 2 * D_MODEL:3 * D_MODEL] * _bdot(merged, wo_ref[...])
    h = _rms_mod(x, g_ref[...], mod[:, 4 * D_MODEL:5 * D_MODEL], mod[:, 3 * D_MODEL:4 * D_MODEL]).astype(BF16)
    acc = jnp.zeros(x.shape, F32)
    for j in range(D_FF // FF_CHUNK):
        ff = slice(j * FF_CHUNK, (j + 1) * FF_CHUNK)
        a = jnp.dot(h, w1_ref[:, ff], preferred_element_type=F32)
        acc = acc + _bdot(jnp.square(jnp.maximum(a, 0.0)), w2_ref[ff, :])
    o_ref[0] = x + mod[:, 5 * D_MODEL:] * acc


def _merge_mlp(x, mods, layer, per_batch, pool, attn, ssd, gate, norm_g, w):
    b, L, _ = x.shape
    tm = TOKEN_TILE
    tok = lambda n: pl.BlockSpec((1, tm, n), lambda bi, i: (bi, i, 0))
    weights = [w["pool_o"], w["attn_o"], w["ssd_o"], w["out"], w["mlp1"], w["mlp2"]]
    return pl.pallas_call(
        _merge_mlp_kernel,
        grid=(b, L // tm),
        in_specs=[tok(D_MODEL), _mod_spec(layer, per_batch), tok(POOL_WIDTH), tok(ATTN_WIDTH), tok(SSD_INNER),
                  tok(N_BRANCH * D_MODEL), _const_spec((1, D_MODEL))] + [_const_spec(a.shape) for a in weights],
        out_specs=tok(D_MODEL),
        out_shape=jax.ShapeDtypeStruct((b, L, D_MODEL), F32),
        compiler_params=_params("parallel", "parallel"),
        name="merge_mlp",
    )(x, mods, pool, attn, ssd, gate, norm_g, *weights)


def _layer(x, mods, layer, per_batch, w, p, consts, ctx):
    u, q, k, v, z, xbc, dt_raw, gate = _in_projection(
        x, mods, layer, per_batch, p["norm1_g"], p["q_norm_g"], p["k_norm_g"], consts["bm"],
        None if ctx is None else consts["rope"], w)
    pool = _pool_mixer(u, w["pool_w"], p["pool_scale"])
    if ctx is None:
        attn = _context_attention(q, k, v, p["attn_sink"])
        init = None
    else:
        cache_k, cache_v, init = ctx
        attn = _latent_attention(q, k, v, cache_k, cache_v, layer, p["attn_sink"])
    ssd, fin = _ssd_mixer(z, xbc, dt_raw, init, layer, consts, p)
    x = _merge_mlp(x, mods, layer, per_batch, pool, attn, ssd, gate, p["norm2_g"], w)
    return x, (k, v, fin)


def _layer_weights(l, w_in, pool_w, w_pool_o, w_attn_o, w_ssd_o, w_out, w_mlp1, w_mlp2):
    wi = w_in[l]
    o_z = POOL_WIDTH + ATTN_WIDTH + 2 * KV_WIDTH
    o_x = o_z + SSD_INNER
    o_d = o_x + CONV_CH
    o_g = o_d + 2 * SSD_HEADS
    wd = jnp.pad(wi[:, o_d:o_g], ((0, 0), (0, DT_PAD - 2 * SSD_HEADS)))
    cast = lambda a: a.astype(BF16)
    return {"a": cast(wi[:, :o_z]), "z": cast(wi[:, o_z:o_x]), "xbc": cast(wi[:, o_x:o_d]), "dt": cast(wd),
            "gate": cast(wi[:, o_g:]), "pool_w": cast(pool_w[l]), "pool_o": cast(w_pool_o[l]),
            "attn_o": cast(w_attn_o[l]), "ssd_o": cast(w_ssd_o[l]), "out": cast(w_out[l]),
            "mlp1": cast(w_mlp1[l]), "mlp2": cast(w_mlp2[l])}


def kernel(x_prompt, x_sample, cache_k, cache_v, state_ssd, c, c_ctx, w_mod, b_mod, norm1_g, norm2_g, w_in, pool_w, pool_scale, w_pool_o, q_norm_g, k_norm_g, attn_sink, w_attn_o, conv_w, conv_b, dt_bias, a_log, d_skip, ssd_norm_g, w_ssd_o, w_out, w_mlp1, w_mlp2):
    batch, seq, _ = x_prompt.shape
    dec_batch, dec_seq, _ = x_sample.shape
    past = cache_k.shape[2]
    assert 1 + dec_batch <= MOD_ROWS

    cvecs = jnp.concatenate([c_ctx[None, :], c, jnp.zeros((MOD_ROWS - 1 - dec_batch, D_MODEL), F32)], axis=0)
    mods = _modulation(cvecs, w_mod, b_mod)

    consts = {"bm": _block_mean(), "rope": _rope_tables(dec_seq)}
    consts.update(_ssd_consts())
    cache_k = cache_k.reshape(dec_batch, DEPTH, past, KV_WIDTH).astype(BF16)
    cache_v = cache_v.reshape(dec_batch, DEPTH, past, KV_WIDTH).astype(BF16)
    state = state_ssd.reshape(dec_batch, DEPTH, 2, SSD_INNER, D_STATE)

    y_prompt, y_sample = x_prompt, x_sample
    ks, vs, ss = [], [], []
    for l in range(DEPTH):
        w = _layer_weights(l, w_in, pool_w, w_pool_o, w_attn_o, w_ssd_o, w_out, w_mlp1, w_mlp2)
        p = {"norm1_g": norm1_g[l][None, :], "norm2_g": norm2_g[l][None, :], "pool_scale": pool_scale[l][None, :],
             "q_norm_g": jnp.tile(q_norm_g[l], N_Q_HEADS)[None, :], "k_norm_g": jnp.tile(k_norm_g[l], N_KV_HEADS)[None, :],
             "attn_sink": attn_sink[l], "conv_w": conv_w[l], "conv_b": conv_b[l][None, :],
             "dt_bias_t": dt_bias[l][:, :, None], "a_log_t": a_log[l][:, :, None],
             "d_skip": jnp.broadcast_to(jnp.repeat(d_skip[l], SSD_HEADDIM)[:, None], (SSD_INNER, CHUNK)),
             "ssd_norm_g": ssd_norm_g[l][None, :]}
        y_prompt, (k_l, v_l, s_l) = _layer(y_prompt, mods, l, False, w, p, consts, None)
        ks.append(k_l.reshape(batch, seq, N_KV_HEADS, HEAD_DIM))
        vs.append(v_l.reshape(batch, seq, N_KV_HEADS, HEAD_DIM))
        ss.append(s_l.reshape(batch, 2, SSD_HEADS, SSD_HEADDIM, D_STATE))
        y_sample, _ = _layer(y_sample, mods, l, True, w, p, consts, (cache_k, cache_v, state))
    return (y_prompt, y_sample, jnp.stack(ks, axis=1), jnp.stack(vs, axis=1), jnp.stack(ss, axis=1))
```

```python
import functools
import math

import jax
import jax.numpy as jnp
from jax import lax
from jax.experimental import pallas as pl
from jax.experimental.pallas import tpu as pltpu

F32 = jnp.float32
BF16 = jnp.bfloat16

D_MODEL = 1024
DEPTH = 2
GRID_W = 64
MOD_CHUNKS = 6
EPS = 1e-6
POOL_WIDTH = D_MODEL // 2
POOL_GROUPS = 4
POOL_GROUP_W = POOL_WIDTH // POOL_GROUPS
POOL_WINDOWS = (2, 4, 8, 16)
N_Q_HEADS = 8
N_KV_HEADS = 2
GQA_GROUP = N_Q_HEADS // N_KV_HEADS
HEAD_DIM = 64
ATTN_WIDTH = N_Q_HEADS * HEAD_DIM
KV_WIDTH = N_KV_HEADS * HEAD_DIM
WINDOW = 128
BLOCK = 128
ROPE_BASE = 10000.0
ATTN_SCALE = HEAD_DIM ** -0.5
SSD_HEADS = 16
SSD_HEADDIM = 64
SSD_INNER = SSD_HEADS * SSD_HEADDIM
SSD_GROUPS = 2
HEADS_PER_GROUP = SSD_HEADS // SSD_GROUPS
D_STATE = 128
CONV_W = 5
CHUNK = 128
CONV_CH = SSD_INNER + 2 * SSD_GROUPS * D_STATE
N_BRANCH = 3
D_FF = 4 * D_MODEL

LANES = 128
HALO = 8
MOD_ROWS = 16
VMEM_LIMIT = 56 * 1024 * 1024
TOKEN_TILE = 256
NEG_INF = float("-inf")
LOG2E = math.log2(math.e)
SCORE_SCALE = ATTN_SCALE * LOG2E


def _params(*sem):
    return pltpu.CompilerParams(dimension_semantics=sem, vmem_limit_bytes=VMEM_LIMIT)


def _const_spec(shape):
    nd = len(shape)
    return pl.BlockSpec(shape, lambda *_: (0,) * nd, pipeline_mode=pl.Buffered(1))


def _bdot(a, b):
    return jnp.dot(a.astype(BF16), b.astype(BF16), preferred_element_type=F32)


def _bdot_nt(a, b):
    return lax.dot_general(a.astype(BF16), b.astype(BF16), (((1,), (1,)), ((), ())),
                           preferred_element_type=F32)


def _sigmoid(x):
    return 0.5 * jnp.tanh(0.5 * x) + 0.5


def _silu(x):
    half = 0.5 * x
    return half * jnp.tanh(half) + half


def _softplus(x):
    return jnp.maximum(x, 0.0) + jnp.log1p(jnp.exp(-jnp.abs(x)))


def _rms_mod(x, g, scale, shift):
    ms = jnp.mean(x * x, axis=-1, keepdims=True)
    return (x * lax.rsqrt(ms + EPS)) * g * (1.0 + scale) + shift


def _mod_kernel(c_ref, w_ref, b_ref, o_ref):
    o_ref[0] = _bdot(_silu(c_ref[...]), w_ref[0]) + b_ref[0]


def _modulation(cvecs, w_mod, b_mod):
    n = MOD_CHUNKS * D_MODEL
    tn = n // 4
    out = pl.pallas_call(
        _mod_kernel,
        grid=(DEPTH, n // tn),
        in_specs=[pl.BlockSpec((MOD_ROWS, D_MODEL), lambda l, j: (0, 0)),
                  pl.BlockSpec((1, D_MODEL, tn), lambda l, j: (l, 0, j)),
                  pl.BlockSpec((1, 1, tn), lambda l, j: (l, 0, j))],
        out_specs=pl.BlockSpec((1, MOD_ROWS, tn), lambda l, j: (l, 0, j)),
        out_shape=jax.ShapeDtypeStruct((DEPTH, MOD_ROWS, n), F32),
        compiler_params=_params("arbitrary", "arbitrary"),
        name="modulation",
    )(cvecs, w_mod, b_mod.reshape(DEPTH, 1, n))
    return out.reshape(DEPTH * MOD_ROWS, 1, n)


def _mod_spec(layer, per_batch):
    base = layer * MOD_ROWS
    if per_batch:
        return pl.BlockSpec((1, 1, MOD_CHUNKS * D_MODEL), lambda b, i: (base + 1 + b, 0, 0))
    return pl.BlockSpec((1, 1, MOD_CHUNKS * D_MODEL), lambda b, i: (base, 0, 0))


DT_PAD = LANES


def _inproj_kernel(*refs, rope):
    x_ref, mod_ref, g_ref, qg_ref, kg_ref, bm_ref = refs[:6]
    refs = refs[6:]
    if rope:
        cos_ref, sin_ref = refs[:2]
        refs = refs[2:]
    wa_ref, wz_ref, wx_ref, wd_ref, wg_ref, u_ref, q_ref, k_ref, v_ref, z_ref, xbc_ref, dt_ref, gate_ref = refs
    mod = mod_ref[0]
    h = _rms_mod(x_ref[0], g_ref[...], mod[:, D_MODEL:2 * D_MODEL], mod[:, 0:D_MODEL]).astype(BF16)
    a = jnp.dot(h, wa_ref[...], preferred_element_type=F32)
    u_ref[0] = a[:, 0:POOL_WIDTH]
    v_ref[0] = a[:, POOL_WIDTH + ATTN_WIDTH + KV_WIDTH:].astype(v_ref.dtype)
    z_ref[0] = jnp.dot(h, wz_ref[...], preferred_element_type=F32)
    qn = _head_rms(a[:, POOL_WIDTH:POOL_WIDTH + ATTN_WIDTH], bm_ref[...], qg_ref[...])
    kn = _head_rms(a[:, POOL_WIDTH + ATTN_WIDTH:POOL_WIDTH + ATTN_WIDTH + KV_WIDTH], bm_ref[...], kg_ref[...])
    xbc_ref[0] = jnp.dot(h, wx_ref[...], preferred_element_type=F32)
    if rope:
        qn = _rope(qn, cos_ref[...], sin_ref[...])
        kn = _rope(kn, cos_ref[:, 0:KV_WIDTH], sin_ref[:, 0:KV_WIDTH])
    q_ref[0] = (qn * SCORE_SCALE).astype(q_ref.dtype)
    k_ref[0] = kn.astype(k_ref.dtype)
    dt_ref[0] = jnp.dot(h, wd_ref[...], preferred_element_type=F32)
    gate_ref[0] = jnp.dot(h, wg_ref[...], preferred_element_type=F32)


def _in_projection(x, mods, layer, per_batch, norm_g, qg, kg, bm, rope_tabs, w):
    b, L, _ = x.shape
    tm = min(2 * TOKEN_TILE, L)
    rope = rope_tabs is not None
    kv_dtype = BF16 if rope else F32
    outs = ((POOL_WIDTH, F32), (ATTN_WIDTH, BF16), (KV_WIDTH, kv_dtype), (KV_WIDTH, kv_dtype), (SSD_INNER, F32),
            (CONV_CH, F32), (DT_PAD, F32), (N_BRANCH * D_MODEL, F32))
    tok = lambda n: pl.BlockSpec((1, tm, n), lambda bi, i: (bi, i, 0))
    tab = pl.BlockSpec((tm, ATTN_WIDTH), lambda bi, i: (i, 0))
    weights = [w["a"], w["z"], w["xbc"], w["dt"], w["gate"]]
    return pl.pallas_call(
        functools.partial(_inproj_kernel, rope=rope),
        grid=(b, L // tm),
        in_specs=[tok(D_MODEL), _mod_spec(layer, per_batch), _const_spec((1, D_MODEL)), _const_spec(qg.shape),
                  _const_spec(kg.shape), _const_spec(bm.shape)] + ([tab, tab] if rope else [])
                 + [_const_spec(a.shape) for a in weights],
        out_specs=[tok(n) for n, _ in outs],
        out_shape=[jax.ShapeDtypeStruct((b, L, n), dt) for n, dt in outs],
        compiler_params=_params("parallel", "parallel"),
        name="in_projection",
    )(x, mods, norm_g, qg, kg, bm, *(rope_tabs if rope else ()), *weights)


POOL_ROWS = 256
POOL_K = POOL_ROWS + LANES


def _pool_kernel(u_ref, band_ref, pw_ref, ps_ref, o_ref, *, L):
    halo_zeros = jnp.zeros((HALO, POOL_WIDTH), F32)
    tail_zeros = jnp.zeros((POOL_K - POOL_ROWS - 2 * HALO, POOL_WIDTH), F32)
    groups = [slice(gi * POOL_GROUP_W, (gi + 1) * POOL_GROUP_W) for gi in range(POOL_GROUPS)]
    for r0 in range(0, L, POOL_ROWS):
        main = u_ref[0, r0:r0 + POOL_ROWS, :]
        prev = u_ref[0, r0 - HALO:r0, :] if r0 > 0 else halo_zeros
        nxt = u_ref[0, r0 + POOL_ROWS:r0 + POOL_ROWS + HALO, :] if r0 + POOL_ROWS < L else halo_zeros
        rows = jnp.concatenate([prev, main, nxt, tail_zeros], axis=0)
        hi = rows.astype(BF16)
        lo = (rows - hi.astype(F32)).astype(BF16)
        t = r0 + lax.broadcasted_iota(jnp.int32, (POOL_ROWS, POOL_GROUP_W), 0)
        sums = [jnp.dot(band_ref[gi], hi[:, cols], preferred_element_type=F32)
                + jnp.dot(band_ref[gi], lo[:, cols], preferred_element_type=F32) for gi, cols in enumerate(groups)]
        pooled = []
        for w, cols, acc in zip(POOL_WINDOWS, groups, sums):
            cnt = (jnp.minimum(t + w // 2, L) - jnp.maximum(t - w // 2, 0)).astype(F32)
            pooled.append((acc / cnt - main[:, cols]).astype(BF16))
        mixed = [jnp.dot(pooled[gi], pw_ref[gi], preferred_element_type=F32) for gi in range(POOL_GROUPS)]
        for cols, m in zip(groups, mixed):
            o_ref[0, r0:r0 + POOL_ROWS, cols] = m * ps_ref[:, cols]


def _pool_bands():
    d = jnp.arange(POOL_K)[None, :] - HALO - jnp.arange(POOL_ROWS)[:, None]
    return jnp.stack([(d >= -(w // 2)) & (d < w // 2) for w in POOL_WINDOWS]).astype(BF16)


def _pool_mixer(u, pool_w, pool_scale):
    b, L, _ = u.shape
    assert max(POOL_WINDOWS) // 2 <= HALO and L % POOL_ROWS == 0
    seq = pl.BlockSpec((1, L, POOL_WIDTH), lambda bi: (bi, 0, 0))
    bands = _pool_bands()
    return pl.pallas_call(
        functools.partial(_pool_kernel, L=L),
        grid=(b,),
        in_specs=[seq, _const_spec(bands.shape), _const_spec(pool_w.shape), _const_spec(pool_scale.shape)],
        out_specs=seq,
        out_shape=jax.ShapeDtypeStruct((b, L, POOL_WIDTH), F32),
        compiler_params=_params("parallel"),
        name="pool_mixer",
    )(u, bands, pool_w, pool_scale)


def _head_rms(x, block_mean, g):
    sq = x * x
    hi = sq.astype(BF16)
    lo = (sq - hi.astype(F32)).astype(BF16)
    ms = jnp.concatenate(
        [jnp.dot(hi[:, c:c + LANES], block_mean, preferred_element_type=F32)
         + jnp.dot(lo[:, c:c + LANES], block_mean, preferred_element_type=F32)
         for c in range(0, x.shape[-1], LANES)], axis=-1)
    return x * lax.rsqrt(ms + EPS) * g


def _rope(x, cos, sin_signed):
    width = x.shape[-1]
    lane = lax.broadcasted_iota(jnp.int32, x.shape, 1)
    partner = jnp.where((lane & 16) == 0, pltpu.roll(x, width - 16, 1), pltpu.roll(x, 16, 1))
    return x * cos + partner * sin_signed


def _gqa_attention(q, keys, vals, sink_ref, rows, bias):
    kvs = [slice(j * HEAD_DIM, (j + 1) * HEAD_DIM) for j in range(N_KV_HEADS)]
    scores = []
    for j in range(N_KV_HEADS):
        qs = jnp.concatenate([q[:, (j * GQA_GROUP + g) * HEAD_DIM:(j * GQA_GROUP + g + 1) * HEAD_DIM]
                              for g in range(GQA_GROUP)], axis=0)
        scores.append(_bdot_nt(qs, keys[:, kvs[j]]))
    probs, denoms = [], []
    for j in range(N_KV_HEADS):
        p_rows = []
        for g in range(GQA_GROUP):
            s = scores[j][g * rows:(g + 1) * rows]
            if bias is not None:
                s = s + bias
            sink = sink_ref[j * GQA_GROUP + g] * LOG2E
            m = jnp.maximum(jnp.max(s, axis=-1, keepdims=True), sink)
            p = jnp.exp2(s - m)
            denoms.append(jnp.sum(p, axis=-1, keepdims=True) + jnp.exp2(sink - m))
            p_rows.append(p.astype(BF16))
        probs.append(jnp.concatenate(p_rows, axis=0))
    outs = []
    for j in range(N_KV_HEADS):
        o = jnp.dot(probs[j], vals[:, kvs[j]], preferred_element_type=F32)
        outs.extend(o[g * rows:(g + 1) * rows] / denoms[j * GQA_GROUP + g] for g in range(GQA_GROUP))
    return jnp.concatenate(outs, axis=-1)


def _ctx_attn_kernel(sink_ref, q_ref, k_ref, v_ref, o_ref):
    q = q_ref[0]
    o_ref[0] = _gqa_attention(q, k_ref[0].astype(BF16), v_ref[0].astype(BF16), sink_ref, q.shape[0], None)


def _context_attention(q, k, v, sink):
    b, L, _ = q.shape
    seq = lambda n: pl.BlockSpec((1, L, n), lambda bi: (bi, 0, 0))
    return pl.pallas_call(
        _ctx_attn_kernel,
        grid=(b,),
        in_specs=[pl.BlockSpec(memory_space=pltpu.SMEM), seq(ATTN_WIDTH), seq(KV_WIDTH), seq(KV_WIDTH)],
        out_specs=seq(ATTN_WIDTH),
        out_shape=jax.ShapeDtypeStruct((b, L, ATTN_WIDTH), F32),
        compiler_params=_params("parallel"),
        name="context_attention",
    )(sink, q, k, v)


Q_BLOCKS = 2
LOCAL_BLOCKS = Q_BLOCKS + 2


def _lat_attn_kernel(sink_ref, q_ref, *refs):
    k_refs, v_refs = refs[:LOCAL_BLOCKS], refs[LOCAL_BLOCKS:2 * LOCAL_BLOCKS]
    kc_ref, vc_ref, band_ref, o_ref = refs[2 * LOCAL_BLOCKS:]
    keys = jnp.concatenate([r[0] for r in k_refs] + [kc_ref[0, 0]], axis=0)
    vals = jnp.concatenate([r[0] for r in v_refs] + [vc_ref[0, 0]], axis=0)
    o_ref[0] = _gqa_attention(q_ref[0], keys, vals, sink_ref, Q_BLOCKS * BLOCK, band_ref[0])


def _latent_attention(q, k, v, cache_k, cache_v, layer, sink):
    b, L, _ = q.shape
    past = cache_k.shape[2]
    nb = L // BLOCK
    steps = nb // Q_BLOCKS
    assert nb % Q_BLOCKS == 0 and steps >= 2
    rows = lambda n: pl.BlockSpec((1, Q_BLOCKS * BLOCK, n), lambda bi, i: (bi, i, 0))
    local = [pl.BlockSpec((1, BLOCK, KV_WIDTH),
                          functools.partial(lambda bi, i, j: (bi, jnp.clip(Q_BLOCKS * i - 1 + j, 0, nb - 1), 0), j=j))
             for j in range(LOCAL_BLOCKS)]
    cache = pl.BlockSpec((1, 1, past, KV_WIDTH), lambda bi, i: (bi, layer, 0, 0))
    band_spec = pl.BlockSpec((1, Q_BLOCKS * BLOCK, LOCAL_BLOCKS * BLOCK + past),
                             lambda bi, i: (jnp.minimum(i, 1) + jnp.maximum(i - (steps - 2), 0), 0, 0))
    return pl.pallas_call(
        _lat_attn_kernel,
        grid=(b, steps),
        in_specs=[pl.BlockSpec(memory_space=pltpu.SMEM), rows(ATTN_WIDTH)] + local + local + [cache, cache, band_spec],
        out_specs=rows(ATTN_WIDTH),
        out_shape=jax.ShapeDtypeStruct((b, L, ATTN_WIDTH), F32),
        compiler_params=_params("parallel", "parallel"),
        name="latent_attention",
    )(sink, q, *([k] * LOCAL_BLOCKS), *([v] * LOCAL_BLOCKS), cache_k, cache_v, _band_bias(past))


def _band_bias(past):
    n_local = LOCAL_BLOCKS * BLOCK
    r = jnp.arange(Q_BLOCKS * BLOCK)[:, None]
    col = jnp.arange(n_local + past)[None, :]
    band = (jnp.abs(col - BLOCK - r) <= WINDOW) | (col >= n_local)
    first = band & (col >= BLOCK)
    last = band & ((col < n_local - BLOCK) | (col >= n_local))
    return jnp.where(jnp.stack([first, band, last]), 0.0, NEG_INF).astype(F32)


def _rope_tables(L):
    t = jnp.arange(L)
    row = (t // GRID_W).astype(F32)
    col = (t % GRID_W).astype(F32)
    nf = HEAD_DIM // 4
    inv = ROPE_BASE ** (-jnp.arange(nf, dtype=F32) / nf)
    ang_r = row[:, None] * inv[None, :]
    ang_c = col[:, None] * inv[None, :]
    cos = jnp.concatenate([jnp.cos(ang_r), jnp.cos(ang_r), jnp.cos(ang_c), jnp.cos(ang_c)], axis=-1)
    sin = jnp.concatenate([-jnp.sin(ang_r), jnp.sin(ang_r), -jnp.sin(ang_c), jnp.sin(ang_c)], axis=-1)
    return jnp.tile(cos, (1, N_Q_HEADS)), jnp.tile(sin, (1, N_Q_HEADS))


def _block_mean():
    i = jnp.arange(LANES) // HEAD_DIM
    return jnp.where(i[:, None] == i[None, :], 1.0 / HEAD_DIM, 0.0).astype(BF16)


SHIFT_K = 2 * LANES

def _ssd_conv_chunk(xbc_ref, xp_ref, xn_ref, cw_ref, cb_ref, shift_ref, has_prev, has_next):
    main = xbc_ref[0]
    rows = jnp.concatenate([jnp.where(has_prev, xp_ref[0], 0.0), main, jnp.where(has_next, xn_ref[0], 0.0),
                            jnp.zeros((SHIFT_K - CHUNK - 2 * HALO, CONV_CH), F32)], axis=0).astype(BF16)
    taps = jnp.dot(shift_ref[...], rows, preferred_element_type=F32)
    acc = cb_ref[...] + main * cw_ref[CONV_W // 2:CONV_W // 2 + 1, :]
    for i, kk in enumerate(k for k in range(CONV_W) if k != CONV_W // 2):
        acc = acc + taps[i * CHUNK:(i + 1) * CHUNK] * cw_ref[kk:kk + 1, :]
    return _silu(acc)


def _split3(x):
    hi = x.astype(BF16)
    rest = x - hi.astype(F32)
    mid = rest.astype(BF16)
    return hi, mid, (rest - mid.astype(F32)).astype(BF16)


SSD_END_ROWS = (CHUNK - 1, 0)


def _ssd_decay_table_stages(dtt_refs, tri_ref, biast_ref, alogt_ref, tabt_ref, terms_ref):
    steps = []
    for d, dtt_ref in enumerate(dtt_refs):
        dtt = _softplus(dtt_ref[0, 0] + biast_ref[d])
        steps.append((dtt, _split3(dtt * (-jnp.exp(alogt_ref[d]) * LOG2E))))
    yield
    sums = []
    for d, (_, parts_t) in enumerate(steps):
        sums.append(sum(jnp.dot(part, tri_ref[1 - d], preferred_element_type=F32) for part in parts_t))
    yield
    pad = jnp.zeros((CHUNK - 3 * SSD_HEADS, CHUNK), F32)
    terms = [jnp.concatenate([p.astype(F32) for p in _split3(acum_t)] + [pad], axis=0).T.astype(BF16)
             for acum_t in sums]
    yield
    for d, ((dtt, _), acum_t) in enumerate(zip(steps, sums)):
        last_t = acum_t[:, SSD_END_ROWS[d]:SSD_END_ROWS[d] + 1]
        terms_ref[d] = terms[d]
        tabt_ref[d, 0] = acum_t
        tabt_ref[d, 1] = dtt
        tabt_ref[d, 2] = jnp.exp2(acum_t)
        tabt_ref[d, 3] = jnp.exp2(last_t - acum_t) * dtt


def _ssd_load_decay_tables(operands, tri_ref, spread_ref, tabt_ref, terms_ref):
    a_cols = jnp.dot(terms_ref[...].reshape(2 * CHUNK, CHUNK), spread_ref[...], preferred_element_type=F32)
    tables = []
    for d, (xs_t, bc) in enumerate(operands):
        acum_t = tabt_ref[d, 0]
        end = SSD_END_ROWS[d]
        tables.append({"bc": bc, "xs_t": xs_t, "mask": tri_ref[d] > 0, "acum_t": acum_t,
                       "a_col": a_cols[d * CHUNK:(d + 1) * CHUNK],
                       "dt_t": tabt_ref[d, 1], "decay_in_t": tabt_ref[d, 2], "coef_t": tabt_ref[d, 3],
                       "chunk_decay_t": jnp.exp2(acum_t[:, end:end + 1])})
    return tables


BACKGROUND_POINTS = ((0, 0), (0, 2), (1, 0))


def _ssd_scan_chunks(scans, background):
    zero = jnp.zeros((SSD_HEADDIM, CHUNK), BF16)
    ys = [[] for _ in scans]
    next(background, None)
    for g in range(SSD_GROUPS):
        hs = slice(g * HEADS_PER_GROUP * SSD_HEADDIM, (g + 1) * HEADS_PER_GROUP * SSD_HEADDIM)
        heads = range(g * HEADS_PER_GROUP, (g + 1) * HEADS_PER_GROUP)
        stage = []
        for t, s_ref in scans:
            bm = t["bc"][:, g * D_STATE:(g + 1) * D_STATE].astype(BF16)
            cm = t["bc"][:, (SSD_GROUPS + g) * D_STATE:(SSD_GROUPS + g + 1) * D_STATE].astype(BF16)
            cbm = jnp.where(t["mask"], _bdot_nt(cm, bm), 0.0)
            state = s_ref[hs, :]
            y_off = _bdot_nt(state, cm)
            y_off = jnp.concatenate(
                [y_off[(h - heads[0]) * SSD_HEADDIM:(h - heads[0] + 1) * SSD_HEADDIM] * t["decay_in_t"][h:h + 1, :]
                 for h in heads], axis=0)
            stage.append((bm, cbm, state, y_off, []))
        for pair in range(HEADS_PER_GROUP // 2):
            if (g, pair) in BACKGROUND_POINTS:
                next(background, None)
            for (t, _), (_, cbm, _, _, y_diag) in zip(scans, stage):
                ws, xs = [], []
                for h in (heads[0] + 2 * pair, heads[0] + 2 * pair + 1):
                    a_col = t["a_col"][:, h * CHUNK:(h + 1) * CHUNK]
                    ws.append((cbm * jnp.exp2(jnp.minimum(a_col - t["acum_t"][h:h + 1, :], 0.0))).astype(BF16))
                    xs.append((t["xs_t"][h * SSD_HEADDIM:(h + 1) * SSD_HEADDIM, :] * t["dt_t"][h:h + 1, :]).astype(BF16))
                lhs = jnp.concatenate([jnp.concatenate([xs[0], zero], axis=1),
                                       jnp.concatenate([zero, xs[1]], axis=1)], axis=0)
                y_diag.append(_bdot_nt(lhs, jnp.concatenate(ws, axis=1)))
        for i, ((t, s_ref), (bm, _, state, y_off, y_diag)) in enumerate(zip(scans, stage)):
            ys[i].append(jnp.concatenate(y_diag, axis=0) + y_off)
            x_state = jnp.concatenate(
                [(t["xs_t"][h * SSD_HEADDIM:(h + 1) * SSD_HEADDIM, :] * t["coef_t"][h:h + 1, :]).astype(BF16)
                 for h in heads], axis=0)
            update = jnp.dot(x_state, bm, preferred_element_type=F32)
            for h in heads:
                hp = slice(h * SSD_HEADDIM, (h + 1) * SSD_HEADDIM)
                rel = slice((h - heads[0]) * SSD_HEADDIM, (h - heads[0] + 1) * SSD_HEADDIM)
                s_ref[hp, :] = state[rel] * t["chunk_decay_t"][h:h + 1, :] + update[rel]
    return [jnp.concatenate(y, axis=0) for y in ys]


def _ssd_kernel(xf_ref, xfp_ref, xfn_ref, xb_ref, xbp_ref, xbn_ref, dtt0f_ref, dtt0b_ref,
                dttnf_ref, dttnb_ref, init_ref, tri_ref, spread_ref, shift_ref, cw_ref, cb_ref, biast_ref,
                alogt_ref, dskip_ref, o_ref, fin_ref, xst_ref, bc_ref, sf_ref, sb_ref, tabt_ref, terms_ref,
                *, nc, has_init):
    s = pl.program_id(1)
    half = nc // 2
    first_half = s < half
    cf = s
    cb = nc - 1 - s
    rf = pl.multiple_of(cf * CHUNK, CHUNK)
    rb = pl.multiple_of(cb * CHUNK, CHUNK)

    @pl.when(first_half)
    def _():
        for c, rows, refs in ((cf, rf, (xf_ref, xfp_ref, xfn_ref)), (cb, rb, (xb_ref, xbp_ref, xbn_ref))):
            xc = _ssd_conv_chunk(*refs, cw_ref, cb_ref, shift_ref, c > 0, c < nc - 1)
            xst_ref[c] = xc[:, 0:SSD_INNER].T
            bc_ref[pl.ds(rows, CHUNK), :] = xc[:, SSD_INNER:]

    table_refs = (tri_ref, biast_ref, alogt_ref, tabt_ref, terms_ref)

    @pl.when(s == 0)
    def _():
        if has_init:
            sf_ref[...] = init_ref[0, 0, 0]
            sb_ref[...] = init_ref[0, 0, 1]
        else:
            sf_ref[...] = jnp.zeros(sf_ref.shape, F32)
            sb_ref[...] = jnp.zeros(sb_ref.shape, F32)
        for _ in _ssd_decay_table_stages((dtt0f_ref, dtt0b_ref), *table_refs):
            pass

    tf, tb = _ssd_load_decay_tables([(xst_ref[cf], bc_ref[pl.ds(rf, CHUNK), :]),
                                     (xst_ref[cb], bc_ref[pl.ds(rb, CHUNK), :])],
                                    tri_ref, spread_ref, tabt_ref, terms_ref)
    next_tables = _ssd_decay_table_stages((dttnf_ref, dttnb_ref), *table_refs)
    yf_t, yb_t = _ssd_scan_chunks([(tf, sf_ref), (tb, sb_ref)], next_tables)
    for _ in next_tables:
        pass
    yf = (yf_t + dskip_ref[...] * tf["xs_t"]).T
    yb = yb_t.T

    @pl.when(first_half)
    def _():
        o_ref[0, pl.ds(rf, CHUNK), :] = yf
        o_ref[0, pl.ds(rb, CHUNK), :] = yb

    @pl.when(jnp.logical_not(first_half))
    def _():
        o_ref[0, pl.ds(rf, CHUNK), :] = o_ref[0, pl.ds(rf, CHUNK), :] + yf
        o_ref[0, pl.ds(rb, CHUNK), :] = o_ref[0, pl.ds(rb, CHUNK), :] + yb

    @pl.when(s == nc - 1)
    def _():
        fin_ref[0, 0] = sf_ref[...]
        fin_ref[0, 1] = sb_ref[...]


def _ssd_mixer(xbc, dt_raw, init_state, layer, consts, p):
    b, L, _ = xbc.shape
    nc = L // CHUNK
    has_init = init_state is not None
    dt_t = jnp.stack([dt_raw[:, :, 0:SSD_HEADS], dt_raw[:, :, SSD_HEADS:2 * SSD_HEADS]], axis=1)
    dt_t = jnp.swapaxes(dt_t, 2, 3)
    assert nc % 2 == 0
    half = nc // 2
    if not has_init:
        init_state = jnp.zeros((1, 1, 1, HALO, LANES), F32)
        init_spec = pl.BlockSpec((1, 1, 1, HALO, LANES), lambda bi, s: (0, 0, 0, 0, 0))
    else:
        init_spec = pl.BlockSpec((1, 1, 2, SSD_INNER, D_STATE), lambda bi, s: (bi, layer, 0, 0, 0))
    per = CHUNK // HALO
    last = L // HALO - 1
    conv_f = lambda s: jnp.minimum(s, half - 1)
    conv_b = lambda s: jnp.maximum(nc - 1 - s, half)

    def conv_specs(chunk):
        return [pl.BlockSpec((1, CHUNK, CONV_CH), lambda bi, s: (bi, chunk(s), 0)),
                pl.BlockSpec((1, HALO, CONV_CH), lambda bi, s: (bi, jnp.maximum(chunk(s) * per - 1, 0), 0)),
                pl.BlockSpec((1, HALO, CONV_CH), lambda bi, s: (bi, jnp.minimum((chunk(s) + 1) * per, last), 0))]

    in_specs = conv_specs(conv_f) + conv_specs(conv_b) + [
        pl.BlockSpec((1, 1, SSD_HEADS, CHUNK), lambda bi, s: (bi, 0, 0, 0)),
        pl.BlockSpec((1, 1, SSD_HEADS, CHUNK), lambda bi, s: (bi, 1, 0, nc - 1)),
        pl.BlockSpec((1, 1, SSD_HEADS, CHUNK), lambda bi, s: (bi, 0, 0, jnp.minimum(s + 1, nc - 1))),
        pl.BlockSpec((1, 1, SSD_HEADS, CHUNK), lambda bi, s: (bi, 1, 0, jnp.maximum(nc - 2 - s, 0))),
        init_spec,
        _const_spec((2, CHUNK, CHUNK)), _const_spec((CHUNK, SSD_HEADS * CHUNK)),
        _const_spec(((CONV_W - 1) * CHUNK, SHIFT_K)), _const_spec((CONV_W, CONV_CH)), _const_spec((1, CONV_CH)),
        _const_spec((2, SSD_HEADS, 1)), _const_spec((2, SSD_HEADS, 1)), _const_spec((SSD_INNER, CHUNK)),
    ]
    out, fin = pl.pallas_call(
        functools.partial(_ssd_kernel, nc=nc, has_init=has_init),
        grid=(b, nc),
        in_specs=in_specs,
        out_specs=[pl.BlockSpec((1, L, SSD_INNER), lambda bi, s: (bi, 0, 0)),
                   pl.BlockSpec((1, 2, SSD_INNER, D_STATE), lambda bi, s: (bi, 0, 0, 0))],
        out_shape=[jax.ShapeDtypeStruct((b, L, SSD_INNER), F32),
                   jax.ShapeDtypeStruct((b, 2, SSD_INNER, D_STATE), F32)],
        scratch_shapes=[pltpu.VMEM((nc, SSD_INNER, CHUNK), F32), pltpu.VMEM((L, CONV_CH - SSD_INNER), F32),
                        pltpu.VMEM((SSD_INNER, D_STATE), F32), pltpu.VMEM((SSD_INNER, D_STATE), F32),
                        pltpu.VMEM((2, 4, SSD_HEADS, CHUNK), F32), pltpu.VMEM((2, CHUNK, CHUNK), BF16)],
        compiler_params=_params("parallel", "arbitrary"),
        name="ssd_mixer",
    )(xbc, xbc, xbc, xbc, xbc, xbc, dt_t, dt_t, dt_t, dt_t, init_state, consts["tri"], consts["spread"],
      consts["shift"], p["conv_w"], p["conv_b"], p["dt_bias_t"], p["a_log_t"], p["d_skip"])
    return out, fin


def _ssd_consts():
    i = jnp.arange(CHUNK)
    lower = (i[None, :] <= i[:, None]).astype(F32)
    t = jnp.arange(CHUNK)[:, None]
    col = jnp.arange(SHIFT_K)[None, :]
    shift = jnp.concatenate([(col == HALO + t + kk - CONV_W // 2) for kk in range(CONV_W) if kk != CONV_W // 2], axis=0)
    k = jnp.arange(CHUNK)[:, None]
    lane_head = jnp.arange(SSD_HEADS * CHUNK)[None, :] // CHUNK
    spread = (k < 3 * SSD_HEADS) & (k % SSD_HEADS == lane_head)
    return {"tri": jnp.stack([lower, lower.T]).astype(BF16), "shift": shift.astype(BF16),
            "spread": spread.astype(BF16)}


FF_CHUNK = 1024


def _merge_mlp_kernel(x_ref, mod_ref, pool_ref, attn_ref, ssd_ref, z_ref, gate_ref, g_ref, sg_ref,
                      wp_ref, wa_ref, ws_ref, wo_ref, w1_ref, w2_ref, o_ref):
    mod = mod_ref[0]
    pool_o = _bdot(pool_ref[0], wp_ref[...])
    ssd = ssd_ref[0] * _silu(z_ref[0])
    ssd = ssd * lax.rsqrt(jnp.mean(ssd * ssd, axis=-1, keepdims=True) + EPS) * sg_ref[...]
    attn_o = _bdot(attn_ref[0], wa_ref[...])
    ssd_o = _bdot(ssd, ws_ref[...])
    gates = _sigmoid(gate_ref[0])
    merged = (gates[:, 0:D_MODEL] * pool_o + gates[:, D_MODEL:2 * D_MODEL] * attn_o
              + gates[:, 2 * D_MODEL:] * ssd_o)
    x = x_ref[0] + mod[:, 2 * D_MODEL:3 * D_MODEL] * _bdot(merged, wo_ref[...])
    h = _rms_mod(x, g_ref[...], mod[:, 4 * D_MODEL:5 * D_MODEL], mod[:, 3 * D_MODEL:4 * D_MODEL]).astype(BF16)
    acc = jnp.zeros(x.shape, F32)
    for j in range(D_FF // FF_CHUNK):
        ff = slice(j * FF_CHUNK, (j + 1) * FF_CHUNK)
        a = jnp.dot(h, w1_ref[:, ff], preferred_element_type=F32)
        acc = acc + _bdot(jnp.square(jnp.maximum(a, 0.0)), w2_ref[ff, :])
    o_ref[0] = x + mod[:, 5 * D_MODEL:] * acc


def _merge_mlp(x, mods, layer, per_batch, pool, attn, ssd, z, gate, norm_g, ssd_norm_g, w):
    b, L, _ = x.shape
    tm = TOKEN_TILE
    tok = lambda n: pl.BlockSpec((1, tm, n), lambda bi, i: (bi, i, 0))
    weights = [w["pool_o"], w["attn_o"], w["ssd_o"], w["out"], w["mlp1"], w["mlp2"]]
    return pl.pallas_call(
        _merge_mlp_kernel,
        grid=(b, L // tm),
        in_specs=[tok(D_MODEL), _mod_spec(layer, per_batch), tok(POOL_WIDTH), tok(ATTN_WIDTH), tok(SSD_INNER),
                  tok(SSD_INNER), tok(N_BRANCH * D_MODEL), _const_spec((1, D_MODEL)), _const_spec((1, SSD_INNER))]
                 + [_const_spec(a.shape) for a in weights],
        out_specs=tok(D_MODEL),
        out_shape=jax.ShapeDtypeStruct((b, L, D_MODEL), F32),
        compiler_params=_params("parallel", "parallel"),
        name="merge_mlp",
    )(x, mods, pool, attn, ssd, z, gate, norm_g, ssd_norm_g, *weights)


def _layer(x, mods, layer, per_batch, w, p, consts, ctx):
    u, q, k, v, z, xbc, dt_raw, gate = _in_projection(
        x, mods, layer, per_batch, p["norm1_g"], p["q_norm_g"], p["k_norm_g"], consts["bm"],
        None if ctx is None else consts["rope"], w)
    pool = _pool_mixer(u, w["pool_w"], p["pool_scale"])
    if ctx is None:
        attn = _context_attention(q, k, v, p["attn_sink"])
        init = None
    else:
        cache_k, cache_v, init = ctx
        attn = _latent_attention(q, k, v, cache_k, cache_v, layer, p["attn_sink"])
    ssd, fin = _ssd_mixer(xbc, dt_raw, init, layer, consts, p)
    x = _merge_mlp(x, mods, layer, per_batch, pool, attn, ssd, z, gate, p["norm2_g"], p["ssd_norm_g"], w)
    return x, (k, v, fin)


def _layer_weights(l, w_in, pool_w, w_pool_o, w_attn_o, w_ssd_o, w_out, w_mlp1, w_mlp2):
    wi = w_in[l]
    o_z = POOL_WIDTH + ATTN_WIDTH + 2 * KV_WIDTH
    o_x = o_z + SSD_INNER
    o_d = o_x + CONV_CH
    o_g = o_d + 2 * SSD_HEADS
    wd = jnp.pad(wi[:, o_d:o_g], ((0, 0), (0, DT_PAD - 2 * SSD_HEADS)))
    cast = lambda a: a.astype(BF16)
    return {"a": cast(wi[:, :o_z]), "z": cast(wi[:, o_z:o_x]), "xbc": cast(wi[:, o_x:o_d]), "dt": cast(wd),
            "gate": cast(wi[:, o_g:]), "pool_w": cast(pool_w[l]), "pool_o": cast(w_pool_o[l]),
            "attn_o": cast(w_attn_o[l]), "ssd_o": cast(w_ssd_o[l]), "out": cast(w_out[l]),
            "mlp1": cast(w_mlp1[l]), "mlp2": cast(w_mlp2[l])}


def kernel(x_prompt, x_sample, cache_k, cache_v, state_ssd, c, c_ctx, w_mod, b_mod, norm1_g, norm2_g, w_in, pool_w, pool_scale, w_pool_o, q_norm_g, k_norm_g, attn_sink, w_attn_o, conv_w, conv_b, dt_bias, a_log, d_skip, ssd_norm_g, w_ssd_o, w_out, w_mlp1, w_mlp2):
    batch, seq, _ = x_prompt.shape
    dec_batch, dec_seq, _ = x_sample.shape
    past = cache_k.shape[2]
    assert 1 + dec_batch <= MOD_ROWS

    cvecs = jnp.concatenate([c_ctx[None, :], c, jnp.zeros((MOD_ROWS - 1 - dec_batch, D_MODEL), F32)], axis=0)
    mods = _modulation(cvecs, w_mod, b_mod)

    consts = {"bm": _block_mean(), "rope": _rope_tables(dec_seq)}
    consts.update(_ssd_consts())
    cache_k = cache_k.reshape(dec_batch, DEPTH, past, KV_WIDTH).astype(BF16)
    cache_v = cache_v.reshape(dec_batch, DEPTH, past, KV_WIDTH).astype(BF16)
    state = state_ssd.reshape(dec_batch, DEPTH, 2, SSD_INNER, D_STATE)

    y_prompt, y_sample = x_prompt, x_sample
    ks, vs, ss = [], [], []
    for l in range(DEPTH):
        w = _layer_weights(l, w_in, pool_w, w_pool_o, w_attn_o, w_ssd_o, w_out, w_mlp1, w_mlp2)
        p = {"norm1_g": norm1_g[l][None, :], "norm2_g": norm2_g[l][None, :], "pool_scale": pool_scale[l][None, :],
             "q_norm_g": jnp.tile(q_norm_g[l], N_Q_HEADS)[None, :], "k_norm_g": jnp.tile(k_norm_g[l], N_KV_HEADS)[None, :],
             "attn_sink": attn_sink[l], "conv_w": conv_w[l], "conv_b": conv_b[l][None, :],
             "dt_bias_t": dt_bias[l][:, :, None], "a_log_t": a_log[l][:, :, None],
             "d_skip": jnp.broadcast_to(jnp.repeat(d_skip[l], SSD_HEADDIM)[:, None], (SSD_INNER, CHUNK)),
             "ssd_norm_g": ssd_norm_g[l][None, :]}
        y_prompt, (k_l, v_l, s_l) = _layer(y_prompt, mods, l, False, w, p, consts, None)
        ks.append(k_l.reshape(batch, seq, N_KV_HEADS, HEAD_DIM))
        vs.append(v_l.reshape(batch, seq, N_KV_HEADS, HEAD_DIM))
        ss.append(s_l.reshape(batch, 2, SSD_HEADS, SSD_HEADDIM, D_STATE))
        y_sample, _ = _layer(y_sample, mods, l, True, w, p, consts, (cache_k, cache_v, state))
    return (y_prompt, y_sample, jnp.stack(ks, axis=1), jnp.stack(vs, axis=1), jnp.stack(ss, axis=1))
```

```python
import functools
import math

import jax
import jax.numpy as jnp
from jax import lax
from jax.experimental import pallas as pl
from jax.experimental.pallas import tpu as pltpu

F32 = jnp.float32
BF16 = jnp.bfloat16

D_MODEL = 1024
DEPTH = 2
GRID_W = 64
MOD_CHUNKS = 6
EPS = 1e-6
POOL_WIDTH = D_MODEL // 2
POOL_GROUPS = 4
POOL_GROUP_W = POOL_WIDTH // POOL_GROUPS
POOL_WINDOWS = (2, 4, 8, 16)
N_Q_HEADS = 8
N_KV_HEADS = 2
GQA_GROUP = N_Q_HEADS // N_KV_HEADS
HEAD_DIM = 64
ATTN_WIDTH = N_Q_HEADS * HEAD_DIM
KV_WIDTH = N_KV_HEADS * HEAD_DIM
WINDOW = 128
BLOCK = 128
ROPE_BASE = 10000.0
ATTN_SCALE = HEAD_DIM ** -0.5
SSD_HEADS = 16
SSD_HEADDIM = 64
SSD_INNER = SSD_HEADS * SSD_HEADDIM
SSD_GROUPS = 2
HEADS_PER_GROUP = SSD_HEADS // SSD_GROUPS
D_STATE = 128
CONV_W = 5
CHUNK = 128
CONV_CH = SSD_INNER + 2 * SSD_GROUPS * D_STATE
N_BRANCH = 3
D_FF = 4 * D_MODEL

LANES = 128
HALO = 8
MOD_ROWS = 16
VMEM_LIMIT = 56 * 1024 * 1024
TOKEN_TILE = 256
NEG_INF = float("-inf")
LOG2E = math.log2(math.e)
SCORE_SCALE = ATTN_SCALE * LOG2E


def _params(*sem):
    return pltpu.CompilerParams(dimension_semantics=sem, vmem_limit_bytes=VMEM_LIMIT)


def _const_spec(shape):
    nd = len(shape)
    return pl.BlockSpec(shape, lambda *_: (0,) * nd, pipeline_mode=pl.Buffered(1))


def _bdot(a, b):
    return jnp.dot(a.astype(BF16), b.astype(BF16), preferred_element_type=F32)


def _bdot_nt(a, b):
    return lax.dot_general(a.astype(BF16), b.astype(BF16), (((1,), (1,)), ((), ())),
                           preferred_element_type=F32)


def _sigmoid(x):
    return 0.5 * jnp.tanh(0.5 * x) + 0.5


def _silu(x):
    half = 0.5 * x
    return half * jnp.tanh(half) + half


def _softplus(x):
    return jnp.maximum(x, 0.0) + jnp.log1p(jnp.exp(-jnp.abs(x)))


def _rms_mod(x, g, scale, shift):
    ms = jnp.mean(x * x, axis=-1, keepdims=True)
    return (x * lax.rsqrt(ms + EPS)) * g * (1.0 + scale) + shift


def _mod_kernel(c_ref, w_ref, b_ref, o_ref):
    o_ref[0] = _bdot(_silu(c_ref[...]), w_ref[0]) + b_ref[0]


def _modulation(cvecs, w_mod, b_mod):
    n = MOD_CHUNKS * D_MODEL
    tn = n // 4
    out = pl.pallas_call(
        _mod_kernel,
        grid=(DEPTH, n // tn),
        in_specs=[pl.BlockSpec((MOD_ROWS, D_MODEL), lambda l, j: (0, 0)),
                  pl.BlockSpec((1, D_MODEL, tn), lambda l, j: (l, 0, j)),
                  pl.BlockSpec((1, 1, tn), lambda l, j: (l, 0, j))],
        out_specs=pl.BlockSpec((1, MOD_ROWS, tn), lambda l, j: (l, 0, j)),
        out_shape=jax.ShapeDtypeStruct((DEPTH, MOD_ROWS, n), F32),
        compiler_params=_params("arbitrary", "arbitrary"),
        name="modulation",
    )(cvecs, w_mod, b_mod.reshape(DEPTH, 1, n))
    return out.reshape(DEPTH * MOD_ROWS, 1, n)


def _mod_spec(layer, per_batch):
    base = layer * MOD_ROWS
    if per_batch:
        return pl.BlockSpec((1, 1, MOD_CHUNKS * D_MODEL), lambda b, i: (base + 1 + b, 0, 0))
    return pl.BlockSpec((1, 1, MOD_CHUNKS * D_MODEL), lambda b, i: (base, 0, 0))


DT_PAD = LANES


def _inproj_kernel(*refs, rope):
    x_ref, mod_ref, g_ref, qg_ref, kg_ref, bm_ref = refs[:6]
    refs = refs[6:]
    if rope:
        cos_ref, sin_ref = refs[:2]
        refs = refs[2:]
    wa_ref, wz_ref, wx_ref, wd_ref, wg_ref, u_ref, q_ref, k_ref, v_ref, z_ref, xbc_ref, dt_ref, gate_ref = refs
    mod = mod_ref[0]
    tm = x_ref.shape[1]
    for r0 in range(0, tm, TOKEN_TILE):
        rows = slice(r0, r0 + TOKEN_TILE)
        h = _rms_mod(x_ref[0, rows, :], g_ref[...], mod[:, D_MODEL:2 * D_MODEL], mod[:, 0:D_MODEL]).astype(BF16)
        a = jnp.dot(h, wa_ref[...], preferred_element_type=F32)
        u_ref[0, rows, :] = a[:, 0:POOL_WIDTH]
        v_ref[0, rows, :] = a[:, POOL_WIDTH + ATTN_WIDTH + KV_WIDTH:].astype(v_ref.dtype)
        z_ref[0, rows, :] = jnp.dot(h, wz_ref[...], preferred_element_type=F32)
        qn = _head_rms(a[:, POOL_WIDTH:POOL_WIDTH + ATTN_WIDTH], bm_ref[...], qg_ref[...])
        kn = _head_rms(a[:, POOL_WIDTH + ATTN_WIDTH:POOL_WIDTH + ATTN_WIDTH + KV_WIDTH], bm_ref[...], kg_ref[...])
        xbc_ref[0, rows, :] = jnp.dot(h, wx_ref[...], preferred_element_type=F32)
        if rope:
            qn = _rope(qn, cos_ref[rows, :], sin_ref[rows, :])
            kn = _rope(kn, cos_ref[rows, 0:KV_WIDTH], sin_ref[rows, 0:KV_WIDTH])
        q_ref[0, rows, :] = (qn * SCORE_SCALE).astype(q_ref.dtype)
        k_ref[0, rows, :] = kn.astype(k_ref.dtype)
        dt_ref[0, rows, :] = jnp.dot(h, wd_ref[...], preferred_element_type=F32)
        gate_ref[0, rows, :] = jnp.dot(h, wg_ref[...], preferred_element_type=F32)


def _in_projection(x, mods, layer, per_batch, norm_g, qg, kg, bm, rope_tabs, w):
    b, L, _ = x.shape
    tm = min(2 * TOKEN_TILE, L)
    rope = rope_tabs is not None
    kv_dtype = BF16 if rope else F32
    outs = ((POOL_WIDTH, F32), (ATTN_WIDTH, BF16), (KV_WIDTH, kv_dtype), (KV_WIDTH, kv_dtype), (SSD_INNER, F32),
            (CONV_CH, F32), (DT_PAD, F32), (N_BRANCH * D_MODEL, F32))
    tok = lambda n: pl.BlockSpec((1, tm, n), lambda bi, i: (bi, i, 0))
    tab = pl.BlockSpec((tm, ATTN_WIDTH), lambda bi, i: (i, 0))
    weights = [w["a"], w["z"], w["xbc"], w["dt"], w["gate"]]
    return pl.pallas_call(
        functools.partial(_inproj_kernel, rope=rope),
        grid=(b, L // tm),
        in_specs=[tok(D_MODEL), _mod_spec(layer, per_batch), _const_spec((1, D_MODEL)), _const_spec(qg.shape),
                  _const_spec(kg.shape), _const_spec(bm.shape)] + ([tab, tab] if rope else [])
                 + [_const_spec(a.shape) for a in weights],
        out_specs=[tok(n) for n, _ in outs],
        out_shape=[jax.ShapeDtypeStruct((b, L, n), dt) for n, dt in outs],
        compiler_params=_params("parallel", "parallel"),
        name="in_projection",
    )(x, mods, norm_g, qg, kg, bm, *(rope_tabs if rope else ()), *weights)


POOL_ROWS = 256
POOL_K = POOL_ROWS + LANES


def _pool_kernel(u_ref, band_ref, pw_ref, ps_ref, o_ref, *, L):
    halo_zeros = jnp.zeros((HALO, POOL_WIDTH), F32)
    tail_zeros = jnp.zeros((POOL_K - POOL_ROWS - 2 * HALO, POOL_WIDTH), F32)
    groups = [slice(gi * POOL_GROUP_W, (gi + 1) * POOL_GROUP_W) for gi in range(POOL_GROUPS)]
    for r0 in range(0, L, POOL_ROWS):
        main = u_ref[0, r0:r0 + POOL_ROWS, :]
        prev = u_ref[0, r0 - HALO:r0, :] if r0 > 0 else halo_zeros
        nxt = u_ref[0, r0 + POOL_ROWS:r0 + POOL_ROWS + HALO, :] if r0 + POOL_ROWS < L else halo_zeros
        rows = jnp.concatenate([prev, main, nxt, tail_zeros], axis=0)
        hi = rows.astype(BF16)
        lo = (rows - hi.astype(F32)).astype(BF16)
        t = r0 + lax.broadcasted_iota(jnp.int32, (POOL_ROWS, POOL_GROUP_W), 0)
        sums = [jnp.dot(band_ref[gi], hi[:, cols], preferred_element_type=F32)
                + jnp.dot(band_ref[gi], lo[:, cols], preferred_element_type=F32) for gi, cols in enumerate(groups)]
        pooled = []
        for w, cols, acc in zip(POOL_WINDOWS, groups, sums):
            cnt = (jnp.minimum(t + w // 2, L) - jnp.maximum(t - w // 2, 0)).astype(F32)
            pooled.append((acc / cnt - main[:, cols]).astype(BF16))
        mixed = [jnp.dot(pooled[gi], pw_ref[gi], preferred_element_type=F32) for gi in range(POOL_GROUPS)]
        for cols, m in zip(groups, mixed):
            o_ref[0, r0:r0 + POOL_ROWS, cols] = m * ps_ref[:, cols]


def _pool_bands():
    d = jnp.arange(POOL_K)[None, :] - HALO - jnp.arange(POOL_ROWS)[:, None]
    return jnp.stack([(d >= -(w // 2)) & (d < w // 2) for w in POOL_WINDOWS]).astype(BF16)


def _pool_mixer(u, pool_w, pool_scale):
    b, L, _ = u.shape
    assert max(POOL_WINDOWS) // 2 <= HALO and L % POOL_ROWS == 0
    seq = pl.BlockSpec((1, L, POOL_WIDTH), lambda bi: (bi, 0, 0))
    bands = _pool_bands()
    return pl.pallas_call(
        functools.partial(_pool_kernel, L=L),
        grid=(b,),
        in_specs=[seq, _const_spec(bands.shape), _const_spec(pool_w.shape), _const_spec(pool_scale.shape)],
        out_specs=seq,
        out_shape=jax.ShapeDtypeStruct((b, L, POOL_WIDTH), F32),
        compiler_params=_params("parallel"),
        name="pool_mixer",
    )(u, bands, pool_w, pool_scale)


def _head_rms(x, block_mean, g):
    sq = x * x
    hi = sq.astype(BF16)
    lo = (sq - hi.astype(F32)).astype(BF16)
    ms = jnp.concatenate(
        [jnp.dot(hi[:, c:c + LANES], block_mean, preferred_element_type=F32)
         + jnp.dot(lo[:, c:c + LANES], block_mean, preferred_element_type=F32)
         for c in range(0, x.shape[-1], LANES)], axis=-1)
    return x * lax.rsqrt(ms + EPS) * g


def _rope(x, cos, sin_signed):
    width = x.shape[-1]
    lane = lax.broadcasted_iota(jnp.int32, x.shape, 1)
    partner = jnp.where((lane & 16) == 0, pltpu.roll(x, width - 16, 1), pltpu.roll(x, 16, 1))
    return x * cos + partner * sin_signed


def _gqa_attention(q, keys, vals, sink_ref, rows, bias):
    kvs = [slice(j * HEAD_DIM, (j + 1) * HEAD_DIM) for j in range(N_KV_HEADS)]
    scores = []
    for j in range(N_KV_HEADS):
        qs = jnp.concatenate([q[:, (j * GQA_GROUP + g) * HEAD_DIM:(j * GQA_GROUP + g + 1) * HEAD_DIM]
                              for g in range(GQA_GROUP)], axis=0)
        scores.append(_bdot_nt(qs, keys[:, kvs[j]]))
    probs, denoms = [], []
    for j in range(N_KV_HEADS):
        p_rows = []
        for g in range(GQA_GROUP):
            s = scores[j][g * rows:(g + 1) * rows]
            if bias is not None:
                s = s + bias
            sink = sink_ref[j * GQA_GROUP + g] * LOG2E
            m = jnp.maximum(jnp.max(s, axis=-1, keepdims=True), sink)
            p = jnp.exp2(s - m)
            denoms.append(jnp.sum(p, axis=-1, keepdims=True) + jnp.exp2(sink - m))
            p_rows.append(p.astype(BF16))
        probs.append(jnp.concatenate(p_rows, axis=0))
    outs = []
    for j in range(N_KV_HEADS):
        o = jnp.dot(probs[j], vals[:, kvs[j]], preferred_element_type=F32)
        outs.extend(o[g * rows:(g + 1) * rows] / denoms[j * GQA_GROUP + g] for g in range(GQA_GROUP))
    return jnp.concatenate(outs, axis=-1)


def _ctx_attn_kernel(sink_ref, q_ref, k_ref, v_ref, o_ref):
    q = q_ref[0]
    o_ref[0] = _gqa_attention(q, k_ref[0].astype(BF16), v_ref[0].astype(BF16), sink_ref, q.shape[0], None)


def _context_attention(q, k, v, sink):
    b, L, _ = q.shape
    seq = lambda n: pl.BlockSpec((1, L, n), lambda bi: (bi, 0, 0))
    return pl.pallas_call(
        _ctx_attn_kernel,
        grid=(b,),
        in_specs=[pl.BlockSpec(memory_space=pltpu.SMEM), seq(ATTN_WIDTH), seq(KV_WIDTH), seq(KV_WIDTH)],
        out_specs=seq(ATTN_WIDTH),
        out_shape=jax.ShapeDtypeStruct((b, L, ATTN_WIDTH), F32),
        compiler_params=_params("parallel"),
        name="context_attention",
    )(sink, q, k, v)


Q_BLOCKS = 2
LOCAL_BLOCKS = Q_BLOCKS + 2


def _lat_attn_kernel(sink_ref, q_ref, *refs):
    k_refs, v_refs = refs[:LOCAL_BLOCKS], refs[LOCAL_BLOCKS:2 * LOCAL_BLOCKS]
    kc_ref, vc_ref, band_ref, o_ref = refs[2 * LOCAL_BLOCKS:]
    keys = jnp.concatenate([r[0] for r in k_refs] + [kc_ref[0, 0]], axis=0)
    vals = jnp.concatenate([r[0] for r in v_refs] + [vc_ref[0, 0]], axis=0)
    o_ref[0] = _gqa_attention(q_ref[0], keys, vals, sink_ref, Q_BLOCKS * BLOCK, band_ref[0])


def _latent_attention(q, k, v, cache_k, cache_v, layer, sink):
    b, L, _ = q.shape
    past = cache_k.shape[2]
    nb = L // BLOCK
    steps = nb // Q_BLOCKS
    assert nb % Q_BLOCKS == 0 and steps >= 2
    rows = lambda n: pl.BlockSpec((1, Q_BLOCKS * BLOCK, n), lambda bi, i: (bi, i, 0))
    local = [pl.BlockSpec((1, BLOCK, KV_WIDTH),
                          functools.partial(lambda bi, i, j: (bi, jnp.clip(Q_BLOCKS * i - 1 + j, 0, nb - 1), 0), j=j))
             for j in range(LOCAL_BLOCKS)]
    cache = pl.BlockSpec((1, 1, past, KV_WIDTH), lambda bi, i: (bi, layer, 0, 0))
    band_spec = pl.BlockSpec((1, Q_BLOCKS * BLOCK, LOCAL_BLOCKS * BLOCK + past),
                             lambda bi, i: (jnp.minimum(i, 1) + jnp.maximum(i - (steps - 2), 0), 0, 0))
    return pl.pallas_call(
        _lat_attn_kernel,
        grid=(b, steps),
        in_specs=[pl.BlockSpec(memory_space=pltpu.SMEM), rows(ATTN_WIDTH)] + local + local + [cache, cache, band_spec],
        out_specs=rows(ATTN_WIDTH),
        out_shape=jax.ShapeDtypeStruct((b, L, ATTN_WIDTH), F32),
        compiler_params=_params("parallel", "parallel"),
        name="latent_attention",
    )(sink, q, *([k] * LOCAL_BLOCKS), *([v] * LOCAL_BLOCKS), cache_k, cache_v, _band_bias(past))


def _band_bias(past):
    n_local = LOCAL_BLOCKS * BLOCK
    r = jnp.arange(Q_BLOCKS * BLOCK)[:, None]
    col = jnp.arange(n_local + past)[None, :]
    band = (jnp.abs(col - BLOCK - r) <= WINDOW) | (col >= n_local)
    first = band & (col >= BLOCK)
    last = band & ((col < n_local - BLOCK) | (col >= n_local))
    return jnp.where(jnp.stack([first, band, last]), 0.0, NEG_INF).astype(F32)


def _rope_tables(L):
    t = jnp.arange(L)
    row = (t // GRID_W).astype(F32)
    col = (t % GRID_W).astype(F32)
    nf = HEAD_DIM // 4
    inv = ROPE_BASE ** (-jnp.arange(nf, dtype=F32) / nf)
    ang_r = row[:, None] * inv[None, :]
    ang_c = col[:, None] * inv[None, :]
    cos = jnp.concatenate([jnp.cos(ang_r), jnp.cos(ang_r), jnp.cos(ang_c), jnp.cos(ang_c)], axis=-1)
    sin = jnp.concatenate([-jnp.sin(ang_r), jnp.sin(ang_r), -jnp.sin(ang_c), jnp.sin(ang_c)], axis=-1)
    return jnp.tile(cos, (1, N_Q_HEADS)), jnp.tile(sin, (1, N_Q_HEADS))


def _block_mean():
    i = jnp.arange(LANES) // HEAD_DIM
    return jnp.where(i[:, None] == i[None, :], 1.0 / HEAD_DIM, 0.0).astype(BF16)


SHIFT_K = 2 * LANES

def _ssd_conv_chunk(xbc_ref, xp_ref, xn_ref, cw_ref, cb_ref, shift_ref, has_prev, has_next):
    main = xbc_ref[0]
    rows = jnp.concatenate([jnp.where(has_prev, xp_ref[0], 0.0), main, jnp.where(has_next, xn_ref[0], 0.0),
                            jnp.zeros((SHIFT_K - CHUNK - 2 * HALO, CONV_CH), F32)], axis=0).astype(BF16)
    taps = jnp.dot(shift_ref[...], rows, preferred_element_type=F32)
    acc = cb_ref[...] + main * cw_ref[CONV_W // 2:CONV_W // 2 + 1, :]
    for i, kk in enumerate(k for k in range(CONV_W) if k != CONV_W // 2):
        acc = acc + taps[i * CHUNK:(i + 1) * CHUNK] * cw_ref[kk:kk + 1, :]
    return _silu(acc)


def _split3(x):
    hi = x.astype(BF16)
    rest = x - hi.astype(F32)
    mid = rest.astype(BF16)
    return hi, mid, (rest - mid.astype(F32)).astype(BF16)


SSD_END_ROWS = (CHUNK - 1, 0)


def _ssd_decay_table_stages(dtt_refs, tri_ref, biast_ref, alogt_ref, tabt_ref, terms_ref):
    steps = []
    for d, dtt_ref in enumerate(dtt_refs):
        dtt = _softplus(dtt_ref[0, 0] + biast_ref[d])
        steps.append((dtt, _split3(dtt * (-jnp.exp(alogt_ref[d]) * LOG2E))))
    yield
    sums = []
    for d, (_, parts_t) in enumerate(steps):
        sums.append(sum(jnp.dot(part, tri_ref[1 - d], preferred_element_type=F32) for part in parts_t))
    yield
    pad = jnp.zeros((CHUNK - 3 * SSD_HEADS, CHUNK), F32)
    terms = [jnp.concatenate([p.astype(F32) for p in _split3(acum_t)] + [pad], axis=0).T.astype(BF16)
             for acum_t in sums]
    yield
    for d, ((dtt, _), acum_t) in enumerate(zip(steps, sums)):
        last_t = acum_t[:, SSD_END_ROWS[d]:SSD_END_ROWS[d] + 1]
        terms_ref[d] = terms[d]
        tabt_ref[d, 0] = acum_t
        tabt_ref[d, 1] = dtt
        tabt_ref[d, 2] = jnp.exp2(acum_t)
        tabt_ref[d, 3] = jnp.exp2(last_t - acum_t) * dtt


def _ssd_load_decay_tables(operands, tri_ref, spread_ref, tabt_ref, terms_ref):
    a_cols = jnp.dot(terms_ref[...].reshape(2 * CHUNK, CHUNK), spread_ref[...], preferred_element_type=F32)
    tables = []
    for d, (xs_t, bc) in enumerate(operands):
        acum_t = tabt_ref[d, 0]
        end = SSD_END_ROWS[d]
        tables.append({"bc": bc, "xs_t": xs_t, "mask": tri_ref[d] > 0, "acum_t": acum_t,
                       "a_col": a_cols[d * CHUNK:(d + 1) * CHUNK],
                       "dt_t": tabt_ref[d, 1], "decay_in_t": tabt_ref[d, 2], "coef_t": tabt_ref[d, 3],
                       "chunk_decay_t": jnp.exp2(acum_t[:, end:end + 1])})
    return tables


BACKGROUND_POINTS = ((0, 0), (0, 2), (1, 0))


def _ssd_scan_chunks(scans, background):
    zero = jnp.zeros((SSD_HEADDIM, CHUNK), BF16)
    ys = [[] for _ in scans]
    next(background, None)
    for g in range(SSD_GROUPS):
        hs = slice(g * HEADS_PER_GROUP * SSD_HEADDIM, (g + 1) * HEADS_PER_GROUP * SSD_HEADDIM)
        heads = range(g * HEADS_PER_GROUP, (g + 1) * HEADS_PER_GROUP)
        stage = []
        for t, s_ref in scans:
            bm = t["bc"][:, g * D_STATE:(g + 1) * D_STATE].astype(BF16)
            cm = t["bc"][:, (SSD_GROUPS + g) * D_STATE:(SSD_GROUPS + g + 1) * D_STATE].astype(BF16)
            cbm = jnp.where(t["mask"], _bdot_nt(cm, bm), 0.0)
            state = s_ref[hs, :]
            y_off = _bdot_nt(state, cm)
            y_off = jnp.concatenate(
                [y_off[(h - heads[0]) * SSD_HEADDIM:(h - heads[0] + 1) * SSD_HEADDIM] * t["decay_in_t"][h:h + 1, :]
                 for h in heads], axis=0)
            stage.append((bm, cbm, state, y_off, []))
        for pair in range(HEADS_PER_GROUP // 2):
            if (g, pair) in BACKGROUND_POINTS:
                next(background, None)
            for (t, _), (_, cbm, _, _, y_diag) in zip(scans, stage):
                ws, xs = [], []
                for h in (heads[0] + 2 * pair, heads[0] + 2 * pair + 1):
                    a_col = t["a_col"][:, h * CHUNK:(h + 1) * CHUNK]
                    ws.append((cbm * jnp.exp2(jnp.minimum(a_col - t["acum_t"][h:h + 1, :], 0.0))).astype(BF16))
                    xs.append((t["xs_t"][h * SSD_HEADDIM:(h + 1) * SSD_HEADDIM, :] * t["dt_t"][h:h + 1, :]).astype(BF16))
                lhs = jnp.concatenate([jnp.concatenate([xs[0], zero], axis=1),
                                       jnp.concatenate([zero, xs[1]], axis=1)], axis=0)
                y_diag.append(_bdot_nt(lhs, jnp.concatenate(ws, axis=1)))
        for i, ((t, s_ref), (bm, _, state, y_off, y_diag)) in enumerate(zip(scans, stage)):
            ys[i].append(jnp.concatenate(y_diag, axis=0) + y_off)
            x_state = jnp.concatenate(
                [(t["xs_t"][h * SSD_HEADDIM:(h + 1) * SSD_HEADDIM, :] * t["coef_t"][h:h + 1, :]).astype(BF16)
                 for h in heads], axis=0)
            update = jnp.dot(x_state, bm, preferred_element_type=F32)
            for h in heads:
                hp = slice(h * SSD_HEADDIM, (h + 1) * SSD_HEADDIM)
                rel = slice((h - heads[0]) * SSD_HEADDIM, (h - heads[0] + 1) * SSD_HEADDIM)
                s_ref[hp, :] = state[rel] * t["chunk_decay_t"][h:h + 1, :] + update[rel]
    return [jnp.concatenate(y, axis=0) for y in ys]


def _ssd_kernel(xf_ref, xfp_ref, xfn_ref, xb_ref, xbp_ref, xbn_ref, dtt0f_ref, dtt0b_ref,
                dttnf_ref, dttnb_ref, init_ref, tri_ref, spread_ref, shift_ref, cw_ref, cb_ref, biast_ref,
                alogt_ref, dskip_ref, o_ref, fin_ref, xst_ref, bc_ref, sf_ref, sb_ref, tabt_ref, terms_ref,
                *, nc, has_init):
    s = pl.program_id(1)
    half = nc // 2
    first_half = s < half
    cf = s
    cb = nc - 1 - s
    rf = pl.multiple_of(cf * CHUNK, CHUNK)
    rb = pl.multiple_of(cb * CHUNK, CHUNK)

    @pl.when(first_half)
    def _():
        for c, rows, refs in ((cf, rf, (xf_ref, xfp_ref, xfn_ref)), (cb, rb, (xb_ref, xbp_ref, xbn_ref))):
            xc = _ssd_conv_chunk(*refs, cw_ref, cb_ref, shift_ref, c > 0, c < nc - 1)
            xst_ref[c] = xc[:, 0:SSD_INNER].T
            bc_ref[pl.ds(rows, CHUNK), :] = xc[:, SSD_INNER:]

    table_refs = (tri_ref, biast_ref, alogt_ref, tabt_ref, terms_ref)

    @pl.when(s == 0)
    def _():
        if has_init:
            sf_ref[...] = init_ref[0, 0, 0]
            sb_ref[...] = init_ref[0, 0, 1]
        else:
            sf_ref[...] = jnp.zeros(sf_ref.shape, F32)
            sb_ref[...] = jnp.zeros(sb_ref.shape, F32)
        for _ in _ssd_decay_table_stages((dtt0f_ref, dtt0b_ref), *table_refs):
            pass

    tf, tb = _ssd_load_decay_tables([(xst_ref[cf], bc_ref[pl.ds(rf, CHUNK), :]),
                                     (xst_ref[cb], bc_ref[pl.ds(rb, CHUNK), :])],
                                    tri_ref, spread_ref, tabt_ref, terms_ref)
    next_tables = _ssd_decay_table_stages((dttnf_ref, dttnb_ref), *table_refs)
    yf_t, yb_t = _ssd_scan_chunks([(tf, sf_ref), (tb, sb_ref)], next_tables)
    for _ in next_tables:
        pass
    yf = (yf_t + dskip_ref[...] * tf["xs_t"]).T
    yb = yb_t.T

    @pl.when(first_half)
    def _():
        o_ref[0, pl.ds(rf, CHUNK), :] = yf
        o_ref[0, pl.ds(rb, CHUNK), :] = yb

    @pl.when(jnp.logical_not(first_half))
    def _():
        o_ref[0, pl.ds(rf, CHUNK), :] = o_ref[0, pl.ds(rf, CHUNK), :] + yf
        o_ref[0, pl.ds(rb, CHUNK), :] = o_ref[0, pl.ds(rb, CHUNK), :] + yb

    @pl.when(s == nc - 1)
    def _():
        fin_ref[0, 0] = sf_ref[...]
        fin_ref[0, 1] = sb_ref[...]


def _ssd_mixer(xbc, dt_raw, init_state, layer, consts, p):
    b, L, _ = xbc.shape
    nc = L // CHUNK
    has_init = init_state is not None
    dt_t = jnp.stack([dt_raw[:, :, 0:SSD_HEADS], dt_raw[:, :, SSD_HEADS:2 * SSD_HEADS]], axis=1)
    dt_t = jnp.swapaxes(dt_t, 2, 3)
    assert nc % 2 == 0
    half = nc // 2
    if not has_init:
        init_state = jnp.zeros((1, 1, 1, HALO, LANES), F32)
        init_spec = pl.BlockSpec((1, 1, 1, HALO, LANES), lambda bi, s: (0, 0, 0, 0, 0))
    else:
        init_spec = pl.BlockSpec((1, 1, 2, SSD_INNER, D_STATE), lambda bi, s: (bi, layer, 0, 0, 0))
    per = CHUNK // HALO
    last = L // HALO - 1
    conv_f = lambda s: jnp.minimum(s, half - 1)
    conv_b = lambda s: jnp.maximum(nc - 1 - s, half)

    def conv_specs(chunk):
        return [pl.BlockSpec((1, CHUNK, CONV_CH), lambda bi, s: (bi, chunk(s), 0)),
                pl.BlockSpec((1, HALO, CONV_CH), lambda bi, s: (bi, jnp.maximum(chunk(s) * per - 1, 0), 0)),
                pl.BlockSpec((1, HALO, CONV_CH), lambda bi, s: (bi, jnp.minimum((chunk(s) + 1) * per, last), 0))]

    in_specs = conv_specs(conv_f) + conv_specs(conv_b) + [
        pl.BlockSpec((1, 1, SSD_HEADS, CHUNK), lambda bi, s: (bi, 0, 0, 0)),
        pl.BlockSpec((1, 1, SSD_HEADS, CHUNK), lambda bi, s: (bi, 1, 0, nc - 1)),
        pl.BlockSpec((1, 1, SSD_HEADS, CHUNK), lambda bi, s: (bi, 0, 0, jnp.minimum(s + 1, nc - 1))),
        pl.BlockSpec((1, 1, SSD_HEADS, CHUNK), lambda bi, s: (bi, 1, 0, jnp.maximum(nc - 2 - s, 0))),
        init_spec,
        _const_spec((2, CHUNK, CHUNK)), _const_spec((CHUNK, SSD_HEADS * CHUNK)),
        _const_spec(((CONV_W - 1) * CHUNK, SHIFT_K)), _const_spec((CONV_W, CONV_CH)), _const_spec((1, CONV_CH)),
        _const_spec((2, SSD_HEADS, 1)), _const_spec((2, SSD_HEADS, 1)), _const_spec((SSD_INNER, CHUNK)),
    ]
    out, fin = pl.pallas_call(
        functools.partial(_ssd_kernel, nc=nc, has_init=has_init),
        grid=(b, nc),
        in_specs=in_specs,
        out_specs=[pl.BlockSpec((1, L, SSD_INNER), lambda bi, s: (bi, 0, 0)),
                   pl.BlockSpec((1, 2, SSD_INNER, D_STATE), lambda bi, s: (bi, 0, 0, 0))],
        out_shape=[jax.ShapeDtypeStruct((b, L, SSD_INNER), F32),
                   jax.ShapeDtypeStruct((b, 2, SSD_INNER, D_STATE), F32)],
        scratch_shapes=[pltpu.VMEM((nc, SSD_INNER, CHUNK), F32), pltpu.VMEM((L, CONV_CH - SSD_INNER), F32),
                        pltpu.VMEM((SSD_INNER, D_STATE), F32), pltpu.VMEM((SSD_INNER, D_STATE), F32),
                        pltpu.VMEM((2, 4, SSD_HEADS, CHUNK), F32), pltpu.VMEM((2, CHUNK, CHUNK), BF16)],
        compiler_params=_params("parallel", "arbitrary"),
        name="ssd_mixer",
    )(xbc, xbc, xbc, xbc, xbc, xbc, dt_t, dt_t, dt_t, dt_t, init_state, consts["tri"], consts["spread"],
      consts["shift"], p["conv_w"], p["conv_b"], p["dt_bias_t"], p["a_log_t"], p["d_skip"])
    return out, fin


def _ssd_consts():
    i = jnp.arange(CHUNK)
    lower = (i[None, :] <= i[:, None]).astype(F32)
    t = jnp.arange(CHUNK)[:, None]
    col = jnp.arange(SHIFT_K)[None, :]
    shift = jnp.concatenate([(col == HALO + t + kk - CONV_W // 2) for kk in range(CONV_W) if kk != CONV_W // 2], axis=0)
    k = jnp.arange(CHUNK)[:, None]
    lane_head = jnp.arange(SSD_HEADS * CHUNK)[None, :] // CHUNK
    spread = (k < 3 * SSD_HEADS) & (k % SSD_HEADS == lane_head)
    return {"tri": jnp.stack([lower, lower.T]).astype(BF16), "shift": shift.astype(BF16),
            "spread": spread.astype(BF16)}


FF_CHUNK = 1024


def _merge_mlp_kernel(x_ref, mod_ref, pool_ref, attn_ref, ssd_ref, z_ref, gate_ref, g_ref, sg_ref,
                      wp_ref, wa_ref, ws_ref, wo_ref, w1_ref, w2_ref, o_ref):
    mod = mod_ref[0]
    pool_o = _bdot(pool_ref[0], wp_ref[...])
    ssd = ssd_ref[0] * _silu(z_ref[0])
    ssd = ssd * lax.rsqrt(jnp.mean(ssd * ssd, axis=-1, keepdims=True) + EPS) * sg_ref[...]
    attn_o = _bdot(attn_ref[0], wa_ref[...])
    ssd_o = _bdot(ssd, ws_ref[...])
    gates = _sigmoid(gate_ref[0])
    merged = (gates[:, 0:D_MODEL] * pool_o + gates[:, D_MODEL:2 * D_MODEL] * attn_o
              + gates[:, 2 * D_MODEL:] * ssd_o)
    x = x_ref[0] + mod[:, 2 * D_MODEL:3 * D_MODEL] * _bdot(merged, wo_ref[...])
    h = _rms_mod(x, g_ref[...], mod[:, 4 * D_MODEL:5 * D_MODEL], mod[:, 3 * D_MODEL:4 * D_MODEL]).astype(BF16)
    acc = jnp.zeros(x.shape, F32)
    for j in range(D_FF // FF_CHUNK):
        ff = slice(j * FF_CHUNK, (j + 1) * FF_CHUNK)
        a = jnp.dot(h, w1_ref[:, ff], preferred_element_type=F32)
        acc = acc + _bdot(jnp.square(jnp.maximum(a, 0.0)), w2_ref[ff, :])
    o_ref[0] = x + mod[:, 5 * D_MODEL:] * acc


def _merge_mlp(x, mods, layer, per_batch, pool, attn, ssd, z, gate, norm_g, ssd_norm_g, w):
    b, L, _ = x.shape
    tm = TOKEN_TILE
    tok = lambda n: pl.BlockSpec((1, tm, n), lambda bi, i: (bi, i, 0))
    weights = [w["pool_o"], w["attn_o"], w["ssd_o"], w["out"], w["mlp1"], w["mlp2"]]
    return pl.pallas_call(
        _merge_mlp_kernel,
        grid=(b, L // tm),
        in_specs=[tok(D_MODEL), _mod_spec(layer, per_batch), tok(POOL_WIDTH), tok(ATTN_WIDTH), tok(SSD_INNER),
                  tok(SSD_INNER), tok(N_BRANCH * D_MODEL), _const_spec((1, D_MODEL)), _const_spec((1, SSD_INNER))]
                 + [_const_spec(a.shape) for a in weights],
        out_specs=tok(D_MODEL),
        out_shape=jax.ShapeDtypeStruct((b, L, D_MODEL), F32),
        compiler_params=_params("parallel", "parallel"),
        name="merge_mlp",
    )(x, mods, pool, attn, ssd, z, gate, norm_g, ssd_norm_g, *weights)


def _layer(x, mods, layer, per_batch, w, p, consts, ctx):
    u, q, k, v, z, xbc, dt_raw, gate = _in_projection(
        x, mods, layer, per_batch, p["norm1_g"], p["q_norm_g"], p["k_norm_g"], consts["bm"],
        None if ctx is None else consts["rope"], w)
    pool = _pool_mixer(u, w["pool_w"], p["pool_scale"])
    if ctx is None:
        attn = _context_attention(q, k, v, p["attn_sink"])
        init = None
    else:
        cache_k, cache_v, init = ctx
        attn = _latent_attention(q, k, v, cache_k, cache_v, layer, p["attn_sink"])
    ssd, fin = _ssd_mixer(xbc, dt_raw, init, layer, consts, p)
    x = _merge_mlp(x, mods, layer, per_batch, pool, attn, ssd, z, gate, p["norm2_g"], p["ssd_norm_g"], w)
    return x, (k, v, fin)


def _layer_weights(l, w_in, pool_w, w_pool_o, w_attn_o, w_ssd_o, w_out, w_mlp1, w_mlp2):
    wi = w_in[l]
    o_z = POOL_WIDTH + ATTN_WIDTH + 2 * KV_WIDTH
    o_x = o_z + SSD_INNER
    o_d = o_x + CONV_CH
    o_g = o_d + 2 * SSD_HEADS
    wd = jnp.pad(wi[:, o_d:o_g], ((0, 0), (0, DT_PAD - 2 * SSD_HEADS)))
    cast = lambda a: a.astype(BF16)
    return {"a": cast(wi[:, :o_z]), "z": cast(wi[:, o_z:o_x]), "xbc": cast(wi[:, o_x:o_d]), "dt": cast(wd),
            "gate": cast(wi[:, o_g:]), "pool_w": cast(pool_w[l]), "pool_o": cast(w_pool_o[l]),
            "attn_o": cast(w_attn_o[l]), "ssd_o": cast(w_ssd_o[l]), "out": cast(w_out[l]),
            "mlp1": cast(w_mlp1[l]), "mlp2": cast(w_mlp2[l])}


def kernel(x_prompt, x_sample, cache_k, cache_v, state_ssd, c, c_ctx, w_mod, b_mod, norm1_g, norm2_g, w_in, pool_w, pool_scale, w_pool_o, q_norm_g, k_norm_g, attn_sink, w_attn_o, conv_w, conv_b, dt_bias, a_log, d_skip, ssd_norm_g, w_ssd_o, w_out, w_mlp1, w_mlp2):
    batch, seq, _ = x_prompt.shape
    dec_batch, dec_seq, _ = x_sample.shape
    past = cache_k.shape[2]
    assert 1 + dec_batch <= MOD_ROWS

    cvecs = jnp.concatenate([c_ctx[None, :], c, jnp.zeros((MOD_ROWS - 1 - dec_batch, D_MODEL), F32)], axis=0)
    mods = _modulation(cvecs, w_mod, b_mod)

    consts = {"bm": _block_mean(), "rope": _rope_tables(dec_seq)}
    consts.update(_ssd_consts())
    cache_k = cache_k.reshape(dec_batch, DEPTH, past, KV_WIDTH).astype(BF16)
    cache_v = cache_v.reshape(dec_batch, DEPTH, past, KV_WIDTH).astype(BF16)
    state = state_ssd.reshape(dec_batch, DEPTH, 2, SSD_INNER, D_STATE)

    y_prompt, y_sample = x_prompt, x_sample
    ks, vs, ss = [], [], []
    for l in range(DEPTH):
        w = _layer_weights(l, w_in, pool_w, w_pool_o, w_attn_o, w_ssd_o, w_out, w_mlp1, w_mlp2)
        p = {"norm1_g": norm1_g[l][None, :], "norm2_g": norm2_g[l][None, :], "pool_scale": pool_scale[l][None, :],
             "q_norm_g": jnp.tile(q_norm_g[l], N_Q_HEADS)[None, :], "k_norm_g": jnp.tile(k_norm_g[l], N_KV_HEADS)[None, :],
             "attn_sink": attn_sink[l], "conv_w": conv_w[l], "conv_b": conv_b[l][None, :],
             "dt_bias_t": dt_bias[l][:, :, None], "a_log_t": a_log[l][:, :, None],
             "d_skip": jnp.broadcast_to(jnp.repeat(d_skip[l], SSD_HEADDIM)[:, None], (SSD_INNER, CHUNK)),
             "ssd_norm_g": ssd_norm_g[l][None, :]}
        y_prompt, (k_l, v_l, s_l) = _layer(y_prompt, mods, l, False, w, p, consts, None)
        ks.append(k_l.reshape(batch, seq, N_KV_HEADS, HEAD_DIM))
        vs.append(v_l.reshape(batch, seq, N_KV_HEADS, HEAD_DIM))
        ss.append(s_l.reshape(batch, 2, SSD_HEADS, SSD_HEADDIM, D_STATE))
        y_sample, _ = _layer(y_sample, mods, l, True, w, p, consts, (cache_k, cache_v, state))
    return (y_prompt, y_sample, jnp.stack(ks, axis=1), jnp.stack(vs, axis=1), jnp.stack(ss, axis=1))
```

```python
import functools
import math

import jax
import jax.numpy as jnp
from jax import lax
from jax.experimental import pallas as pl
from jax.experimental.pallas import tpu as pltpu

F32 = jnp.float32
BF16 = jnp.bfloat16

D_MODEL = 1024
DEPTH = 2
GRID_W = 64
MOD_CHUNKS = 6
EPS = 1e-6
POOL_WIDTH = D_MODEL // 2
POOL_GROUPS = 4
POOL_GROUP_W = POOL_WIDTH // POOL_GROUPS
POOL_WINDOWS = (2, 4, 8, 16)
N_Q_HEADS = 8
N_KV_HEADS = 2
GQA_GROUP = N_Q_HEADS // N_KV_HEADS
HEAD_DIM = 64
ATTN_WIDTH = N_Q_HEADS * HEAD_DIM
KV_WIDTH = N_KV_HEADS * HEAD_DIM
WINDOW = 128
BLOCK = 128
ROPE_BASE = 10000.0
ATTN_SCALE = HEAD_DIM ** -0.5
SSD_HEADS = 16
SSD_HEADDIM = 64
SSD_INNER = SSD_HEADS * SSD_HEADDIM
SSD_GROUPS = 2
HEADS_PER_GROUP = SSD_HEADS // SSD_GROUPS
D_STATE = 128
CONV_W = 5
CHUNK = 128
CONV_CH = SSD_INNER + 2 * SSD_GROUPS * D_STATE
N_BRANCH = 3
D_FF = 4 * D_MODEL

LANES = 128
HALO = 8
MOD_ROWS = 16
VMEM_LIMIT = 56 * 1024 * 1024
TOKEN_TILE = 256
NEG_INF = float("-inf")
LOG2E = math.log2(math.e)
SCORE_SCALE = ATTN_SCALE * LOG2E


def _params(*sem):
    return pltpu.CompilerParams(dimension_semantics=sem, vmem_limit_bytes=VMEM_LIMIT)


def _const_spec(shape):
    nd = len(shape)
    return pl.BlockSpec(shape, lambda *_: (0,) * nd, pipeline_mode=pl.Buffered(1))


def _bdot(a, b):
    return jnp.dot(a.astype(BF16), b.astype(BF16), preferred_element_type=F32)


def _bdot_nt(a, b):
    return lax.dot_general(a.astype(BF16), b.astype(BF16), (((1,), (1,)), ((), ())),
                           preferred_element_type=F32)


def _silu(x):
    half = 0.5 * x
    return half * jnp.tanh(half) + half


def _softplus(x):
    return jnp.maximum(x, 0.0) + jnp.log1p(jnp.exp(-jnp.abs(x)))


def _rms_mod(x, g, scale, shift):
    ms = jnp.mean(x * x, axis=-1, keepdims=True)
    return (x * lax.rsqrt(ms + EPS)) * g * (1.0 + scale) + shift


def _mod_kernel(c_ref, w_ref, b_ref, o_ref):
    o_ref[0] = _bdot(_silu(c_ref[...]), w_ref[0]) + b_ref[0]


def _modulation(cvecs, w_mod, b_mod):
    n = MOD_CHUNKS * D_MODEL
    tn = n // 4
    out = pl.pallas_call(
        _mod_kernel,
        grid=(DEPTH, n // tn),
        in_specs=[pl.BlockSpec((MOD_ROWS, D_MODEL), lambda l, j: (0, 0)),
                  pl.BlockSpec((1, D_MODEL, tn), lambda l, j: (l, 0, j)),
                  pl.BlockSpec((1, 1, tn), lambda l, j: (l, 0, j))],
        out_specs=pl.BlockSpec((1, MOD_ROWS, tn), lambda l, j: (l, 0, j)),
        out_shape=jax.ShapeDtypeStruct((DEPTH, MOD_ROWS, n), F32),
        compiler_params=_params("arbitrary", "arbitrary"),
        name="modulation",
    )(cvecs, w_mod, b_mod.reshape(DEPTH, 1, n))
    return out.reshape(DEPTH * MOD_ROWS, 1, n)


def _mod_spec(layer, per_batch):
    base = layer * MOD_ROWS
    if per_batch:
        return pl.BlockSpec((1, 1, MOD_CHUNKS * D_MODEL), lambda b, i: (base + 1 + b, 0, 0))
    return pl.BlockSpec((1, 1, MOD_CHUNKS * D_MODEL), lambda b, i: (base, 0, 0))


DT_PAD = LANES


def _inproj_kernel(*refs, rope):
    x_ref, mod_ref, g_ref, qg_ref, kg_ref, bm_ref = refs[:6]
    refs = refs[6:]
    if rope:
        cos_ref, sin_ref = refs[:2]
        refs = refs[2:]
    wa_ref, wz_ref, wx_ref, wd_ref, wg_ref, u_ref, q_ref, k_ref, v_ref, z_ref, xbc_ref, dt_ref, gate_ref = refs
    mod = mod_ref[0]
    tm = x_ref.shape[1]
    for r0 in range(0, tm, TOKEN_TILE):
        rows = slice(r0, r0 + TOKEN_TILE)
        h = _rms_mod(x_ref[0, rows, :], g_ref[...], mod[:, D_MODEL:2 * D_MODEL], mod[:, 0:D_MODEL]).astype(BF16)
        a = jnp.dot(h, wa_ref[...], preferred_element_type=F32)
        u_ref[0, rows, :] = a[:, 0:POOL_WIDTH]
        v_ref[0, rows, :] = a[:, POOL_WIDTH + ATTN_WIDTH + KV_WIDTH:].astype(v_ref.dtype)
        z_ref[0, rows, :] = jnp.dot(h, wz_ref[...], preferred_element_type=F32)
        qn = _head_rms(a[:, POOL_WIDTH:POOL_WIDTH + ATTN_WIDTH], bm_ref[...], qg_ref[...])
        kn = _head_rms(a[:, POOL_WIDTH + ATTN_WIDTH:POOL_WIDTH + ATTN_WIDTH + KV_WIDTH], bm_ref[...], kg_ref[...])
        xbc_ref[0, rows, :] = jnp.dot(h, wx_ref[...], preferred_element_type=F32)
        if rope:
            qn = _rope(qn, cos_ref[rows, :], sin_ref[rows, :])
            kn = _rope(kn, cos_ref[rows, 0:KV_WIDTH], sin_ref[rows, 0:KV_WIDTH])
        q_ref[0, rows, :] = (qn * SCORE_SCALE).astype(q_ref.dtype)
        k_ref[0, rows, :] = kn.astype(k_ref.dtype)
        dt_ref[0, rows, :] = jnp.dot(h, wd_ref[...], preferred_element_type=F32)
        gate_ref[0, rows, :] = jnp.dot(h, wg_ref[...], preferred_element_type=F32)


def _in_projection(x, mods, layer, per_batch, norm_g, qg, kg, bm, rope_tabs, w):
    b, L, _ = x.shape
    tm = min(2 * TOKEN_TILE, L)
    rope = rope_tabs is not None
    kv_dtype = BF16 if rope else F32
    outs = ((POOL_WIDTH, F32), (ATTN_WIDTH, BF16), (KV_WIDTH, kv_dtype), (KV_WIDTH, kv_dtype), (SSD_INNER, F32),
            (CONV_CH, F32), (DT_PAD, F32), (N_BRANCH * D_MODEL, F32))
    tok = lambda n: pl.BlockSpec((1, tm, n), lambda bi, i: (bi, i, 0))
    tab = pl.BlockSpec((tm, ATTN_WIDTH), lambda bi, i: (i, 0))
    weights = [w["a"], w["z"], w["xbc"], w["dt"], w["gate"]]
    return pl.pallas_call(
        functools.partial(_inproj_kernel, rope=rope),
        grid=(b, L // tm),
        in_specs=[tok(D_MODEL), _mod_spec(layer, per_batch), _const_spec((1, D_MODEL)), _const_spec(qg.shape),
                  _const_spec(kg.shape), _const_spec(bm.shape)] + ([tab, tab] if rope else [])
                 + [_const_spec(a.shape) for a in weights],
        out_specs=[tok(n) for n, _ in outs],
        out_shape=[jax.ShapeDtypeStruct((b, L, n), dt) for n, dt in outs],
        compiler_params=_params("parallel", "parallel"),
        name="in_projection",
    )(x, mods, norm_g, qg, kg, bm, *(rope_tabs if rope else ()), *weights)


POOL_ROWS = 256
POOL_K = POOL_ROWS + LANES


def _pool_kernel(u_ref, band_ref, pw_ref, ps_ref, o_ref, *, L):
    halo_zeros = jnp.zeros((HALO, POOL_WIDTH), F32)
    tail_zeros = jnp.zeros((POOL_K - POOL_ROWS - 2 * HALO, POOL_WIDTH), F32)
    groups = [slice(gi * POOL_GROUP_W, (gi + 1) * POOL_GROUP_W) for gi in range(POOL_GROUPS)]
    for r0 in range(0, L, POOL_ROWS):
        main = u_ref[0, r0:r0 + POOL_ROWS, :]
        prev = u_ref[0, r0 - HALO:r0, :] if r0 > 0 else halo_zeros
        nxt = u_ref[0, r0 + POOL_ROWS:r0 + POOL_ROWS + HALO, :] if r0 + POOL_ROWS < L else halo_zeros
        rows = jnp.concatenate([prev, main, nxt, tail_zeros], axis=0)
        hi = rows.astype(BF16)
        lo = (rows - hi.astype(F32)).astype(BF16)
        t = r0 + lax.broadcasted_iota(jnp.int32, (POOL_ROWS, POOL_GROUP_W), 0)
        sums = [jnp.dot(band_ref[gi], hi[:, cols], preferred_element_type=F32)
                + jnp.dot(band_ref[gi], lo[:, cols], preferred_element_type=F32) for gi, cols in enumerate(groups)]
        pooled = []
        for w, cols, acc in zip(POOL_WINDOWS, groups, sums):
            cnt = (jnp.minimum(t + w // 2, L) - jnp.maximum(t - w // 2, 0)).astype(F32)
            pooled.append((acc / cnt - main[:, cols]).astype(BF16))
        mixed = [jnp.dot(pooled[gi], pw_ref[gi], preferred_element_type=F32) for gi in range(POOL_GROUPS)]
        for cols, m in zip(groups, mixed):
            o_ref[0, r0:r0 + POOL_ROWS, cols] = m * ps_ref[:, cols]


def _pool_bands():
    d = jnp.arange(POOL_K)[None, :] - HALO - jnp.arange(POOL_ROWS)[:, None]
    return jnp.stack([(d >= -(w // 2)) & (d < w // 2) for w in POOL_WINDOWS]).astype(BF16)


def _pool_mixer(u, pool_w, pool_scale):
    b, L, _ = u.shape
    assert max(POOL_WINDOWS) // 2 <= HALO and L % POOL_ROWS == 0
    seq = pl.BlockSpec((1, L, POOL_WIDTH), lambda bi: (bi, 0, 0))
    bands = _pool_bands()
    return pl.pallas_call(
        functools.partial(_pool_kernel, L=L),
        grid=(b,),
        in_specs=[seq, _const_spec(bands.shape), _const_spec(pool_w.shape), _const_spec(pool_scale.shape)],
        out_specs=seq,
        out_shape=jax.ShapeDtypeStruct((b, L, POOL_WIDTH), F32),
        compiler_params=_params("parallel"),
        name="pool_mixer",
    )(u, bands, pool_w, pool_scale)


def _head_rms(x, block_mean, g):
    sq = x * x
    hi = sq.astype(BF16)
    lo = (sq - hi.astype(F32)).astype(BF16)
    ms = jnp.concatenate(
        [jnp.dot(hi[:, c:c + LANES], block_mean, preferred_element_type=F32)
         + jnp.dot(lo[:, c:c + LANES], block_mean, preferred_element_type=F32)
         for c in range(0, x.shape[-1], LANES)], axis=-1)
    return x * lax.rsqrt(ms + EPS) * g


def _rope(x, cos, sin_signed):
    width = x.shape[-1]
    lane = lax.broadcasted_iota(jnp.int32, x.shape, 1)
    partner = jnp.where((lane & 16) == 0, pltpu.roll(x, width - 16, 1), pltpu.roll(x, 16, 1))
    return x * cos + partner * sin_signed


def _gqa_attention(q, keys, vals, sink_ref, rows, bias):
    kvs = [slice(j * HEAD_DIM, (j + 1) * HEAD_DIM) for j in range(N_KV_HEADS)]
    scores = []
    for j in range(N_KV_HEADS):
        qs = jnp.concatenate([q[:, (j * GQA_GROUP + g) * HEAD_DIM:(j * GQA_GROUP + g + 1) * HEAD_DIM]
                              for g in range(GQA_GROUP)], axis=0)
        scores.append(_bdot_nt(qs, keys[:, kvs[j]]))
    probs, denoms = [], []
    for j in range(N_KV_HEADS):
        p_rows = []
        for g in range(GQA_GROUP):
            s = scores[j][g * rows:(g + 1) * rows]
            if bias is not None:
                s = s + bias
            sink = sink_ref[j * GQA_GROUP + g] * LOG2E
            m = jnp.maximum(jnp.max(s, axis=-1, keepdims=True), sink)
            p = jnp.exp2(s - m)
            denoms.append(jnp.sum(p, axis=-1, keepdims=True) + jnp.exp2(sink - m))
            p_rows.append(p.astype(BF16))
        probs.append(jnp.concatenate(p_rows, axis=0))
    outs = []
    for j in range(N_KV_HEADS):
        o = jnp.dot(probs[j], vals[:, kvs[j]], preferred_element_type=F32)
        outs.extend(o[g * rows:(g + 1) * rows] / denoms[j * GQA_GROUP + g] for g in range(GQA_GROUP))
    return jnp.concatenate(outs, axis=-1)


def _ctx_attn_kernel(sink_ref, q_ref, k_ref, v_ref, o_ref):
    q = q_ref[0]
    o_ref[0] = _gqa_attention(q, k_ref[0].astype(BF16), v_ref[0].astype(BF16), sink_ref, q.shape[0], None)


def _context_attention(q, k, v, sink):
    b, L, _ = q.shape
    seq = lambda n: pl.BlockSpec((1, L, n), lambda bi: (bi, 0, 0))
    return pl.pallas_call(
        _ctx_attn_kernel,
        grid=(b,),
        in_specs=[pl.BlockSpec(memory_space=pltpu.SMEM), seq(ATTN_WIDTH), seq(KV_WIDTH), seq(KV_WIDTH)],
        out_specs=seq(ATTN_WIDTH),
        out_shape=jax.ShapeDtypeStruct((b, L, ATTN_WIDTH), F32),
        compiler_params=_params("parallel"),
        name="context_attention",
    )(sink, q, k, v)


Q_BLOCKS = 2
LOCAL_BLOCKS = Q_BLOCKS + 2


def _lat_attn_kernel(sink_ref, q_ref, *refs):
    k_refs, v_refs = refs[:LOCAL_BLOCKS], refs[LOCAL_BLOCKS:2 * LOCAL_BLOCKS]
    kc_ref, vc_ref, band_ref, o_ref = refs[2 * LOCAL_BLOCKS:]
    keys = jnp.concatenate([r[0] for r in k_refs] + [kc_ref[0, 0]], axis=0)
    vals = jnp.concatenate([r[0] for r in v_refs] + [vc_ref[0, 0]], axis=0)
    o_ref[0] = _gqa_attention(q_ref[0], keys, vals, sink_ref, Q_BLOCKS * BLOCK, band_ref[0])


def _latent_attention(q, k, v, cache_k, cache_v, layer, sink):
    b, L, _ = q.shape
    past = cache_k.shape[2]
    nb = L // BLOCK
    steps = nb // Q_BLOCKS
    assert nb % Q_BLOCKS == 0 and steps >= 2
    rows = lambda n: pl.BlockSpec((1, Q_BLOCKS * BLOCK, n), lambda bi, i: (bi, i, 0))
    local = [pl.BlockSpec((1, BLOCK, KV_WIDTH),
                          functools.partial(lambda bi, i, j: (bi, jnp.clip(Q_BLOCKS * i - 1 + j, 0, nb - 1), 0), j=j))
             for j in range(LOCAL_BLOCKS)]
    cache = pl.BlockSpec((1, 1, past, KV_WIDTH), lambda bi, i: (bi, layer, 0, 0))
    band_spec = pl.BlockSpec((1, Q_BLOCKS * BLOCK, LOCAL_BLOCKS * BLOCK + past),
                             lambda bi, i: (jnp.minimum(i, 1) + jnp.maximum(i - (steps - 2), 0), 0, 0))
    return pl.pallas_call(
        _lat_attn_kernel,
        grid=(b, steps),
        in_specs=[pl.BlockSpec(memory_space=pltpu.SMEM), rows(ATTN_WIDTH)] + local + local + [cache, cache, band_spec],
        out_specs=rows(ATTN_WIDTH),
        out_shape=jax.ShapeDtypeStruct((b, L, ATTN_WIDTH), F32),
        compiler_params=_params("parallel", "parallel"),
        name="latent_attention",
    )(sink, q, *([k] * LOCAL_BLOCKS), *([v] * LOCAL_BLOCKS), cache_k, cache_v, _band_bias(past))


def _band_bias(past):
    n_local = LOCAL_BLOCKS * BLOCK
    r = jnp.arange(Q_BLOCKS * BLOCK)[:, None]
    col = jnp.arange(n_local + past)[None, :]
    band = (jnp.abs(col - BLOCK - r) <= WINDOW) | (col >= n_local)
    first = band & (col >= BLOCK)
    last = band & ((col < n_local - BLOCK) | (col >= n_local))
    return jnp.where(jnp.stack([first, band, last]), 0.0, NEG_INF).astype(F32)


def _rope_tables(L):
    t = jnp.arange(L)
    row = (t // GRID_W).astype(F32)
    col = (t % GRID_W).astype(F32)
    nf = HEAD_DIM // 4
    inv = ROPE_BASE ** (-jnp.arange(nf, dtype=F32) / nf)
    ang_r = row[:, None] * inv[None, :]
    ang_c = col[:, None] * inv[None, :]
    cos = jnp.concatenate([jnp.cos(ang_r), jnp.cos(ang_r), jnp.cos(ang_c), jnp.cos(ang_c)], axis=-1)
    sin = jnp.concatenate([-jnp.sin(ang_r), jnp.sin(ang_r), -jnp.sin(ang_c), jnp.sin(ang_c)], axis=-1)
    return jnp.tile(cos, (1, N_Q_HEADS)), jnp.tile(sin, (1, N_Q_HEADS))


def _block_mean():
    i = jnp.arange(LANES) // HEAD_DIM
    return jnp.where(i[:, None] == i[None, :], 1.0 / HEAD_DIM, 0.0).astype(BF16)


SHIFT_K = 2 * LANES

def _ssd_conv_chunk(xbc_ref, xp_ref, xn_ref, cw_ref, cb_ref, shift_ref, has_prev, has_next):
    main = xbc_ref[0]
    rows = jnp.concatenate([jnp.where(has_prev, xp_ref[0], 0.0), main, jnp.where(has_next, xn_ref[0], 0.0),
                            jnp.zeros((SHIFT_K - CHUNK - 2 * HALO, CONV_CH), F32)], axis=0).astype(BF16)
    taps = jnp.dot(shift_ref[...], rows, preferred_element_type=F32)
    acc = cb_ref[...] + main * cw_ref[CONV_W // 2:CONV_W // 2 + 1, :]
    for i, kk in enumerate(k for k in range(CONV_W) if k != CONV_W // 2):
        acc = acc + taps[i * CHUNK:(i + 1) * CHUNK] * cw_ref[kk:kk + 1, :]
    return _silu(acc)


def _split3(x):
    hi = x.astype(BF16)
    rest = x - hi.astype(F32)
    mid = rest.astype(BF16)
    return hi, mid, (rest - mid.astype(F32)).astype(BF16)


SSD_END_ROWS = (CHUNK - 1, 0)


def _ssd_decay_table_stages(dtt_refs, tri_ref, biast_ref, alogt_ref, tabt_ref, terms_ref):
    steps = []
    for d, dtt_ref in enumerate(dtt_refs):
        dtt = _softplus(dtt_ref[0, 0] + biast_ref[d])
        steps.append((dtt, _split3(dtt * (-jnp.exp(alogt_ref[d]) * LOG2E))))
    yield
    sums = []
    for d, (_, parts_t) in enumerate(steps):
        sums.append(sum(jnp.dot(part, tri_ref[1 - d], preferred_element_type=F32) for part in parts_t))
    yield
    pad = jnp.zeros((CHUNK - 3 * SSD_HEADS, CHUNK), F32)
    terms = [jnp.concatenate([p.astype(F32) for p in _split3(acum_t)] + [pad], axis=0).T.astype(BF16)
             for acum_t in sums]
    yield
    for d, ((dtt, _), acum_t) in enumerate(zip(steps, sums)):
        last_t = acum_t[:, SSD_END_ROWS[d]:SSD_END_ROWS[d] + 1]
        terms_ref[d] = terms[d]
        tabt_ref[d, 0] = acum_t
        tabt_ref[d, 1] = dtt
        tabt_ref[d, 2] = jnp.exp2(acum_t)
        tabt_ref[d, 3] = jnp.exp2(last_t - acum_t) * dtt


def _ssd_load_decay_tables(operands, tri_ref, spread_ref, tabt_ref, terms_ref):
    a_cols = jnp.dot(terms_ref[...].reshape(2 * CHUNK, CHUNK), spread_ref[...], preferred_element_type=F32)
    tables = []
    for d, (xs_t, bc) in enumerate(operands):
        acum_t = tabt_ref[d, 0]
        end = SSD_END_ROWS[d]
        tables.append({"bc": bc, "xs_t": xs_t, "mask": tri_ref[d] > 0, "acum_t": acum_t,
                       "a_col": a_cols[d * CHUNK:(d + 1) * CHUNK],
                       "dt_t": tabt_ref[d, 1], "decay_in_t": tabt_ref[d, 2], "coef_t": tabt_ref[d, 3],
                       "chunk_decay_t": jnp.exp2(acum_t[:, end:end + 1])})
    return tables


BACKGROUND_POINTS = ((0, 0), (0, 2), (1, 0))


def _ssd_scan_chunks(scans, background):
    zero = jnp.zeros((SSD_HEADDIM, CHUNK), BF16)
    ys = [[] for _ in scans]
    next(background, None)
    for g in range(SSD_GROUPS):
        hs = slice(g * HEADS_PER_GROUP * SSD_HEADDIM, (g + 1) * HEADS_PER_GROUP * SSD_HEADDIM)
        heads = range(g * HEADS_PER_GROUP, (g + 1) * HEADS_PER_GROUP)
        stage = []
        for t, s_ref in scans:
            bm = t["bc"][:, g * D_STATE:(g + 1) * D_STATE].astype(BF16)
            cm = t["bc"][:, (SSD_GROUPS + g) * D_STATE:(SSD_GROUPS + g + 1) * D_STATE].astype(BF16)
            cbm = jnp.where(t["mask"], _bdot_nt(cm, bm), 0.0)
            state = s_ref[hs, :]
            y_off = _bdot_nt(state, cm)
            y_off = jnp.concatenate(
                [y_off[(h - heads[0]) * SSD_HEADDIM:(h - heads[0] + 1) * SSD_HEADDIM] * t["decay_in_t"][h:h + 1, :]
                 for h in heads], axis=0)
            stage.append((bm, cbm, state, y_off, []))
        for pair in range(HEADS_PER_GROUP // 2):
            if (g, pair) in BACKGROUND_POINTS:
                next(background, None)
            for (t, _), (_, cbm, _, _, y_diag) in zip(scans, stage):
                ws, xs = [], []
                for h in (heads[0] + 2 * pair, heads[0] + 2 * pair + 1):
                    a_col = t["a_col"][:, h * CHUNK:(h + 1) * CHUNK]
                    ws.append((cbm * jnp.exp2(jnp.minimum(a_col - t["acum_t"][h:h + 1, :], 0.0))).astype(BF16))
                    xs.append((t["xs_t"][h * SSD_HEADDIM:(h + 1) * SSD_HEADDIM, :] * t["dt_t"][h:h + 1, :]).astype(BF16))
                lhs = jnp.concatenate([jnp.concatenate([xs[0], zero], axis=1),
                                       jnp.concatenate([zero, xs[1]], axis=1)], axis=0)
                y_diag.append(_bdot_nt(lhs, jnp.concatenate(ws, axis=1)))
        for i, ((t, s_ref), (bm, _, state, y_off, y_diag)) in enumerate(zip(scans, stage)):
            ys[i].append(jnp.concatenate(y_diag, axis=0) + y_off)
            x_state = jnp.concatenate(
                [(t["xs_t"][h * SSD_HEADDIM:(h + 1) * SSD_HEADDIM, :] * t["coef_t"][h:h + 1, :]).astype(BF16)
                 for h in heads], axis=0)
            update = jnp.dot(x_state, bm, preferred_element_type=F32)
            for h in heads:
                hp = slice(h * SSD_HEADDIM, (h + 1) * SSD_HEADDIM)
                rel = slice((h - heads[0]) * SSD_HEADDIM, (h - heads[0] + 1) * SSD_HEADDIM)
                s_ref[hp, :] = state[rel] * t["chunk_decay_t"][h:h + 1, :] + update[rel]
    return [jnp.concatenate(y, axis=0) for y in ys]


def _ssd_kernel(xf_ref, xfp_ref, xfn_ref, xb_ref, xbp_ref, xbn_ref, dtt0f_ref, dtt0b_ref,
                dttnf_ref, dttnb_ref, init_ref, tri_ref, spread_ref, shift_ref, cw_ref, cb_ref, biast_ref,
                alogt_ref, dskip_ref, o_ref, fin_ref, xst_ref, bc_ref, sf_ref, sb_ref, tabt_ref, terms_ref,
                *, nc, has_init):
    s = pl.program_id(1)
    half = nc // 2
    first_half = s < half
    cf = s
    cb = nc - 1 - s
    rf = pl.multiple_of(cf * CHUNK, CHUNK)
    rb = pl.multiple_of(cb * CHUNK, CHUNK)

    @pl.when(first_half)
    def _():
        for c, rows, refs in ((cf, rf, (xf_ref, xfp_ref, xfn_ref)), (cb, rb, (xb_ref, xbp_ref, xbn_ref))):
            xc = _ssd_conv_chunk(*refs, cw_ref, cb_ref, shift_ref, c > 0, c < nc - 1)
            xst_ref[c] = xc[:, 0:SSD_INNER].T
            bc_ref[pl.ds(rows, CHUNK), :] = xc[:, SSD_INNER:]

    table_refs = (tri_ref, biast_ref, alogt_ref, tabt_ref, terms_ref)

    @pl.when(s == 0)
    def _():
        if has_init:
            sf_ref[...] = init_ref[0, 0, 0]
            sb_ref[...] = init_ref[0, 0, 1]
        else:
            sf_ref[...] = jnp.zeros(sf_ref.shape, F32)
            sb_ref[...] = jnp.zeros(sb_ref.shape, F32)
        for _ in _ssd_decay_table_stages((dtt0f_ref, dtt0b_ref), *table_refs):
            pass

    tf, tb = _ssd_load_decay_tables([(xst_ref[cf], bc_ref[pl.ds(rf, CHUNK), :]),
                                     (xst_ref[cb], bc_ref[pl.ds(rb, CHUNK), :])],
                                    tri_ref, spread_ref, tabt_ref, terms_ref)
    next_tables = _ssd_decay_table_stages((dttnf_ref, dttnb_ref), *table_refs)
    yf_t, yb_t = _ssd_scan_chunks([(tf, sf_ref), (tb, sb_ref)], next_tables)
    for _ in next_tables:
        pass
    yf = (yf_t + dskip_ref[...] * tf["xs_t"]).T
    yb = yb_t.T

    @pl.when(first_half)
    def _():
        o_ref[0, pl.ds(rf, CHUNK), :] = yf
        o_ref[0, pl.ds(rb, CHUNK), :] = yb

    @pl.when(jnp.logical_not(first_half))
    def _():
        o_ref[0, pl.ds(rf, CHUNK), :] = o_ref[0, pl.ds(rf, CHUNK), :] + yf
        o_ref[0, pl.ds(rb, CHUNK), :] = o_ref[0, pl.ds(rb, CHUNK), :] + yb

    @pl.when(s == nc - 1)
    def _():
        fin_ref[0, 0] = sf_ref[...]
        fin_ref[0, 1] = sb_ref[...]


def _ssd_mixer(xbc, dt_raw, init_state, layer, consts, p):
    b, L, _ = xbc.shape
    nc = L // CHUNK
    has_init = init_state is not None
    dt_t = jnp.stack([dt_raw[:, :, 0:SSD_HEADS], dt_raw[:, :, SSD_HEADS:2 * SSD_HEADS]], axis=1)
    dt_t = jnp.swapaxes(dt_t, 2, 3)
    assert nc % 2 == 0
    half = nc // 2
    if not has_init:
        init_state = jnp.zeros((1, 1, 1, HALO, LANES), F32)
        init_spec = pl.BlockSpec((1, 1, 1, HALO, LANES), lambda bi, s: (0, 0, 0, 0, 0))
    else:
        init_spec = pl.BlockSpec((1, 1, 2, SSD_INNER, D_STATE), lambda bi, s: (bi, layer, 0, 0, 0))
    per = CHUNK // HALO
    last = L // HALO - 1
    conv_f = lambda s: jnp.minimum(s, half - 1)
    conv_b = lambda s: jnp.maximum(nc - 1 - s, half)

    def conv_specs(chunk):
        return [pl.BlockSpec((1, CHUNK, CONV_CH), lambda bi, s: (bi, chunk(s), 0)),
                pl.BlockSpec((1, HALO, CONV_CH), lambda bi, s: (bi, jnp.maximum(chunk(s) * per - 1, 0), 0)),
                pl.BlockSpec((1, HALO, CONV_CH), lambda bi, s: (bi, jnp.minimum((chunk(s) + 1) * per, last), 0))]

    in_specs = conv_specs(conv_f) + conv_specs(conv_b) + [
        pl.BlockSpec((1, 1, SSD_HEADS, CHUNK), lambda bi, s: (bi, 0, 0, 0)),
        pl.BlockSpec((1, 1, SSD_HEADS, CHUNK), lambda bi, s: (bi, 1, 0, nc - 1)),
        pl.BlockSpec((1, 1, SSD_HEADS, CHUNK), lambda bi, s: (bi, 0, 0, jnp.minimum(s + 1, nc - 1))),
        pl.BlockSpec((1, 1, SSD_HEADS, CHUNK), lambda bi, s: (bi, 1, 0, jnp.maximum(nc - 2 - s, 0))),
        init_spec,
        _const_spec((2, CHUNK, CHUNK)), _const_spec((CHUNK, SSD_HEADS * CHUNK)),
        _const_spec(((CONV_W - 1) * CHUNK, SHIFT_K)), _const_spec((CONV_W, CONV_CH)), _const_spec((1, CONV_CH)),
        _const_spec((2, SSD_HEADS, 1)), _const_spec((2, SSD_HEADS, 1)), _const_spec((SSD_INNER, CHUNK)),
    ]
    out, fin = pl.pallas_call(
        functools.partial(_ssd_kernel, nc=nc, has_init=has_init),
        grid=(b, nc),
        in_specs=in_specs,
        out_specs=[pl.BlockSpec((1, L, SSD_INNER), lambda bi, s: (bi, 0, 0)),
                   pl.BlockSpec((1, 2, SSD_INNER, D_STATE), lambda bi, s: (bi, 0, 0, 0))],
        out_shape=[jax.ShapeDtypeStruct((b, L, SSD_INNER), F32),
                   jax.ShapeDtypeStruct((b, 2, SSD_INNER, D_STATE), F32)],
        scratch_shapes=[pltpu.VMEM((nc, SSD_INNER, CHUNK), F32), pltpu.VMEM((L, CONV_CH - SSD_INNER), F32),
                        pltpu.VMEM((SSD_INNER, D_STATE), F32), pltpu.VMEM((SSD_INNER, D_STATE), F32),
                        pltpu.VMEM((2, 4, SSD_HEADS, CHUNK), F32), pltpu.VMEM((2, CHUNK, CHUNK), BF16)],
        compiler_params=_params("parallel", "arbitrary"),
        name="ssd_mixer",
    )(xbc, xbc, xbc, xbc, xbc, xbc, dt_t, dt_t, dt_t, dt_t, init_state, consts["tri"], consts["spread"],
      consts["shift"], p["conv_w"], p["conv_b"], p["dt_bias_t"], p["a_log_t"], p["d_skip"])
    return out, fin


def _ssd_consts():
    i = jnp.arange(CHUNK)
    lower = (i[None, :] <= i[:, None]).astype(F32)
    t = jnp.arange(CHUNK)[:, None]
    col = jnp.arange(SHIFT_K)[None, :]
    shift = jnp.concatenate([(col == HALO + t + kk - CONV_W // 2) for kk in range(CONV_W) if kk != CONV_W // 2], axis=0)
    k = jnp.arange(CHUNK)[:, None]
    lane_head = jnp.arange(SSD_HEADS * CHUNK)[None, :] // CHUNK
    spread = (k < 3 * SSD_HEADS) & (k % SSD_HEADS == lane_head)
    return {"tri": jnp.stack([lower, lower.T]).astype(BF16), "shift": shift.astype(BF16),
            "spread": spread.astype(BF16)}


FF_CHUNK = 1024


def _merge_mlp_kernel(x_ref, mod_ref, pool_ref, attn_ref, ssd_ref, z_ref, gate_ref, g_ref, sg_ref,
                      wp_ref, wa_ref, ws_ref, wo_ref, w1_ref, w2_ref, o_ref):
    mod = mod_ref[0]
    pool_o = _bdot(pool_ref[0], wp_ref[...])
    ssd = ssd_ref[0] * _silu(z_ref[0])
    ssd = ssd * lax.rsqrt(jnp.mean(ssd * ssd, axis=-1, keepdims=True) + EPS) * sg_ref[...]
    attn_o = _bdot(attn_ref[0], wa_ref[...])
    ssd_o = _bdot(ssd, ws_ref[...])
    t = jnp.tanh(gate_ref[0])
    merged = ((pool_o + t[:, 0:D_MODEL] * pool_o) + (attn_o + t[:, D_MODEL:2 * D_MODEL] * attn_o)
              + (ssd_o + t[:, 2 * D_MODEL:] * ssd_o))
    x = x_ref[0] + mod[:, 2 * D_MODEL:3 * D_MODEL] * _bdot(merged, wo_ref[...])
    h = _rms_mod(x, g_ref[...], mod[:, 4 * D_MODEL:5 * D_MODEL], mod[:, 3 * D_MODEL:4 * D_MODEL]).astype(BF16)
    acc = jnp.zeros(x.shape, F32)
    for j in range(D_FF // FF_CHUNK):
        ff = slice(j * FF_CHUNK, (j + 1) * FF_CHUNK)
        a = jnp.dot(h, w1_ref[:, ff], preferred_element_type=F32)
        acc = acc + _bdot(jnp.square(jnp.maximum(a, 0.0)), w2_ref[ff, :])
    o_ref[0] = x + mod[:, 5 * D_MODEL:] * acc


def _merge_mlp(x, mods, layer, per_batch, pool, attn, ssd, z, gate, norm_g, ssd_norm_g, w):
    b, L, _ = x.shape
    tm = TOKEN_TILE
    tok = lambda n: pl.BlockSpec((1, tm, n), lambda bi, i: (bi, i, 0))
    weights = [w["pool_o"], w["attn_o"], w["ssd_o"], w["out"], w["mlp1"], w["mlp2"]]
    return pl.pallas_call(
        _merge_mlp_kernel,
        grid=(b, L // tm),
        in_specs=[tok(D_MODEL), _mod_spec(layer, per_batch), tok(POOL_WIDTH), tok(ATTN_WIDTH), tok(SSD_INNER),
                  tok(SSD_INNER), tok(N_BRANCH * D_MODEL), _const_spec((1, D_MODEL)), _const_spec((1, SSD_INNER))]
                 + [_const_spec(a.shape) for a in weights],
        out_specs=tok(D_MODEL),
        out_shape=jax.ShapeDtypeStruct((b, L, D_MODEL), F32),
        compiler_params=_params("parallel", "parallel"),
        name="merge_mlp",
    )(x, mods, pool, attn, ssd, z, gate, norm_g, ssd_norm_g, *weights)


def _layer(x, mods, layer, per_batch, w, p, consts, ctx):
    u, q, k, v, z, xbc, dt_raw, gate = _in_projection(
        x, mods, layer, per_batch, p["norm1_g"], p["q_norm_g"], p["k_norm_g"], consts["bm"],
        None if ctx is None else consts["rope"], w)
    pool = _pool_mixer(u, w["pool_w"], p["pool_scale"])
    if ctx is None:
        attn = _context_attention(q, k, v, p["attn_sink"])
        init = None
    else:
        cache_k, cache_v, init = ctx
        attn = _latent_attention(q, k, v, cache_k, cache_v, layer, p["attn_sink"])
    ssd, fin = _ssd_mixer(xbc, dt_raw, init, layer, consts, p)
    x = _merge_mlp(x, mods, layer, per_batch, pool, attn, ssd, z, gate, p["norm2_g"], p["ssd_norm_g"], w)
    return x, (k, v, fin)


def _layer_weights(l, w_in, pool_w, w_pool_o, w_attn_o, w_ssd_o, w_out, w_mlp1, w_mlp2):
    wi = w_in[l]
    o_z = POOL_WIDTH + ATTN_WIDTH + 2 * KV_WIDTH
    o_x = o_z + SSD_INNER
    o_d = o_x + CONV_CH
    o_g = o_d + 2 * SSD_HEADS
    wd = jnp.pad(wi[:, o_d:o_g], ((0, 0), (0, DT_PAD - 2 * SSD_HEADS)))
    cast = lambda a: a.astype(BF16)
    return {"a": cast(wi[:, :o_z]), "z": cast(wi[:, o_z:o_x]), "xbc": cast(wi[:, o_x:o_d]), "dt": cast(wd),
            "gate": cast(wi[:, o_g:] * 0.5), "pool_w": cast(pool_w[l]), "pool_o": cast(w_pool_o[l]),
            "attn_o": cast(w_attn_o[l]), "ssd_o": cast(w_ssd_o[l]), "out": cast(w_out[l] * 0.5),
            "mlp1": cast(w_mlp1[l]), "mlp2": cast(w_mlp2[l])}


def kernel(x_prompt, x_sample, cache_k, cache_v, state_ssd, c, c_ctx, w_mod, b_mod, norm1_g, norm2_g, w_in, pool_w, pool_scale, w_pool_o, q_norm_g, k_norm_g, attn_sink, w_attn_o, conv_w, conv_b, dt_bias, a_log, d_skip, ssd_norm_g, w_ssd_o, w_out, w_mlp1, w_mlp2):
    batch, seq, _ = x_prompt.shape
    dec_batch, dec_seq, _ = x_sample.shape
    past = cache_k.shape[2]
    assert 1 + dec_batch <= MOD_ROWS

    cvecs = jnp.concatenate([c_ctx[None, :], c, jnp.zeros((MOD_ROWS - 1 - dec_batch, D_MODEL), F32)], axis=0)
    mods = _modulation(cvecs, w_mod, b_mod)

    consts = {"bm": _block_mean(), "rope": _rope_tables(dec_seq)}
    consts.update(_ssd_consts())
    cache_k = cache_k.reshape(dec_batch, DEPTH, past, KV_WIDTH).astype(BF16)
    cache_v = cache_v.reshape(dec_batch, DEPTH, past, KV_WIDTH).astype(BF16)
    state = state_ssd.reshape(dec_batch, DEPTH, 2, SSD_INNER, D_STATE)

    y_prompt, y_sample = x_prompt, x_sample
    ks, vs, ss = [], [], []
    for l in range(DEPTH):
        w = _layer_weights(l, w_in, pool_w, w_pool_o, w_attn_o, w_ssd_o, w_out, w_mlp1, w_mlp2)
        p = {"norm1_g": norm1_g[l][None, :], "norm2_g": norm2_g[l][None, :], "pool_scale": pool_scale[l][None, :],
             "q_norm_g": jnp.tile(q_norm_g[l], N_Q_HEADS)[None, :], "k_norm_g": jnp.tile(k_norm_g[l], N_KV_HEADS)[None, :],
             "attn_sink": attn_sink[l], "conv_w": conv_w[l], "conv_b": conv_b[l][None, :],
             "dt_bias_t": dt_bias[l][:, :, None], "a_log_t": a_log[l][:, :, None],
             "d_skip": jnp.broadcast_to(jnp.repeat(d_skip[l], SSD_HEADDIM)[:, None], (SSD_INNER, CHUNK)),
             "ssd_norm_g": ssd_norm_g[l][None, :]}
        y_prompt, (k_l, v_l, s_l) = _layer(y_prompt, mods, l, False, w, p, consts, None)
        ks.append(k_l.reshape(batch, seq, N_KV_HEADS, HEAD_DIM))
        vs.append(v_l.reshape(batch, seq, N_KV_HEADS, HEAD_DIM))
        ss.append(s_l.reshape(batch, 2, SSD_HEADS, SSD_HEADDIM, D_STATE))
        y_sample, _ = _layer(y_sample, mods, l, True, w, p, consts, (cache_k, cache_v, state))
    return (y_prompt, y_sample, jnp.stack(ks, axis=1), jnp.stack(vs, axis=1), jnp.stack(ss, axis=1))
```
